```python
import jax, jax.numpy as jnp
from jax import lax
import numpy as np

D_MODEL = 1024
BATCH = 1
SEQ = 16384
DEPTH = 1
DEC_BATCH = 16
DEC_SEQ = 16
PAST_LEN = 4096

CHUNK = 64
N_HEADS_A = 8
N_KV_A = 1
HD_A = 128
N_HEADS_IDX = 8
D_IDX = 64
TOPK_MAX = 256
Q_BLOCK = 128
IDX_SCALE = (D_IDX ** -0.5) * (N_HEADS_IDX ** -0.5)
N_HEADS_B = 8
HD_B = 64
N_PREV_CHUNKS = 8
MAX_REL = 128
N_EXPERTS = 32
TOP_K = 4
D_FF = 1024
SWIGLU_LIMIT = 7.0
SWIGLU_ALPHA = 1.702
ROPE_THETA = 10000.0
EPS = 1e-6
NEG = -1e30

WA_Q = N_HEADS_A * HD_A
WA_KV = N_KV_A * HD_A
WI_Q = N_HEADS_IDX * D_IDX
WB = N_HEADS_B * HD_B
SPLITS = (WA_Q, WA_KV, WA_KV, WI_Q, D_IDX, N_HEADS_IDX, WB, WB, WB, D_MODEL, D_MODEL)
D_IN = sum(SPLITS)

kernel_name = 'streaming_dsa_chunkband_moe_step'


def rms_norm(x, g):
    xf = x.astype(jnp.float32)
    y = xf * lax.rsqrt(jnp.mean(xf * xf, axis=-1, keepdims=True) + EPS)
    return (y * g.astype(jnp.float32)).astype(x.dtype)


def rope(x, pos):
    d = x.shape[-1]
    inv = ROPE_THETA ** (-jnp.arange(0, d, 2, dtype=jnp.float32) / d)
    ang = pos.astype(jnp.float32)[:, None] * inv[None, :]
    cos = jnp.cos(ang)[:, None, :]
    sin = jnp.sin(ang)[:, None, :]
    xf = x.astype(jnp.float32)
    x1, x2 = xf[..., : d // 2], xf[..., d // 2:]
    return jnp.concatenate([x1 * cos - x2 * sin, x2 * cos + x1 * sin], axis=-1).astype(x.dtype)


def masked_softmax(logits, mask):
    return jax.nn.softmax(jnp.where(mask, logits, NEG), axis=-1)


def mixer_inputs(x, pos, lw):
    B, T, _ = x.shape
    h = rms_norm(x, lw['g_mix'])
    offs = np.cumsum(SPLITS)[:-1].tolist()
    qa, ka, va, qi, ki, wi, qb, kb, vb, ga, gb = jnp.split(h @ lw['w_in'], offs, axis=-1)
    qa = rope(rms_norm(qa.reshape(B, T, N_HEADS_A, HD_A), lw['g_qa']), pos)
    ka = rope(rms_norm(ka.reshape(B, T, N_KV_A, HD_A), lw['g_ka']), pos)
    va = va.reshape(B, T, N_KV_A, HD_A)
    qi = rope(qi.reshape(B, T, N_HEADS_IDX, D_IDX), pos)
    ki = rope(rms_norm(ki, lw['g_ki'])[:, :, None, :], pos)[:, :, 0, :]
    qb = rms_norm(qb.reshape(B, T, N_HEADS_B, HD_B), lw['g_qb'])
    kb = rms_norm(kb.reshape(B, T, N_HEADS_B, HD_B), lw['g_kb'])
    vb = vb.reshape(B, T, N_HEADS_B, HD_B)
    gates = jax.nn.sigmoid(jnp.concatenate([ga, gb], axis=-1) + lw['b_gate'])
    return qa, ka, va, qi, ki, wi, qb, kb, vb, gates[..., :D_MODEL], gates[..., D_MODEL:]


def dsa_attend(q, qi, wi, q_pos, k, v, kidx, k_pos, topk):
    B, T = q.shape[:2]
    rel = jax.nn.relu(jnp.einsum('bthi,bsi->bths', qi, kidx))
    score = jnp.einsum('bth,bths->bts', wi, rel) * IDX_SCALE
    admissible = (k_pos[None, :] // CHUNK) <= (q_pos[:, None] // CHUNK)
    score = jnp.where(admissible[None], score, NEG)
    _, idx = lax.top_k(score, topk)
    gather = jax.vmap(lambda rows, ids: rows[ids])
    k_sel = gather(k, idx)
    v_sel = gather(v, idx)
    valid = (k_pos[idx] // CHUNK) <= (q_pos[None, :, None] // CHUNK)
    qg = q.reshape(B, T, N_KV_A, N_HEADS_A // N_KV_A, HD_A)
    logits = jnp.einsum('btvgd,btkvd->btvgk', qg, k_sel).astype(jnp.float32) * HD_A ** -0.5
    p = masked_softmax(logits, valid[:, :, None, None, :])
    out = jnp.einsum('btvgk,btkvd->btvgd', p.astype(v.dtype), v_sel)
    return out.reshape(B, T, N_HEADS_A * HD_A)


def dsa_prompt(q, qi, wi, k, v, kidx, pos):
    B, S = q.shape[:2]
    nb = S // Q_BLOCK
    topk = min(TOPK_MAX, S // 4)

    def to_blocks(a):
        return jnp.swapaxes(a.reshape((B, nb, Q_BLOCK) + a.shape[2:]), 0, 1)

    def one_block(args):
        q_b, qi_b, wi_b, pos_b = args
        return dsa_attend(q_b, qi_b, wi_b, pos_b, k, v, kidx, pos, topk)

    out = lax.map(one_block, (to_blocks(q), to_blocks(qi), to_blocks(wi), pos.reshape(nb, Q_BLOCK)))
    return jnp.swapaxes(out, 0, 1).reshape(B, S, N_HEADS_A * HD_A)


def band_prompt(q, k, v, rel_bias):
    B, S, H, d = q.shape
    nc = S // CHUNK
    back = N_PREV_CHUNKS * CHUNK
    band = back + CHUNK
    pad = ((0, 0), (back, 0), (0, 0), (0, 0))
    sel = jnp.arange(nc)[:, None] + jnp.arange(N_PREV_CHUNKS + 1)[None, :]
    kband = jnp.pad(k, pad).reshape(B, nc + N_PREV_CHUNKS, CHUNK, H, d)[:, sel].reshape(B, nc, band, H, d)
    vband = jnp.pad(v, pad).reshape(B, nc + N_PREV_CHUNKS, CHUNK, H, d)[:, sel].reshape(B, nc, band, H, d)
    i = jnp.arange(CHUNK)
    j = jnp.arange(band)
    rel = jnp.clip(back + i[:, None] - j[None, :], -MAX_REL, MAX_REL) + MAX_REL
    bias = rel_bias.astype(jnp.float32)[:, rel]
    valid = ((jnp.arange(nc)[:, None] - N_PREV_CHUNKS) * CHUNK + j[None, :]) >= 0
    qc = q.reshape(B, nc, CHUNK, H, d)
    logits = jnp.einsum('bcihd,bcjhd->bchij', qc, kband).astype(jnp.float32) * d ** -0.5 + bias[None, None]
    p = masked_softmax(logits, valid[None, :, None, None, :])
    out = jnp.einsum('bchij,bcjhd->bcihd', p.astype(v.dtype), vband)
    return out.reshape(B, S, H * d)


def band_step(q, k, v, q_pos, k_pos, rel_bias):
    B, T, H, d = q.shape
    qc = q_pos[:, None] // CHUNK
    kc = k_pos[None, :] // CHUNK
    mask = (kc <= qc) & (kc >= qc - N_PREV_CHUNKS)
    rel = jnp.clip(q_pos[:, None] - k_pos[None, :], -MAX_REL, MAX_REL) + MAX_REL
    bias = rel_bias.astype(jnp.float32)[:, rel]
    logits = jnp.einsum('bthd,bshd->bhts', q, k).astype(jnp.float32) * d ** -0.5 + bias[None]
    p = masked_softmax(logits, mask[None, None])
    out = jnp.einsum('bhts,bshd->bthd', p.astype(v.dtype), v)
    return out.reshape(B, T, H * d)


def moe(h, lw):
    shp = h.shape
    hf = h.reshape(-1, D_MODEL)
    logits = (hf @ lw['w_router'] + lw['b_router']).astype(jnp.float32)
    top_val, top_idx = lax.top_k(logits, TOP_K)
    w = jax.nn.softmax(top_val, axis=-1)
    combine = jnp.einsum('tk,tke->te', w, jax.nn.one_hot(top_idx, N_EXPERTS, dtype=jnp.float32)).astype(h.dtype)
    out = jnp.zeros_like(hf)
    for e in range(N_EXPERTS):
        u = hf @ lw['w_up'][e] + lw['b_up'][e]
        glu = jnp.minimum(u[:, :D_FF], SWIGLU_LIMIT)
        lin = jnp.clip(u[:, D_FF:], -SWIGLU_LIMIT, SWIGLU_LIMIT)
        act = glu * jax.nn.sigmoid(SWIGLU_ALPHA * glu) * (lin + 1.0)
        out = out + combine[:, e:e + 1] * (act @ lw['w_down'][e] + lw['b_down'][e])
    return out.reshape(shp)


def merge_and_ffn(x, oa, ob, ga, gb, lw):
    m = ga * (oa @ lw['w_br_a']) + gb * (ob @ lw['w_br_b'])
    x = x + m @ lw['w_out']
    return x + moe(rms_norm(x, lw['g_ffn']), lw)


def prompt_layer(x, lw):
    B, S, _ = x.shape
    pos = jnp.arange(S, dtype=jnp.int32)
    qa, ka, va, qi, ki, wi, qb, kb, vb, ga, gb = mixer_inputs(x, pos, lw)
    oa = dsa_prompt(qa, qi, wi, ka, va, ki, pos)
    ob = band_prompt(qb, kb, vb, lw['rel_bias'])
    y = merge_and_ffn(x, oa, ob, ga, gb, lw)
    keep = min(N_PREV_CHUNKS * CHUNK, S)
    return y, (ka, va, ki, kb[:, S - keep:], vb[:, S - keep:])


def sample_layer(x, a_k, a_v, a_kidx, b_k, b_v, lw):
    B, T, _ = x.shape
    P = a_k.shape[1]
    Bc = b_k.shape[1]
    pos = P + jnp.arange(T, dtype=jnp.int32)
    qa, ka, va, qi, ki, wi, qb, kb, vb, ga, gb = mixer_inputs(x, pos, lw)
    k_all = jnp.concatenate([a_k, ka], axis=1)
    v_all = jnp.concatenate([a_v, va], axis=1)
    ki_all = jnp.concatenate([a_kidx, ki], axis=1)
    topk = min(TOPK_MAX, (P + T) // 4)
    oa = dsa_attend(qa, qi, wi, pos, k_all, v_all, ki_all, jnp.arange(P + T, dtype=jnp.int32), topk)
    kb_all = jnp.concatenate([b_k, kb], axis=1)
    vb_all = jnp.concatenate([b_v, vb], axis=1)
    kpos_b = jnp.concatenate([P - Bc + jnp.arange(Bc, dtype=jnp.int32), pos])
    ob = band_step(qb, kb_all, vb_all, pos, kpos_b, lw['rel_bias'])
    y = merge_and_ffn(x, oa, ob, ga, gb, lw)
    return y, (ka, va, ki, kb_all[:, T:], vb_all[:, T:])


def setup_inputs(seed: int = 0) -> dict:
    key = jax.random.key(seed)
    ks = jax.random.split(key, 32)

    def nrm(k, shape, scale):
        return jax.random.normal(k, shape, jnp.float32) * scale

    b_buf = min(N_PREV_CHUNKS * CHUNK, PAST_LEN)
    return {
        'x_prompt': nrm(ks[0], (BATCH, SEQ, D_MODEL), 1.0),
        'x_sample': nrm(ks[1], (DEC_BATCH, DEC_SEQ, D_MODEL), 1.0),
        'cache_a_k': nrm(ks[2], (DEPTH, DEC_BATCH, PAST_LEN, N_KV_A, HD_A), 1.0),
        'cache_a_v': nrm(ks[3], (DEPTH, DEC_BATCH, PAST_LEN, N_KV_A, HD_A), 1.0),
        'cache_a_kidx': nrm(ks[4], (DEPTH, DEC_BATCH, PAST_LEN, D_IDX), 1.0),
        'state_b_k': nrm(ks[5], (DEPTH, DEC_BATCH, b_buf, N_HEADS_B, HD_B), 1.0),
        'state_b_v': nrm(ks[6], (DEPTH, DEC_BATCH, b_buf, N_HEADS_B, HD_B), 1.0),
        'g_mix': 1.0 + nrm(ks[7], (DEPTH, D_MODEL), 0.02),
        'w_in': nrm(ks[8], (DEPTH, D_MODEL, D_IN), D_MODEL ** -0.5),
        'b_gate': nrm(ks[9], (DEPTH, 2 * D_MODEL), 0.02),
        'g_qa': 1.0 + nrm(ks[10], (DEPTH, HD_A), 0.02),
        'g_ka': 1.0 + nrm(ks[11], (DEPTH, HD_A), 0.02),
        'g_ki': 1.0 + nrm(ks[12], (DEPTH, D_IDX), 0.02),
        'g_qb': 1.0 + nrm(ks[13], (DEPTH, HD_B), 0.02),
        'g_kb': 1.0 + nrm(ks[14], (DEPTH, HD_B), 0.02),
        'rel_bias': nrm(ks[15], (DEPTH, N_HEADS_B, 2 * MAX_REL + 1), 0.1),
        'w_br_a': nrm(ks[16], (DEPTH, WA_Q, D_MODEL), WA_Q ** -0.5),
        'w_br_b': nrm(ks[17], (DEPTH, WB, D_MODEL), WB ** -0.5),
        'w_out': nrm(ks[18], (DEPTH, D_MODEL, D_MODEL), D_MODEL ** -0.5),
        'g_ffn': 1.0 + nrm(ks[19], (DEPTH, D_MODEL), 0.02),
        'w_router': nrm(ks[20], (DEPTH, D_MODEL, N_EXPERTS), D_MODEL ** -0.5),
        'b_router': nrm(ks[21], (DEPTH, N_EXPERTS), 0.01),
        'w_up': nrm(ks[22], (DEPTH, N_EXPERTS, D_MODEL, 2 * D_FF), D_MODEL ** -0.5),
        'b_up': nrm(ks[23], (DEPTH, N_EXPERTS, 2 * D_FF), 0.02),
        'w_down': nrm(ks[24], (DEPTH, N_EXPERTS, D_FF, D_MODEL), D_FF ** -0.5),
        'b_down': nrm(ks[25], (DEPTH, N_EXPERTS, D_MODEL), 0.02),
    }


def reference(x_prompt, x_sample, cache_a_k, cache_a_v, cache_a_kidx, state_b_k, state_b_v,
              g_mix, w_in, b_gate, g_qa, g_ka, g_ki, g_qb, g_kb, rel_bias, w_br_a, w_br_b, w_out,
              g_ffn, w_router, b_router, w_up, b_up, w_down, b_down):
    y_prompt = x_prompt
    y_sample = x_sample
    states_p = []
    states_s = []
    for l in range(DEPTH):
        lw = {
            'g_mix': g_mix[l], 'w_in': w_in[l], 'b_gate': b_gate[l],
            'g_qa': g_qa[l], 'g_ka': g_ka[l], 'g_ki': g_ki[l], 'g_qb': g_qb[l], 'g_kb': g_kb[l],
            'rel_bias': rel_bias[l], 'w_br_a': w_br_a[l], 'w_br_b': w_br_b[l], 'w_out': w_out[l],
            'g_ffn': g_ffn[l], 'w_router': w_router[l], 'b_router': b_router[l],
            'w_up': w_up[l], 'b_up': b_up[l], 'w_down': w_down[l], 'b_down': b_down[l],
        }
        y_prompt, st_p = prompt_layer(y_prompt, lw)
        y_sample, st_s = sample_layer(y_sample, cache_a_k[l], cache_a_v[l], cache_a_kidx[l],
                                      state_b_k[l], state_b_v[l], lw)
        states_p.append(st_p)
        states_s.append(st_s)
    a_k_p, a_v_p, a_ki_p, b_k_p, b_v_p = [jnp.stack(t) for t in zip(*states_p)]
    a_k_s, a_v_s, a_ki_s, b_k_s, b_v_s = [jnp.stack(t) for t in zip(*states_s)]
    return (y_prompt, y_sample, a_k_p, a_v_p, a_ki_p, b_k_p, b_v_p, a_k_s, a_v_s, a_ki_s, b_k_s, b_v_s)
```

```python
import functools

import numpy as np
import jax
import jax.numpy as jnp
from jax import lax
from jax.experimental import pallas as pl
from jax.experimental.pallas import tpu as pltpu

F32 = jnp.float32
BF16 = jnp.bfloat16
I32 = jnp.int32

CHUNK = 64
CHUNK_SHIFT = 6
N_HEADS_A = 8
HD_A = 128
N_HEADS_IDX = 8
D_IDX = 64
TOPK_MAX = 256
N_HEADS_B = 8
HD_B = 64
N_PREV_CHUNKS = 8
BAND_BACK = N_PREV_CHUNKS * CHUNK
MAX_REL = 128
N_EXPERTS = 32
TOP_K = 4
D_FF = 1024
SWIGLU_LIMIT = 7.0
SWIGLU_ALPHA = 1.702
ROPE_THETA = 10000.0
EPS = 1e-6
NEG = -1e30
IDX_SCALE = (D_IDX ** -0.5) * (N_HEADS_IDX ** -0.5)
LOG2E = 1.4426950408889634

LANES = 128
INT_MIN = -2 ** 31
INT_MAX = 2 ** 31 - 1
VMEM_LIMIT_BYTES = 56 * 1024 * 1024

WA_Q = N_HEADS_A * HD_A
WI_Q = N_HEADS_IDX * D_IDX
WB = N_HEADS_B * HD_B


def _pick_tile(n, target, mult):
    best = None
    for t in range(mult, min(n, target) + 1, mult):
        if n % t == 0:
            best = t
    return best if best is not None else n


def _cparams(n_axes):
    return pltpu.CompilerParams(dimension_semantics=("arbitrary",) * n_axes,
                                vmem_limit_bytes=VMEM_LIMIT_BYTES)


def _lane_iota(shape):
    return lax.broadcasted_iota(I32, shape, len(shape) - 1)


def _rms(x, g):
    ms = jnp.mean(x * x, axis=-1, keepdims=True)
    return x * lax.rsqrt(ms + EPS) * g


_C_QA = 0
_C_KA = _C_QA + WA_Q
_C_VA = _C_KA + HD_A
_C_QI = _C_VA + HD_A
_C_KI = _C_QI + WI_Q
_C_WI = _C_KI + LANES
_C_QB = _C_WI + LANES
_C_KB = _C_QB + WB
_C_VB = _C_KB + WB
_C_END = _C_VB + WB


def _proj_kernel(x_ref, gmix_ref, w_ref, gqa_ref, gka_ref, gki_ref, gqb_ref, gkb_ref,
                 cosa_ref, sina_ref, cosi_ref, sinia_ref, sinib_ref,
                 q_ref, qi_ref, wi_ref, kaf_ref, vaf_ref, kif_ref, kab_ref, vab_ref, ki2_ref,
                 qb_ref, kbf_ref, vbf_ref, kbb_ref, vbb_ref):
    x = x_ref[...]
    h = _rms(x, gmix_ref[...]).astype(BF16)

    def seg(a, b):
        return jnp.dot(h, w_ref[:, a:b], preferred_element_type=F32)

    cosa = cosa_ref[...]
    sina = sina_ref[...]
    cosi = cosi_ref[...]
    sinia = sinia_ref[...]
    sinib = sinib_ref[...]
    lane = _lane_iota((x.shape[0], LANES))
    lo_half = lane < HD_B

    def rope_a(n):
        return n * cosa + pltpu.roll(n, HD_A // 2, 1) * sina

    def rope_i(n):
        return n * cosi + pltpu.roll(n, LANES - D_IDX // 2, 1) * sinia + pltpu.roll(n, D_IDX // 2, 1) * sinib

    z = seg(_C_QA, _C_KA)
    gqa = gqa_ref[...]
    for hd in range(N_HEADS_A):
        zh = z[:, hd * HD_A:(hd + 1) * HD_A]
        q_ref[hd] = rope_a(_rms(zh, gqa)).astype(BF16)

    ka = rope_a(_rms(seg(_C_KA, _C_VA), gka_ref[...]))
    kaf_ref[...] = ka
    kab_ref[...] = ka.astype(BF16)
    va = seg(_C_VA, _C_QI)
    vaf_ref[...] = va
    vab_ref[...] = va.astype(BF16)

    zk = seg(_C_KI, _C_WI)
    ms = jnp.sum(zk * zk, axis=-1, keepdims=True) * (1.0 / D_IDX)
    ki = rope_i(zk * lax.rsqrt(ms + EPS) * gki_ref[...])
    kif_ref[...] = ki
    ki2_ref[...] = (ki + pltpu.roll(ki, D_IDX, 1)).astype(BF16)

    z = seg(_C_QI, _C_KI)
    for p in range(N_HEADS_IDX // 2):
        r = rope_i(z[:, p * LANES:(p + 1) * LANES])
        qi_ref[2 * p] = jnp.where(lo_half, r, 0.0).astype(BF16)
        qi_ref[2 * p + 1] = jnp.where(lo_half, 0.0, r).astype(BF16)

    wi_ref[...] = seg(_C_WI, _C_QB) * IDX_SCALE

    def norm_b(zb, g):
        sq = zb * zb
        s_all = jnp.sum(sq, axis=-1, keepdims=True)
        s_lo = jnp.sum(jnp.where(lo_half, sq, 0.0), axis=-1, keepdims=True)
        r_lo = lax.rsqrt(s_lo * (1.0 / HD_B) + EPS)
        r_hi = lax.rsqrt((s_all - s_lo) * (1.0 / HD_B) + EPS)
        return zb * jnp.where(lo_half, r_lo, r_hi) * g

    z = seg(_C_QB, _C_KB)
    gqb = gqb_ref[...]
    for p in range(N_HEADS_B // 2):
        n = norm_b(z[:, p * LANES:(p + 1) * LANES], gqb)
        qb_ref[2 * p] = jnp.where(lo_half, n, 0.0).astype(BF16)
        qb_ref[2 * p + 1] = jnp.where(lo_half, 0.0, n).astype(BF16)
    z = seg(_C_KB, _C_VB)
    gkb = gkb_ref[...]
    for p in range(N_HEADS_B // 2):
        n = norm_b(z[:, p * LANES:(p + 1) * LANES], gkb)
        kbf_ref[:, p * LANES:(p + 1) * LANES] = n
        kbb_ref[:, p * LANES:(p + 1) * LANES] = n.astype(BF16)
    z = seg(_C_VB, _C_END)
    vbf_ref[...] = z
    vbb_ref[...] = z.astype(BF16)


def _proj(x_all, pos_all, lw):
    t_all, d = x_all.shape
    tm = _pick_tile(t_all, 256, 16)
    w_in = lw['w_in']
    offs = np.cumsum((WA_Q, HD_A, HD_A, WI_Q, D_IDX, N_HEADS_IDX, WB, WB, WB))
    qa_w, ka_w, va_w, qi_w, ki_w, wi_w, qb_w, kb_w, vb_w = [
        w_in[:, a:b] for a, b in zip(np.concatenate([[0], offs[:-1]]), offs)]

    def padl(w):
        return jnp.pad(w, ((0, 0), (0, LANES - w.shape[1])))

    w_pack = jnp.concatenate([qa_w, ka_w, va_w, qi_w, padl(ki_w), padl(wi_w), qb_w, kb_w, vb_w],
                             axis=1).astype(BF16)

    posf = pos_all.astype(F32)[:, None]

    def tables(dh):
        inv = ROPE_THETA ** (-jnp.arange(0, dh, 2, dtype=F32) / dh)
        ang = posf * inv[None, :]
        return jnp.cos(ang), jnp.sin(ang)

    ca, sa = tables(HD_A)
    cosa = jnp.concatenate([ca, ca], axis=1)
    sina = jnp.concatenate([-sa, sa], axis=1)
    ci, si = tables(D_IDX)
    zi = jnp.zeros_like(si)
    cosi = jnp.concatenate([ci, ci, ci, ci], axis=1)
    sinia = jnp.concatenate([-si, zi, -si, zi], axis=1)
    sinib = jnp.concatenate([zi, si, zi, si], axis=1)

    row = lambda g: g.reshape(1, -1).astype(F32)
    gki = jnp.pad(lw['g_ki'], (0, LANES - D_IDX)).reshape(1, LANES)
    gqb = jnp.tile(lw['g_qb'], 2).reshape(1, LANES)
    gkb = jnp.tile(lw['g_kb'], 2).reshape(1, LANES)

    tok = lambda w: pl.BlockSpec((tm, w), lambda i: (i, 0))
    full = lambda a: pl.BlockSpec(a.shape, lambda i: (0,) * a.ndim)
    hm = pl.BlockSpec((N_HEADS_A, tm, LANES), lambda i: (0, i, 0))

    ins = [x_all, row(lw['g_mix']), w_pack, row(lw['g_qa']), row(lw['g_ka']), gki, gqb, gkb,
           cosa, sina, cosi, sinia, sinib]
    in_specs = [tok(d), full(ins[1]), full(w_pack), full(ins[3]), full(ins[4]), full(gki), full(gqb),
                full(gkb), tok(LANES), tok(LANES), tok(LANES), tok(LANES), tok(LANES)]
    sds = jax.ShapeDtypeStruct
    out_shape = [
        sds((N_HEADS_A, t_all, LANES), BF16),
        sds((N_HEADS_IDX, t_all, LANES), BF16),
        sds((t_all, LANES), F32),
        sds((t_all, HD_A), F32), sds((t_all, HD_A), F32), sds((t_all, LANES), F32),
        sds((t_all, HD_A), BF16), sds((t_all, HD_A), BF16), sds((t_all, LANES), BF16),
        sds((N_HEADS_B, t_all, LANES), BF16),
        sds((t_all, WB), F32), sds((t_all, WB), F32), sds((t_all, WB), BF16), sds((t_all, WB), BF16),
    ]
    out_specs = [hm, hm, tok(LANES), tok(HD_A), tok(HD_A), tok(LANES), tok(HD_A), tok(HD_A), tok(LANES),
                 hm, tok(WB), tok(WB), tok(WB), tok(WB)]
    return pl.pallas_call(
        _proj_kernel, grid=(t_all // tm,), in_specs=in_specs, out_specs=out_specs, out_shape=out_shape,
        compiler_params=_cparams(1), name="proj")(*ins)


def _sortable(x):
    b = pltpu.bitcast(x, I32)
    return jnp.where(b < 0, jnp.int32(INT_MIN) - b, b)


def _dsa_kernel(q_ref, qi_ref, wi_ref, ki2_ref, k_ref, v_ref, o_ref,
                keys_ref, wb_ref, lohi_ref, cnt_ref, m_ref, l_ref, acc_ref, p_ref, tiec_ref,
                *, tq, tk, nkt_max, topk, pos_base, n_valid):
    i = pl.program_id(1)
    pos0 = pos_base + i * tq
    k_end = ((pos0 + tq - 1) // CHUNK + 1) * CHUNK
    k_lim = jnp.minimum(k_end, n_valid)
    nkt = jnp.minimum((k_lim + tk - 1) // tk, nkt_max)
    ncol = tk // LANES
    nh = N_HEADS_A
    c2 = (HD_A ** -0.5) * LOG2E
    topk_f = float(topk)

    qrow = pos0 + lax.broadcasted_iota(I32, (tq, LANES), 0)
    qchunk = lax.shift_right_logical(qrow, CHUNK_SHIFT)
    lane = _lane_iota((tq, LANES))

    def admissible(j, c):
        kpos = j * tk + c * LANES + lane
        return (lax.shift_right_logical(kpos, CHUNK_SHIFT) <= qchunk) & (kpos < n_valid)

    w = wi_ref[...]
    for h in range(N_HEADS_IDX):
        wb_ref[h] = jnp.broadcast_to(w[:, h:h + 1], (tq, LANES))
    qi2d = qi_ref[...].reshape(N_HEADS_IDX * tq, LANES)

    def score_tile(j, carry):
        kt = ki2_ref[0, pl.ds(pl.multiple_of(j * tk, tk), tk), :]
        s = lax.dot_general(qi2d, kt, (((1,), (1,)), ((), ())), preferred_element_type=F32)
        s = jnp.maximum(s, 0.0)
        for c in range(ncol):
            tot = None
            for h in range(N_HEADS_IDX):
                term = wb_ref[h] * s[h * tq:(h + 1) * tq, c * LANES:(c + 1) * LANES]
                tot = term if tot is None else tot + term
            sc = jnp.where(admissible(j, c), tot, NEG)
            keys_ref[j, :, c * LANES:(c + 1) * LANES] = _sortable(sc)
        return carry

    lax.fori_loop(0, nkt, score_tile, 0)

    lohi_ref[0] = jnp.full((tq, LANES), INT_MIN, I32)
    lohi_ref[1] = jnp.full((tq, LANES), INT_MAX, I32)
    cnt_ref[0] = jnp.full((tq, LANES), 1.0, F32) * (nkt * tk).astype(F32)
    cnt_ref[1] = jnp.zeros((tq, LANES), F32)

    def bis_cond(c):
        it, act = c
        return (act > 0) & (it < 40)

    def bis_body(c):
        it, _ = c
        lo = lohi_ref[0]
        hi = lohi_ref[1]
        clo = cnt_ref[0]
        chi = cnt_ref[1]
        d = hi - lo
        active = (clo != topk_f) & (d != 1)
        mid = lo + lax.shift_right_logical(d, 1)

        def cnt_tile(j, acc):
            kk = keys_ref[j]
            for cc in range(ncol):
                acc = acc + jnp.where(kk[:, cc * LANES:(cc + 1) * LANES] >= mid, 1.0, 0.0)
            return acc

        acc = lax.fori_loop(0, nkt, cnt_tile, jnp.zeros((tq, LANES), F32))
        cnt = jnp.sum(acc, axis=1, keepdims=True)
        up = active & (cnt >= topk_f)
        dn = active & (cnt < topk_f)
        lo_n = jnp.where(up, mid, lo)
        hi_n = jnp.where(dn, mid, hi)
        clo_n = jnp.where(up, cnt, clo)
        chi_n = jnp.where(dn, cnt, chi)
        lohi_ref[0] = lo_n
        lohi_ref[1] = hi_n
        cnt_ref[0] = clo_n
        cnt_ref[1] = chi_n
        act_n = (clo_n != topk_f) & ((hi_n - lo_n) != 1)
        return it + 1, jnp.max(jnp.where(act_n, 1.0, 0.0))

    lax.while_loop(bis_cond, bis_body, (jnp.int32(0), jnp.float32(1.0)))

    lo = lohi_ref[0]
    hi = lohi_ref[1]
    need = topk_f - cnt_ref[1]
    tie_any = jnp.max(jnp.where(cnt_ref[0] > topk_f, 1.0, 0.0))

    m_ref[...] = jnp.full(m_ref.shape, NEG, F32)
    l_ref[...] = jnp.zeros(l_ref.shape, F32)
    acc_ref[...] = jnp.zeros(acc_ref.shape, F32)
    tiec_ref[...] = jnp.zeros(tiec_ref.shape, F32)
    q2d = q_ref[...].reshape(nh * tq, LANES)

    def att_tile(j, carry, *, tie):
        kk = keys_ref[j]
        off = pl.multiple_of(j * tk, tk)
        kt = k_ref[0, pl.ds(off, tk), :]
        vt = v_ref[0, pl.ds(off, tk), :]
        s_all = lax.dot_general(q2d, kt, (((1,), (1,)), ((), ())), preferred_element_type=F32)
        if tie:
            cand = [(kk[:, c * LANES:(c + 1) * LANES] >= lo) & (kk[:, c * LANES:(c + 1) * LANES] < hi)
                    for c in range(ncol)]
            candf = jnp.concatenate([jnp.where(cd, 1.0, 0.0) for cd in cand], axis=1)
            r_i = lax.broadcasted_iota(I32, (tk, tk), 0)
            c_i = lax.broadcasted_iota(I32, (tk, tk), 1)
            upper = jnp.where(r_i < c_i, 1.0, 0.0).astype(BF16)
            pref = jnp.dot(candf.astype(BF16), upper, preferred_element_type=F32)
            base = tiec_ref[...]
            sel = []
            for c in range(ncol):
                kc = kk[:, c * LANES:(c + 1) * LANES]
                rank = base + pref[:, c * LANES:(c + 1) * LANES]
                sel.append(((kc >= hi) | (cand[c] & (rank < need))) & admissible(j, c))
            tiec_ref[...] = base + jnp.sum(candf, axis=1, keepdims=True)
        else:
            sel = [(kk[:, c * LANES:(c + 1) * LANES] >= lo) & admissible(j, c) for c in range(ncol)]
        for h in range(nh):
            m_prev = m_ref[h]
            xs = [jnp.where(sel[c], s_all[h * tq:(h + 1) * tq, c * LANES:(c + 1) * LANES], NEG)
                  for c in range(ncol)]
            m_cur = xs[0]
            for c in range(1, ncol):
                m_cur = jnp.maximum(m_cur, xs[c])
            m_new = jnp.maximum(m_prev, jnp.max(m_cur, axis=1, keepdims=True))
            alpha = jnp.exp2((m_prev - m_new) * c2)
            rs = None
            for c in range(ncol):
                p = jnp.exp2((xs[c] - m_new) * c2)
                p_ref[h * tq:(h + 1) * tq, c * LANES:(c + 1) * LANES] = p.astype(BF16)
                rs = p if rs is None else rs + p
            l_ref[h] = alpha * l_ref[h] + jnp.sum(rs, axis=1, keepdims=True)
            m_ref[h] = m_new
            acc_ref[h] = acc_ref[h] * alpha
        pv = jnp.dot(p_ref[...], vt, preferred_element_type=F32)
        for h in range(nh):
            acc_ref[h] = acc_ref[h] + pv[h * tq:(h + 1) * tq]
        return carry

    @pl.when(tie_any == 0)
    def _():
        lax.fori_loop(0, nkt, functools.partial(att_tile, tie=False), 0)

    @pl.when(tie_any != 0)
    def _():
        lax.fori_loop(0, nkt, functools.partial(att_tile, tie=True), 0)

    for h in range(nh):
        o_ref[:, h * HD_A:(h + 1) * HD_A] = (acc_ref[h] / l_ref[h]).astype(o_ref.dtype)


def _dsa(q_hm, qi_hm, wi, ki2, k, v, out_init, *, n_batch, tq, n_qt, q_off, nk, topk, pos_base, n_valid):
    t_all = wi.shape[0]
    tk = _pick_tile(nk, 512, LANES)
    nkt_max = nk // tk
    kern = functools.partial(_dsa_kernel, tq=tq, tk=tk, nkt_max=nkt_max, topk=topk,
                             pos_base=pos_base, n_valid=n_valid)
    qmap = lambda b, i: (0, q_off + b * n_qt + i, 0)
    rmap = lambda b, i: (q_off + b * n_qt + i, 0)
    kmap = lambda b, i: (b, 0, 0)
    in_specs = [pl.BlockSpec((N_HEADS_A, tq, LANES), qmap), pl.BlockSpec((N_HEADS_IDX, tq, LANES), qmap),
                pl.BlockSpec((tq, LANES), rmap),
                pl.BlockSpec((1, nk, LANES), kmap), pl.BlockSpec((1, nk, LANES), kmap),
                pl.BlockSpec((1, nk, LANES), kmap)]
    args = [q_hm, qi_hm, wi, ki2, k, v]
    aliases = {}
    if out_init is not None:
        in_specs.append(pl.BlockSpec(memory_space=pl.ANY))
        args.append(out_init)
        aliases = {6: 0}
        kern_fn = lambda *refs: kern(*refs[:6], *refs[7:])
    else:
        kern_fn = kern
    scratch = [
        pltpu.VMEM((nkt_max, tq, tk), I32),
        pltpu.VMEM((N_HEADS_IDX, tq, LANES), F32),
        pltpu.VMEM((2, tq, LANES), I32),
        pltpu.VMEM((2, tq, LANES), F32),
        pltpu.VMEM((N_HEADS_A, tq, LANES), F32),
        pltpu.VMEM((N_HEADS_A, tq, LANES), F32),
        pltpu.VMEM((N_HEADS_A, tq, HD_A), F32),
        pltpu.VMEM((N_HEADS_A * tq, tk), BF16),
        pltpu.VMEM((tq, LANES), F32),
    ]
    return pl.pallas_call(
        kern_fn, grid=(n_batch, n_qt), in_specs=in_specs,
        out_specs=pl.BlockSpec((tq, WA_Q), rmap),
        out_shape=jax.ShapeDtypeStruct((t_all, WA_Q), BF16),
        scratch_shapes=scratch, input_output_aliases=aliases,
        compiler_params=_cparams(2), name="dsa")(*args)


def _band_kernel(q_ref, kp_ref, ko_ref, vp_ref, vo_ref, rext_ref, o_ref, bias_ref,
                 *, tq, tqo, n_own, off, prev_always):
    b = pl.program_id(0)
    i = pl.program_id(1)
    w = BAND_BACK + tqo
    scale = HD_B ** -0.5

    @pl.when((b == 0) & (i == 0))
    def _():
        ri = lax.broadcasted_iota(I32, (tq, w), 0)
        ci = lax.broadcasted_iota(I32, (tq, w), 1)
        qc = lax.shift_right_logical(ri, CHUNK_SHIFT)
        jo = ci - BAND_BACK
        valid_prev = (ci < BAND_BACK) & (lax.shift_right_logical(ci, CHUNK_SHIFT) >= qc)
        valid_own = (jo >= 0) & (jo < n_own) & (lax.shift_right_logical(jnp.maximum(jo, 0), CHUNK_SHIFT) <= qc)
        valid = valid_prev | valid_own
        for h in range(N_HEADS_B):
            pat = jnp.broadcast_to(rext_ref[h:h + 1, :], (tq, off + w))
            rolled = pltpu.roll(pat, 0, 1, stride=1, stride_axis=0)
            bias_ref[h] = jnp.where(valid, rolled[:, off:off + w], NEG)

    dead_cols = 0 if prev_always else jnp.where(i > 0, 0, BAND_BACK)
    lane = _lane_iota((tq, LANES))
    lo_half = lane < HD_B
    prev_dead = lax.broadcasted_iota(I32, (tq, w), 1) < dead_cols
    for p in range(N_HEADS_B // 2):
        sl = slice(p * LANES, (p + 1) * LANES)
        kcat = jnp.concatenate([kp_ref[0, :, sl], ko_ref[0, :, sl]], axis=0)
        vcat = jnp.concatenate([vp_ref[0, :, sl], vo_ref[0, :, sl]], axis=0)
        outs = []
        for e in range(2):
            h = 2 * p + e
            s = lax.dot_general(q_ref[h], kcat, (((1,), (1,)), ((), ())), preferred_element_type=F32)
            s = s * scale + bias_ref[h]
            s = jnp.where(prev_dead, NEG, s)
            m = jnp.max(s, axis=1, keepdims=True)
            pexp = jnp.exp(s - m)
            den = jnp.sum(pexp, axis=1, keepdims=True)
            pv = jnp.dot(pexp.astype(BF16), vcat, preferred_element_type=F32)
            outs.append(pv / den)
        o_ref[:, sl] = jnp.where(lo_half, outs[0], outs[1]).astype(o_ref.dtype)


def _band(qb_hm, kprev, kown, vprev, vown, rel_bias, out_init, *, n_batch, tq, tqo, n_qt, q_off, n_own,
          prev_always, prev_map, own_map):
    t_all = qb_hm.shape[1]
    off = max(tq, LANES)
    off = ((off + LANES - 1) // LANES) * LANES
    w = BAND_BACK + tqo
    u = np.arange(off + w)
    idx = np.clip(BAND_BACK + off - u, -MAX_REL, MAX_REL) + MAX_REL
    rext = rel_bias.astype(F32)[:, idx]
    kern = functools.partial(_band_kernel, tq=tq, tqo=tqo, n_own=n_own, off=off, prev_always=prev_always)
    qmap = lambda b, i: (0, q_off + b * n_qt + i, 0)
    rmap = lambda b, i: (q_off + b * n_qt + i, 0)
    in_specs = [pl.BlockSpec((N_HEADS_B, tq, LANES), qmap),
                pl.BlockSpec((1, BAND_BACK, WB), prev_map), pl.BlockSpec((1, tqo, WB), own_map),
                pl.BlockSpec((1, BAND_BACK, WB), prev_map), pl.BlockSpec((1, tqo, WB), own_map),
                pl.BlockSpec(rext.shape, lambda b, i: (0, 0))]
    args = [qb_hm, kprev, kown, vprev, vown, rext]
    aliases = {}
    if out_init is not None:
        in_specs.append(pl.BlockSpec(memory_space=pl.ANY))
        args.append(out_init)
        aliases = {6: 0}
        kern_fn = lambda *refs: kern(*refs[:6], *refs[7:])
    else:
        kern_fn = kern
    return pl.pallas_call(
        kern_fn, grid=(n_batch, n_qt), in_specs=in_specs,
        out_specs=pl.BlockSpec((tq, WB), rmap),
        out_shape=jax.ShapeDtypeStruct((t_all, WB), BF16),
        scratch_shapes=[pltpu.VMEM((N_HEADS_B, tq, w), F32)],
        input_output_aliases=aliases,
        compiler_params=_cparams(2), name="band")(*args)


def _merge_kernel(x_ref, oa_ref, ob_ref, gmix_ref, wg_ref, bg_ref, wa_ref, wb_ref, wo_ref, gffn_ref,
                  wr_ref, br_ref, x1_ref, h2_ref, comb_ref):
    x = x_ref[...]
    d = x.shape[1]
    h = _rms(x, gmix_ref[...]).astype(BF16)
    gates = jax.nn.sigmoid(jnp.dot(h, wg_ref[...], preferred_element_type=F32) + bg_ref[...])
    ya = jnp.dot(oa_ref[...], wa_ref[...], preferred_element_type=F32)
    yb = jnp.dot(ob_ref[...], wb_ref[...], preferred_element_type=F32)
    m = gates[:, :d] * ya + gates[:, d:] * yb
    x1 = x + jnp.dot(m.astype(BF16), wo_ref[...], preferred_element_type=F32)
    x1_ref[...] = x1
    h2 = _rms(x1, gffn_ref[...]).astype(BF16)
    h2_ref[...] = h2
    logits = jnp.dot(h2, wr_ref[...], preferred_element_type=F32) + br_ref[...]
    lane = _lane_iota(logits.shape)
    logits = jnp.where(lane < N_EXPERTS, logits, -jnp.inf)
    comb = jnp.zeros(logits.shape, F32)
    den = jnp.zeros((logits.shape[0], 1), F32)
    v0 = None
    for _ in range(TOP_K):
        mx = jnp.max(logits, axis=1, keepdims=True)
        idx = jnp.min(jnp.where(logits == mx, lane, LANES), axis=1, keepdims=True)
        pick = lane == idx
        if v0 is None:
            v0 = mx
        e = jnp.exp(mx - v0)
        comb = jnp.where(pick, e, comb)
        den = den + e
        logits = jnp.where(pick, -jnp.inf, logits)
    comb_ref[...] = comb / den


def _merge(x_all, oa, ob, lw):
    t_all, d = x_all.shape
    tm = _pick_tile(t_all, 256, 16)
    w_in = lw['w_in']
    w_gate = w_in[:, w_in.shape[1] - 2 * d:].astype(BF16)
    wr = jnp.pad(lw['w_router'], ((0, 0), (0, LANES - N_EXPERTS))).astype(BF16)
    br = jnp.pad(lw['b_router'], (0, LANES - N_EXPERTS)).reshape(1, LANES).astype(F32)
    row = lambda g: g.reshape(1, -1).astype(F32)
    ins = [x_all, oa, ob, row(lw['g_mix']), w_gate, row(lw['b_gate']), lw['w_br_a'].astype(BF16),
           lw['w_br_b'].astype(BF16), lw['w_out'].astype(BF16), row(lw['g_ffn']), wr, br]
    tok = lambda w: pl.BlockSpec((tm, w), lambda i: (i, 0))
    full = lambda a: pl.BlockSpec(a.shape, lambda i: (0,) * a.ndim)
    in_specs = [tok(d), tok(WA_Q), tok(WB)] + [full(a) for a in ins[3:]]
    sds = jax.ShapeDtypeStruct
    return pl.pallas_call(
        _merge_kernel, grid=(t_all // tm,), in_specs=in_specs,
        out_specs=[tok(d), tok(d), tok(LANES)],
        out_shape=[sds((t_all, d), F32), sds((t_all, d), BF16), sds((t_all, LANES), F32)],
        compiler_params=_cparams(1), name="merge")(*ins)


def _swiglu(u):
    glu = jnp.minimum(u[:, :D_FF], SWIGLU_LIMIT)
    lin = jnp.clip(u[:, D_FF:], -SWIGLU_LIMIT, SWIGLU_LIMIT)
    return glu * jax.nn.sigmoid(SWIGLU_ALPHA * glu) * (lin + 1.0)


def _moe_dense_kernel(h2_ref, x1_ref, comb_ref, wu_ref, bu_ref, wd_ref, bd_ref, y_ref, acc_ref):
    e = pl.program_id(1)

    @pl.when(e == 0)
    def _():
        acc_ref[...] = x1_ref[...]

    u = jnp.dot(h2_ref[...], wu_ref[0], preferred_element_type=F32) + bu_ref[0]
    act = _swiglu(u)
    o = jnp.dot(act.astype(BF16), wd_ref[0], preferred_element_type=F32) + bd_ref[0]
    comb = comb_ref[...]
    cw = jnp.sum(jnp.where(_lane_iota(comb.shape) == e, comb, 0.0), axis=1, keepdims=True)
    acc_ref[...] += cw * o

    @pl.when(e == pl.num_programs(1) - 1)
    def _():
        y_ref[...] = acc_ref[...]


def _moe_dense(h2, x1, comb, lw):
    t_all, d = x1.shape
    tm = _pick_tile(t_all, 640, 16)
    wu = lw['w_up'].astype(BF16)
    wd = lw['w_down'].astype(BF16)
    bu = lw['b_up'].reshape(N_EXPERTS, 1, 2 * D_FF).astype(F32)
    bd = lw['b_down'].reshape(N_EXPERTS, 1, d).astype(F32)
    tok = lambda w: pl.BlockSpec((tm, w), lambda i, e: (i, 0))
    ex = lambda a: pl.BlockSpec((1,) + a.shape[1:], lambda i, e: (e, 0, 0))
    return pl.pallas_call(
        _moe_dense_kernel, grid=(t_all // tm, N_EXPERTS),
        in_specs=[tok(d), tok(d), tok(LANES), ex(wu), ex(bu), ex(wd), ex(bd)],
        out_specs=tok(d), out_shape=jax.ShapeDtypeStruct((t_all, d), F32),
        scratch_shapes=[pltpu.VMEM((tm, d), F32)],
        compiler_params=_cparams(2), name="moe")(h2, x1, comb, wu, bu, wd, bd)


def _layer(xp, xs, a_k, a_v, a_kidx, b_k, b_v, lw):
    _, s, d = xp.shape
    bs, ts, _ = xs.shape
    p_len = a_k.shape[1]
    t_s = bs * ts
    t_all = s + t_s
    x_all = jnp.concatenate([xp.reshape(s, d), xs.reshape(t_s, d)], axis=0)
    pos_all = jnp.concatenate([jnp.arange(s, dtype=I32), jnp.tile(p_len + jnp.arange(ts, dtype=I32), bs)])

    (q_hm, qi_hm, wi, kaf, vaf, kif, kab, vab, ki2, qb_hm, kbf, vbf, kbb, vbb) = _proj(x_all, pos_all, lw)

    tq_p = _pick_tile(s, 128, CHUNK)
    oa = _dsa(q_hm, qi_hm, wi, ki2[None], kab[None], vab[None], None,
              n_batch=1, tq=tq_p, n_qt=s // tq_p, q_off=0, nk=s, topk=min(TOPK_MAX, s // 4),
              pos_base=0, n_valid=s)
    n_keys = p_len + ts
    nk_s = ((n_keys + 511) // 512) * 512
    pad_s = nk_s - n_keys

    def with_new(cache_bf, new_rows):
        return jnp.concatenate([cache_bf, new_rows.reshape(bs, ts, LANES),
                                jnp.zeros((bs, pad_s, LANES), BF16)], axis=1)

    kidx_c = a_kidx.astype(BF16)
    k_s = with_new(a_k.reshape(bs, p_len, HD_A).astype(BF16), kab[s:])
    v_s = with_new(a_v.reshape(bs, p_len, HD_A).astype(BF16), vab[s:])
    ki2_s = with_new(jnp.concatenate([kidx_c, kidx_c], axis=-1), ki2[s:])
    oa = _dsa(q_hm, qi_hm, wi, ki2_s, k_s, v_s, oa,
              n_batch=bs, tq=ts, n_qt=1, q_off=s // ts, nk=nk_s, topk=min(TOPK_MAX, n_keys // 4),
              pos_base=p_len, n_valid=n_keys)

    tq_b = BAND_BACK
    ob = _band(qb_hm, kbb[None], kbb[None], vbb[None], vbb[None], lw['rel_bias'], None,
               n_batch=1, tq=tq_b, tqo=tq_b, n_qt=s // tq_b, q_off=0, n_own=tq_b, prev_always=False,
               prev_map=lambda b, i: (0, jnp.maximum(i - 1, 0), 0), own_map=lambda b, i: (0, i, 0))
    own_pad = LANES - ts
    kown_s = jnp.pad(kbb[s:].reshape(bs, ts, WB), ((0, 0), (0, own_pad), (0, 0)))
    vown_s = jnp.pad(vbb[s:].reshape(bs, ts, WB), ((0, 0), (0, own_pad), (0, 0)))
    bk2 = b_k.reshape(bs, BAND_BACK, WB)
    bv2 = b_v.reshape(bs, BAND_BACK, WB)
    ob = _band(qb_hm, bk2.astype(BF16), kown_s, bv2.astype(BF16), vown_s, lw['rel_bias'], ob,
               n_batch=bs, tq=ts, tqo=LANES, n_qt=1, q_off=s // ts, n_own=ts, prev_always=True,
               prev_map=lambda b, i: (b, 0, 0), own_map=lambda b, i: (b, 0, 0))

    x1, h2, comb = _merge(x_all, oa, ob, lw)
    y = _moe_dense(h2, x1, comb, lw)

    keep = min(BAND_BACK, s)
    st_p = (kaf[:s].reshape(1, s, 1, HD_A), vaf[:s].reshape(1, s, 1, HD_A), kif[:s, :D_IDX].reshape(1, s, D_IDX),
            kbf[s - keep:s].reshape(1, keep, N_HEADS_B, HD_B), vbf[s - keep:s].reshape(1, keep, N_HEADS_B, HD_B))
    kb_new = kbf[s:].reshape(bs, ts, N_HEADS_B, HD_B)
    vb_new = vbf[s:].reshape(bs, ts, N_HEADS_B, HD_B)
    st_s = (kaf[s:].reshape(bs, ts, 1, HD_A), vaf[s:].reshape(bs, ts, 1, HD_A),
            kif[s:, :D_IDX].reshape(bs, ts, D_IDX),
            jnp.concatenate([b_k, kb_new], axis=1)[:, ts:], jnp.concatenate([b_v, vb_new], axis=1)[:, ts:])
    return y[:s].reshape(1, s, d), y[s:].reshape(bs, ts, d), st_p, st_s


def kernel(x_prompt, x_sample, cache_a_k, cache_a_v, cache_a_kidx, state_b_k, state_b_v,
           g_mix, w_in, b_gate, g_qa, g_ka, g_ki, g_qb, g_kb, rel_bias, w_br_a, w_br_b, w_out,
           g_ffn, w_router, b_router, w_up, b_up, w_down, b_down):
    assert x_prompt.shape[0] == 1, "prompt batch is folded into the token axis; one stream supported"
    depth = g_mix.shape[0]
    yp, ys = x_prompt, x_sample
    states_p, states_s = [], []
    for l in range(depth):
        lw = dict(g_mix=g_mix[l], w_in=w_in[l], b_gate=b_gate[l], g_qa=g_qa[l], g_ka=g_ka[l], g_ki=g_ki[l],
                  g_qb=g_qb[l], g_kb=g_kb[l], rel_bias=rel_bias[l], w_br_a=w_br_a[l], w_br_b=w_br_b[l],
                  w_out=w_out[l], g_ffn=g_ffn[l], w_router=w_router[l], b_router=b_router[l],
                  w_up=w_up[l], b_up=b_up[l], w_down=w_down[l], b_down=b_down[l])
        yp, ys, st_p, st_s = _layer(yp, ys, cache_a_k[l], cache_a_v[l], cache_a_kidx[l],
                                    state_b_k[l], state_b_v[l], lw)
        states_p.append(st_p)
        states_s.append(st_s)
    a_k_p, a_v_p, a_ki_p, b_k_p, b_v_p = [jnp.stack(t) for t in zip(*states_p)]
    a_k_s, a_v_s, a_ki_s, b_k_s, b_v_s = [jnp.stack(t) for t in zip(*states_s)]
    return (yp, ys, a_k_p, a_v_p, a_ki_p, b_k_p, b_v_p, a_k_s, a_v_s, a_ki_s, b_k_s, b_v_s)
```

```python
import functools

import numpy as np
import jax
import jax.numpy as jnp
from jax import lax
from jax.experimental import pallas as pl
from jax.experimental.pallas import tpu as pltpu

F32 = jnp.float32
BF16 = jnp.bfloat16
I32 = jnp.int32

CHUNK = 64
CHUNK_SHIFT = 6
N_HEADS_A = 8
HD_A = 128
N_HEADS_IDX = 8
D_IDX = 64
TOPK_MAX = 256
N_HEADS_B = 8
HD_B = 64
N_PREV_CHUNKS = 8
BAND_BACK = N_PREV_CHUNKS * CHUNK
MAX_REL = 128
N_EXPERTS = 32
TOP_K = 4
D_FF = 1024
SWIGLU_LIMIT = 7.0
SWIGLU_ALPHA = 1.702
ROPE_THETA = 10000.0
EPS = 1e-6
NEG = -1e30
IDX_SCALE = (D_IDX ** -0.5) * (N_HEADS_IDX ** -0.5)
LOG2E = 1.4426950408889634
QK_SCALE_LOG2E = (HD_A ** -0.5) * LOG2E

LANES = 128
INT_MIN = -2 ** 31
INT_MAX = 2 ** 31 - 1
VMEM_LIMIT_BYTES = 56 * 1024 * 1024

WA_Q = N_HEADS_A * HD_A
WI_Q = N_HEADS_IDX * D_IDX
WB = N_HEADS_B * HD_B


def _pick_tile(n, target, mult):
    best = None
    for t in range(mult, min(n, target) + 1, mult):
        if n % t == 0:
            best = t
    return best if best is not None else n


def _cparams(n_axes):
    return pltpu.CompilerParams(dimension_semantics=("arbitrary",) * n_axes,
                                vmem_limit_bytes=VMEM_LIMIT_BYTES)


def _lane_iota(shape):
    return lax.broadcasted_iota(I32, shape, len(shape) - 1)


def _rms(x, g):
    ms = jnp.mean(x * x, axis=-1, keepdims=True)
    return x * lax.rsqrt(ms + EPS) * g


_C_QA = 0
_C_KA = _C_QA + WA_Q
_C_VA = _C_KA + HD_A
_C_QI = _C_VA + HD_A
_C_KI = _C_QI + WI_Q
_C_WI = _C_KI + LANES
_C_QB = _C_WI + LANES
_C_KB = _C_QB + WB
_C_VB = _C_KB + WB
_C_END = _C_VB + WB


def _proj_kernel(x_ref, gmix_ref, w_ref, gqa_ref, gka_ref, gki_ref, gqb_ref, gkb_ref,
                 cosa_ref, sina_ref, cosi_ref, sinia_ref, sinib_ref,
                 q_ref, qi_ref, wi_ref, kaf_ref, vaf_ref, kif_ref, kab_ref, vab_ref, ki2_ref,
                 qb_ref, kbf_ref, vbf_ref, kbb_ref, vbb_ref):
    x = x_ref[...]
    h = _rms(x, gmix_ref[...]).astype(BF16)

    def seg(a, b):
        return jnp.dot(h, w_ref[:, a:b], preferred_element_type=F32)

    cosa = cosa_ref[...]
    sina = sina_ref[...]
    cosi = cosi_ref[...]
    sinia = sinia_ref[...]
    sinib = sinib_ref[...]
    lane = _lane_iota((x.shape[0], LANES))
    lo_half = lane < HD_B

    def rope_a(n):
        return n * cosa + pltpu.roll(n, HD_A // 2, 1) * sina

    def rope_i(n):
        return n * cosi + pltpu.roll(n, LANES - D_IDX // 2, 1) * sinia + pltpu.roll(n, D_IDX // 2, 1) * sinib

    z = seg(_C_QA, _C_KA)
    gqa = gqa_ref[...]
    for hd in range(N_HEADS_A):
        zh = z[:, hd * HD_A:(hd + 1) * HD_A]
        q_ref[hd] = (rope_a(_rms(zh, gqa)) * QK_SCALE_LOG2E).astype(BF16)

    ka = rope_a(_rms(seg(_C_KA, _C_VA), gka_ref[...]))
    kaf_ref[...] = ka
    kab_ref[...] = ka.astype(BF16)
    va = seg(_C_VA, _C_QI)
    vaf_ref[...] = va
    vab_ref[...] = va.astype(BF16)

    zk = seg(_C_KI, _C_WI)
    ms = jnp.sum(zk * zk, axis=-1, keepdims=True) * (1.0 / D_IDX)
    ki = rope_i(zk * lax.rsqrt(ms + EPS) * gki_ref[...])
    kif_ref[...] = ki
    ki2_ref[...] = (ki + pltpu.roll(ki, D_IDX, 1)).astype(BF16)

    z = seg(_C_QI, _C_KI)
    for p in range(N_HEADS_IDX // 2):
        r = rope_i(z[:, p * LANES:(p + 1) * LANES])
        qi_ref[2 * p] = jnp.where(lo_half, r, 0.0).astype(BF16)
        qi_ref[2 * p + 1] = jnp.where(lo_half, 0.0, r).astype(BF16)

    wi_ref[...] = seg(_C_WI, _C_QB) * IDX_SCALE

    def norm_b(zb, g):
        sq = zb * zb
        s_all = jnp.sum(sq, axis=-1, keepdims=True)
        s_lo = jnp.sum(jnp.where(lo_half, sq, 0.0), axis=-1, keepdims=True)
        r_lo = lax.rsqrt(s_lo * (1.0 / HD_B) + EPS)
        r_hi = lax.rsqrt((s_all - s_lo) * (1.0 / HD_B) + EPS)
        return zb * jnp.where(lo_half, r_lo, r_hi) * g

    z = seg(_C_QB, _C_KB)
    gqb = gqb_ref[...]
    for p in range(N_HEADS_B // 2):
        n = norm_b(z[:, p * LANES:(p + 1) * LANES], gqb)
        qb_ref[2 * p] = jnp.where(lo_half, n, 0.0).astype(BF16)
        qb_ref[2 * p + 1] = jnp.where(lo_half, 0.0, n).astype(BF16)
    z = seg(_C_KB, _C_VB)
    gkb = gkb_ref[...]
    for p in range(N_HEADS_B // 2):
        n = norm_b(z[:, p * LANES:(p + 1) * LANES], gkb)
        kbf_ref[:, p * LANES:(p + 1) * LANES] = n
        kbb_ref[:, p * LANES:(p + 1) * LANES] = n.astype(BF16)
    z = seg(_C_VB, _C_END)
    vbf_ref[...] = z
    vbb_ref[...] = z.astype(BF16)


def _proj(x_all, pos_all, lw):
    t_all, d = x_all.shape
    tm = _pick_tile(t_all, 256, 16)
    w_in = lw['w_in']
    offs = np.cumsum((WA_Q, HD_A, HD_A, WI_Q, D_IDX, N_HEADS_IDX, WB, WB, WB))
    qa_w, ka_w, va_w, qi_w, ki_w, wi_w, qb_w, kb_w, vb_w = [
        w_in[:, a:b] for a, b in zip(np.concatenate([[0], offs[:-1]]), offs)]

    def padl(w):
        return jnp.pad(w, ((0, 0), (0, LANES - w.shape[1])))

    w_pack = jnp.concatenate([qa_w, ka_w, va_w, qi_w, padl(ki_w), padl(wi_w), qb_w, kb_w, vb_w],
                             axis=1).astype(BF16)

    posf = pos_all.astype(F32)[:, None]

    def tables(dh):
        inv = ROPE_THETA ** (-jnp.arange(0, dh, 2, dtype=F32) / dh)
        ang = posf * inv[None, :]
        return jnp.cos(ang), jnp.sin(ang)

    ca, sa = tables(HD_A)
    cosa = jnp.concatenate([ca, ca], axis=1)
    sina = jnp.concatenate([-sa, sa], axis=1)
    ci, si = tables(D_IDX)
    zi = jnp.zeros_like(si)
    cosi = jnp.concatenate([ci, ci, ci, ci], axis=1)
    sinia = jnp.concatenate([-si, zi, -si, zi], axis=1)
    sinib = jnp.concatenate([zi, si, zi, si], axis=1)

    row = lambda g: g.reshape(1, -1).astype(F32)
    gki = jnp.pad(lw['g_ki'], (0, LANES - D_IDX)).reshape(1, LANES)
    gqb = jnp.tile(lw['g_qb'], 2).reshape(1, LANES)
    gkb = jnp.tile(lw['g_kb'], 2).reshape(1, LANES)

    tok = lambda w: pl.BlockSpec((tm, w), lambda i: (i, 0))
    full = lambda a: pl.BlockSpec(a.shape, lambda i: (0,) * a.ndim)
    hm = pl.BlockSpec((N_HEADS_A, tm, LANES), lambda i: (0, i, 0))

    ins = [x_all, row(lw['g_mix']), w_pack, row(lw['g_qa']), row(lw['g_ka']), gki, gqb, gkb,
           cosa, sina, cosi, sinia, sinib]
    in_specs = [tok(d), full(ins[1]), full(w_pack), full(ins[3]), full(ins[4]), full(gki), full(gqb),
                full(gkb), tok(LANES), tok(LANES), tok(LANES), tok(LANES), tok(LANES)]
    sds = jax.ShapeDtypeStruct
    out_shape = [
        sds((N_HEADS_A, t_all, LANES), BF16),
        sds((N_HEADS_IDX, t_all, LANES), BF16),
        sds((t_all, LANES), F32),
        sds((t_all, HD_A), F32), sds((t_all, HD_A), F32), sds((t_all, LANES), F32),
        sds((t_all, HD_A), BF16), sds((t_all, HD_A), BF16), sds((t_all, LANES), BF16),
        sds((N_HEADS_B, t_all, LANES), BF16),
        sds((t_all, WB), F32), sds((t_all, WB), F32), sds((t_all, WB), BF16), sds((t_all, WB), BF16),
    ]
    out_specs = [hm, hm, tok(LANES), tok(HD_A), tok(HD_A), tok(LANES), tok(HD_A), tok(HD_A), tok(LANES),
                 hm, tok(WB), tok(WB), tok(WB), tok(WB)]
    return pl.pallas_call(
        _proj_kernel, grid=(t_all // tm,), in_specs=in_specs, out_specs=out_specs, out_shape=out_shape,
        compiler_params=_cparams(1), name="proj")(*ins)


BIS_UNROLL = 4


def _sortable(x):
    b = pltpu.bitcast(x, I32)
    return jnp.where(b < 0, jnp.int32(INT_MIN) - b, b)


def _dsa_kernel(q_ref, qi_ref, wi_ref, ki2_ref, k_ref, v_ref, o_ref,
                keys_ref, wb_ref, lohi_ref, cnt_ref, m_ref, alpha_ref, acc_ref, s_ref, p_ref, tiec_ref,
                *, tq, tk, nkt_max, topk, pos_base, n_valid):
    i = pl.program_id(1)
    pos0 = pos_base + i * tq
    k_end = ((pos0 + tq - 1) // CHUNK + 1) * CHUNK
    k_lim = jnp.minimum(k_end, n_valid)
    npair = jnp.minimum((k_lim + 2 * tk - 1) // (2 * tk), nkt_max // 2)
    nkt = 2 * npair
    ncol = tk // LANES
    nh = N_HEADS_A
    topk_f = float(topk)

    qrow = pos0 + lax.broadcasted_iota(I32, (tq, LANES), 0)
    qchunk = lax.shift_right_logical(qrow, CHUNK_SHIFT)
    lane = _lane_iota((tq, LANES))

    def admissible(j, c):
        kpos = j * tk + c * LANES + lane
        return (lax.shift_right_logical(kpos, CHUNK_SHIFT) <= qchunk) & (kpos < n_valid)

    w = wi_ref[...]
    for h in range(N_HEADS_IDX):
        wb_ref[h] = jnp.broadcast_to(w[:, h:h + 1], (tq, LANES))
    qi2d = qi_ref[...].reshape(N_HEADS_IDX * tq, LANES)

    dn_t = (((1,), (1,)), ((), ()))

    def idx_dots(j):
        kt = ki2_ref[0, pl.ds(pl.multiple_of(j * tk, tk), tk), :]
        return lax.dot_general(qi2d, kt, dn_t, preferred_element_type=F32)

    def score_tile(j, slot):
        for c in range(ncol):
            tot = None
            for h in range(N_HEADS_IDX):
                r = jnp.maximum(s_ref[slot, h * tq:(h + 1) * tq, c * LANES:(c + 1) * LANES], 0.0)
                term = wb_ref[h] * r
                tot = term if tot is None else tot + term
            sc = jnp.where(admissible(j, c), tot, NEG)
            keys_ref[j, :, c * LANES:(c + 1) * LANES] = _sortable(sc)

    s_ref[0] = idx_dots(0)

    def score_pair(jj, carry):
        a = 2 * jj
        s_ref[1] = idx_dots(a + 1)
        score_tile(a, 0)
        s_ref[0] = idx_dots(jnp.minimum(a + 2, nkt - 2))
        score_tile(a + 1, 1)
        return carry

    lax.fori_loop(0, npair, score_pair, 0)

    lohi_ref[0] = jnp.full((tq, LANES), INT_MIN, I32)
    lohi_ref[1] = jnp.full((tq, LANES), INT_MAX, I32)
    cnt_ref[0] = jnp.full((tq, LANES), 1.0, F32) * (nkt * tk).astype(F32)
    cnt_ref[1] = jnp.zeros((tq, LANES), F32)

    def bis_cond(c):
        it, act = c
        return (act > 0) & (it < 40 // BIS_UNROLL)

    def bis_step():
        lo = lohi_ref[0]
        hi = lohi_ref[1]
        clo = cnt_ref[0]
        chi = cnt_ref[1]
        d = hi - lo
        active = (clo != topk_f) & (d != 1)
        mid = lo + lax.shift_right_logical(d, 1)

        def cnt_tile(j, acc):
            kk = keys_ref[j]
            for cc in range(ncol):
                acc = acc + jnp.where(kk[:, cc * LANES:(cc + 1) * LANES] >= mid, 1.0, 0.0)
            return acc

        acc = lax.fori_loop(0, nkt, cnt_tile, jnp.zeros((tq, LANES), F32))
        cnt = jnp.sum(acc, axis=1, keepdims=True)
        up = active & (cnt >= topk_f)
        dn = active & (cnt < topk_f)
        lo_n = jnp.where(up, mid, lo)
        hi_n = jnp.where(dn, mid, hi)
        clo_n = jnp.where(up, cnt, clo)
        chi_n = jnp.where(dn, cnt, chi)
        lohi_ref[0] = lo_n
        lohi_ref[1] = hi_n
        cnt_ref[0] = clo_n
        cnt_ref[1] = chi_n
        return (clo_n != topk_f) & ((hi_n - lo_n) != 1)

    def bis_body(c):
        it, _ = c
        for _ in range(BIS_UNROLL):
            act_n = bis_step()
        return it + 1, jnp.max(jnp.where(act_n, 1.0, 0.0))

    lax.while_loop(bis_cond, bis_body, (jnp.int32(0), jnp.float32(1.0)))

    lo = lohi_ref[0]
    hi = lohi_ref[1]
    need = topk_f - cnt_ref[1]
    tie_any = jnp.max(jnp.where(cnt_ref[0] > topk_f, 1.0, 0.0))

    m_ref[...] = jnp.full(m_ref.shape, NEG, F32)
    acc_ref[...] = jnp.zeros(acc_ref.shape, F32)
    tiec_ref[...] = jnp.zeros(tiec_ref.shape, F32)
    q2d = q_ref[...].reshape(nh * tq, LANES)
    ones_col = jnp.where(_lane_iota((tk, LANES)) == 0, 1.0, 0.0).astype(BF16)

    def qk_dots(j):
        kt = k_ref[0, pl.ds(pl.multiple_of(j * tk, tk), tk), :]
        return lax.dot_general(q2d, kt, dn_t, preferred_element_type=F32)

    def softmax_tile(j, slot, tie):
        if tie:
            kk = keys_ref[j]
            cand = [(kk[:, c * LANES:(c + 1) * LANES] >= lo) & (kk[:, c * LANES:(c + 1) * LANES] < hi)
                    for c in range(ncol)]
            candf = jnp.concatenate([jnp.where(cd, 1.0, 0.0) for cd in cand], axis=1)
            r_i = lax.broadcasted_iota(I32, (tk, tk), 0)
            c_i = lax.broadcasted_iota(I32, (tk, tk), 1)
            upper = jnp.where(r_i < c_i, 1.0, 0.0).astype(BF16)
            pref = jnp.dot(candf.astype(BF16), upper, preferred_element_type=F32)
            base = tiec_ref[...]
            sel = []
            for c in range(ncol):
                kc = kk[:, c * LANES:(c + 1) * LANES]
                rank = base + pref[:, c * LANES:(c + 1) * LANES]
                sel.append(((kc >= hi) | (cand[c] & (rank < need))) & admissible(j, c))
            tiec_ref[...] = base + jnp.sum(candf, axis=1, keepdims=True)
        else:
            rb = min(tq, 32)
            lane_r = _lane_iota((rb, LANES))
            for r0 in range(0, tq, rb):
                rs = slice(r0, r0 + rb)
                lo_r = lohi_ref[0, rs]
                qc_r = lax.shift_right_logical(pos0 + r0 + lax.broadcasted_iota(I32, (rb, LANES), 0), CHUNK_SHIFT)
                sel = []
                for c in range(ncol):
                    kpos = j * tk + c * LANES + lane_r
                    adm = (lax.shift_right_logical(kpos, CHUNK_SHIFT) <= qc_r) & (kpos < n_valid)
                    sel.append((keys_ref[j, rs, c * LANES:(c + 1) * LANES] >= lo_r) & adm)
                for h in range(nh):
                    hr = slice(h * tq + r0, h * tq + r0 + rb)
                    m_prev = m_ref[h, rs]
                    xs = [jnp.where(sel[c], s_ref[slot, hr, c * LANES:(c + 1) * LANES], NEG) for c in range(ncol)]
                    m_cur = xs[0]
                    for c in range(1, ncol):
                        m_cur = jnp.maximum(m_cur, xs[c])
                    m_new = jnp.maximum(m_prev, jnp.max(m_cur, axis=1, keepdims=True))
                    alpha_ref[slot, h, rs] = jnp.exp2(m_prev - m_new)
                    for c in range(ncol):
                        p_ref[slot, hr, c * LANES:(c + 1) * LANES] = jnp.exp2(xs[c] - m_new).astype(BF16)
                    m_ref[h, rs] = m_new
            return
        for h in range(nh):
            m_prev = m_ref[h]
            xs = [jnp.where(sel[c], s_ref[slot, h * tq:(h + 1) * tq, c * LANES:(c + 1) * LANES], NEG)
                  for c in range(ncol)]
            m_cur = xs[0]
            for c in range(1, ncol):
                m_cur = jnp.maximum(m_cur, xs[c])
            m_new = jnp.maximum(m_prev, jnp.max(m_cur, axis=1, keepdims=True))
            alpha_ref[slot, h] = jnp.exp2(m_prev - m_new)
            for c in range(ncol):
                p_ref[slot, h * tq:(h + 1) * tq, c * LANES:(c + 1) * LANES] = (
                    jnp.exp2(xs[c] - m_new).astype(BF16))
            m_ref[h] = m_new

    def pv_tile(j, slot):
        vt = v_ref[0, pl.ds(pl.multiple_of(j * tk, tk), tk), :]
        pv = jnp.dot(p_ref[slot], jnp.concatenate([vt, ones_col], axis=1), preferred_element_type=F32)
        for h in range(nh):
            alpha = alpha_ref[slot, h]
            for half in range(2):
                hs = slice(half * HD_A, (half + 1) * HD_A)
                acc_ref[h, :, hs] = acc_ref[h, :, hs] * alpha + pv[h * tq:(h + 1) * tq, hs]

    def attend(tie):
        s_ref[0] = qk_dots(0)
        p_ref[1] = jnp.zeros(p_ref.shape[1:], BF16)
        alpha_ref[1] = jnp.ones(alpha_ref.shape[1:], F32)

        def pair(jj, carry):
            a = 2 * jj
            s_ref[1] = qk_dots(a + 1)
            softmax_tile(a, 0, tie)
            pv_tile(jnp.maximum(a - 1, 0), 1)
            s_ref[0] = qk_dots(jnp.minimum(a + 2, nkt - 2))
            softmax_tile(a + 1, 1, tie)
            pv_tile(a, 0)
            return carry

        lax.fori_loop(0, npair, pair, 0)
        pv_tile(nkt - 1, 1)

    @pl.when(tie_any == 0)
    def _():
        attend(False)

    @pl.when(tie_any != 0)
    def _():
        attend(True)

    for h in range(nh):
        den = acc_ref[h, :, HD_A:HD_A + 1]
        o_ref[:, h * HD_A:(h + 1) * HD_A] = (acc_ref[h, :, :HD_A] / den).astype(o_ref.dtype)


def _dsa(q_hm, qi_hm, wi, ki2, k, v, *, n_batch, tq, n_qt, q_off, nk, topk, pos_base, n_valid):
    tk = _pick_tile(nk, 512, LANES)
    nkt_max = nk // tk
    assert nkt_max % 2 == 0 and nk >= topk, "key tiles are processed in pairs"
    kern = functools.partial(_dsa_kernel, tq=tq, tk=tk, nkt_max=nkt_max, topk=topk,
                             pos_base=pos_base, n_valid=n_valid)
    qmap = lambda b, i: (0, q_off + b * n_qt + i, 0)
    rmap = lambda b, i: (q_off + b * n_qt + i, 0)
    kmap = lambda b, i: (b, 0, 0)
    in_specs = [pl.BlockSpec((N_HEADS_A, tq, LANES), qmap), pl.BlockSpec((N_HEADS_IDX, tq, LANES), qmap),
                pl.BlockSpec((tq, LANES), rmap),
                pl.BlockSpec((1, nk, LANES), kmap), pl.BlockSpec((1, nk, LANES), kmap),
                pl.BlockSpec((1, nk, LANES), kmap)]
    scratch = [
        pltpu.VMEM((nkt_max, tq, tk), I32),
        pltpu.VMEM((N_HEADS_IDX, tq, LANES), F32),
        pltpu.VMEM((2, tq, LANES), I32),
        pltpu.VMEM((2, tq, LANES), F32),
        pltpu.VMEM((N_HEADS_A, tq, LANES), F32),
        pltpu.VMEM((2, N_HEADS_A, tq, LANES), F32),
        pltpu.VMEM((N_HEADS_A, tq, 2 * HD_A), F32),
        pltpu.VMEM((2, N_HEADS_A * tq, tk), F32),
        pltpu.VMEM((2, N_HEADS_A * tq, tk), BF16),
        pltpu.VMEM((tq, LANES), F32),
    ]
    return pl.pallas_call(
        kern, grid=(n_batch, n_qt), in_specs=in_specs,
        out_specs=pl.BlockSpec((tq, WA_Q), lambda b, i: (b * n_qt + i, 0)),
        out_shape=jax.ShapeDtypeStruct((n_batch * n_qt * tq, WA_Q), BF16),
        scratch_shapes=scratch,
        compiler_params=_cparams(2), name="dsa")(q_hm, qi_hm, wi, ki2, k, v)


def _band_kernel(q_ref, kp_ref, ko_ref, vp_ref, vo_ref, rext_ref, o_ref, bias_ref,
                 *, tq, tqo, n_own, off, prev_always):
    b = pl.program_id(0)
    i = pl.program_id(1)
    w = BAND_BACK + tqo
    scale = HD_B ** -0.5

    @pl.when((b == 0) & (i == 0))
    def _():
        ri = lax.broadcasted_iota(I32, (tq, w), 0)
        ci = lax.broadcasted_iota(I32, (tq, w), 1)
        qc = lax.shift_right_logical(ri, CHUNK_SHIFT)
        jo = ci - BAND_BACK
        valid_prev = (ci < BAND_BACK) & (lax.shift_right_logical(ci, CHUNK_SHIFT) >= qc)
        valid_own = (jo >= 0) & (jo < n_own) & (lax.shift_right_logical(jnp.maximum(jo, 0), CHUNK_SHIFT) <= qc)
        valid = valid_prev | valid_own
        for h in range(N_HEADS_B):
            pat = jnp.broadcast_to(rext_ref[h:h + 1, :], (tq, off + w))
            rolled = pltpu.roll(pat, 0, 1, stride=1, stride_axis=0)
            bias_ref[h] = jnp.where(valid, rolled[:, off:off + w], NEG)

    dead_cols = 0 if prev_always else jnp.where(i > 0, 0, BAND_BACK)
    lane = _lane_iota((tq, LANES))
    lo_half = lane < HD_B
    prev_dead = lax.broadcasted_iota(I32, (tq, w), 1) < dead_cols
    for p in range(N_HEADS_B // 2):
        sl = slice(p * LANES, (p + 1) * LANES)
        kcat = jnp.concatenate([kp_ref[0, :, sl], ko_ref[0, :, sl]], axis=0)
        vcat = jnp.concatenate([vp_ref[0, :, sl], vo_ref[0, :, sl]], axis=0)
        outs = []
        for e in range(2):
            h = 2 * p + e
            s = lax.dot_general(q_ref[h], kcat, (((1,), (1,)), ((), ())), preferred_element_type=F32)
            s = s * scale + bias_ref[h]
            s = jnp.where(prev_dead, NEG, s)
            m = jnp.max(s, axis=1, keepdims=True)
            pexp = jnp.exp(s - m)
            den = jnp.sum(pexp, axis=1, keepdims=True)
            pv = jnp.dot(pexp.astype(BF16), vcat, preferred_element_type=F32)
            outs.append(pv / den)
        o_ref[:, sl] = jnp.where(lo_half, outs[0], outs[1]).astype(o_ref.dtype)


def _band(qb_hm, kprev, kown, vprev, vown, rel_bias, *, n_batch, tq, tqo, n_qt, q_off, n_own,
          prev_always, prev_map, own_map):
    off = max(tq, LANES)
    off = ((off + LANES - 1) // LANES) * LANES
    w = BAND_BACK + tqo
    u = np.arange(off + w)
    idx = np.clip(BAND_BACK + off - u, -MAX_REL, MAX_REL) + MAX_REL
    rext = rel_bias.astype(F32)[:, idx]
    kern = functools.partial(_band_kernel, tq=tq, tqo=tqo, n_own=n_own, off=off, prev_always=prev_always)
    qmap = lambda b, i: (0, q_off + b * n_qt + i, 0)
    in_specs = [pl.BlockSpec((N_HEADS_B, tq, LANES), qmap),
                pl.BlockSpec((1, BAND_BACK, WB), prev_map), pl.BlockSpec((1, tqo, WB), own_map),
                pl.BlockSpec((1, BAND_BACK, WB), prev_map), pl.BlockSpec((1, tqo, WB), own_map),
                pl.BlockSpec(rext.shape, lambda b, i: (0, 0))]
    return pl.pallas_call(
        kern, grid=(n_batch, n_qt), in_specs=in_specs,
        out_specs=pl.BlockSpec((tq, WB), lambda b, i: (b * n_qt + i, 0)),
        out_shape=jax.ShapeDtypeStruct((n_batch * n_qt * tq, WB), BF16),
        scratch_shapes=[pltpu.VMEM((N_HEADS_B, tq, w), F32)],
        compiler_params=_cparams(2), name="band")(qb_hm, kprev, kown, vprev, vown, rext)


def _merge_kernel(x_ref, oap_ref, oas_ref, obp_ref, obs_ref, gmix_ref, wg_ref, bg_ref, wa_ref, wb_ref, wo_ref,
                  gffn_ref, wr_ref, br_ref, x1_ref, h2_ref, route_ref, wts_ref, cnt_ref, carry_ref, oa_ref, ob_ref,
                  *, n_prompt_tiles):
    @pl.when(pl.program_id(0) == 0)
    def _():
        carry_ref[...] = jnp.zeros(carry_ref.shape, F32)

    x = x_ref[...]
    d = x.shape[1]
    h = _rms(x, gmix_ref[...]).astype(BF16)
    gates = jax.nn.sigmoid(jnp.dot(h, wg_ref[...], preferred_element_type=F32) + bg_ref[...])
    @pl.when(pl.program_id(0) < n_prompt_tiles)
    def _():
        oa_ref[...] = oap_ref[...]
        ob_ref[...] = obp_ref[...]

    @pl.when(pl.program_id(0) >= n_prompt_tiles)
    def _():
        oa_ref[...] = oas_ref[...]
        ob_ref[...] = obs_ref[...]

    ya = jnp.dot(oa_ref[...], wa_ref[...], preferred_element_type=F32)
    yb = jnp.dot(ob_ref[...], wb_ref[...], preferred_element_type=F32)
    m = gates[:, :d] * ya + gates[:, d:] * yb
    x1 = x + jnp.dot(m.astype(BF16), wo_ref[...], preferred_element_type=F32)
    x1_ref[...] = x1
    h2 = _rms(x1, gffn_ref[...]).astype(BF16)
    for c in range(d // LANES):
        h2_ref[:, c, :] = h2[:, c * LANES:(c + 1) * LANES]
    logits = jnp.dot(h2, wr_ref[...], preferred_element_type=F32) + br_ref[...]
    tm = logits.shape[0]
    lane = _lane_iota(logits.shape)
    logits = jnp.where(lane < N_EXPERTS, logits, -jnp.inf)
    wts = jnp.zeros(logits.shape, F32)
    route = jnp.zeros(logits.shape, I32)
    onehot = jnp.zeros(logits.shape, F32)
    den = jnp.zeros((tm, 1), F32)
    picks = []
    v0 = None
    for k in range(TOP_K):
        mx = jnp.max(logits, axis=1, keepdims=True)
        idx = jnp.min(jnp.where(logits == mx, lane, LANES), axis=1, keepdims=True)
        pick = lane == idx
        if v0 is None:
            v0 = mx
        e = jnp.exp(mx - v0)
        wts = jnp.where(lane == k, e, wts)
        route = jnp.where(lane == k, idx, route)
        onehot = jnp.where(pick, 1.0, onehot)
        picks.append(pick)
        den = den + e
        logits = jnp.where(pick, -jnp.inf, logits)
    wts_ref[...] = wts / den
    r_i = lax.broadcasted_iota(I32, (tm, tm), 0)
    c_i = lax.broadcasted_iota(I32, (tm, tm), 1)
    earlier = jnp.where(c_i < r_i, 1.0, 0.0).astype(BF16)
    cum = carry_ref[...] + jnp.dot(earlier, onehot.astype(BF16), preferred_element_type=F32)
    for k in range(TOP_K):
        rank = jnp.sum(jnp.where(picks[k], cum, 0.0), axis=1, keepdims=True).astype(I32)
        route = jnp.where(lane == TOP_K + k, rank, route)
    route_ref[...] = route
    total = carry_ref[...] + jnp.sum(onehot, axis=0, keepdims=True)
    carry_ref[...] = total
    cnt_ref[...] = jnp.broadcast_to(total, cnt_ref.shape)


def _merge(x_all, oa_p, oa_s, ob_p, ob_s, lw):
    t_all, d = x_all.shape
    n_p, n_s = oa_p.shape[0], oa_s.shape[0]
    tm = _pick_tile(int(np.gcd(n_p, n_s)), 256, 16)
    n_pt = n_p // tm
    w_in = lw['w_in']
    w_gate = w_in[:, w_in.shape[1] - 2 * d:].astype(BF16)
    wr = jnp.pad(lw['w_router'], ((0, 0), (0, LANES - N_EXPERTS))).astype(BF16)
    br = jnp.pad(lw['b_router'], (0, LANES - N_EXPERTS)).reshape(1, LANES).astype(F32)
    row = lambda g: g.reshape(1, -1).astype(F32)
    ins = [x_all, oa_p, oa_s, ob_p, ob_s, row(lw['g_mix']), w_gate, row(lw['b_gate']), lw['w_br_a'].astype(BF16),
           lw['w_br_b'].astype(BF16), lw['w_out'].astype(BF16), row(lw['g_ffn']), wr, br]
    tok = lambda w: pl.BlockSpec((tm, w), lambda i: (i, 0))
    ptok = lambda w: pl.BlockSpec((tm, w), lambda i: (jnp.minimum(i, n_pt - 1), 0))
    stok = lambda w: pl.BlockSpec((tm, w), lambda i: (jnp.maximum(i - n_pt, 0), 0))
    full = lambda a: pl.BlockSpec(a.shape, lambda i: (0,) * a.ndim)
    in_specs = [tok(d), ptok(WA_Q), stok(WA_Q), ptok(WB), stok(WB)] + [full(a) for a in ins[5:]]
    sds = jax.ShapeDtypeStruct
    slabs = d // LANES
    return pl.pallas_call(
        functools.partial(_merge_kernel, n_prompt_tiles=n_pt), grid=(t_all // tm,), in_specs=in_specs,
        out_specs=[tok(d), pl.BlockSpec((tm, slabs, LANES), lambda i: (i, 0, 0)), tok(LANES), tok(LANES),
                   pl.BlockSpec((8, LANES), lambda i: (0, 0))],
        out_shape=[sds((t_all, d), F32),
                   sds((t_all, slabs, LANES), BF16),
                   sds((t_all, LANES), I32),
                   sds((t_all, LANES), F32),
                   sds((8, LANES), F32)],
        scratch_shapes=[pltpu.VMEM((1, LANES), F32), pltpu.VMEM((tm, WA_Q), BF16), pltpu.VMEM((tm, WB), BF16)],
        compiler_params=_cparams(1), name="merge")(*ins)


def _swiglu(u):
    glu = jnp.minimum(u[:, :D_FF], SWIGLU_LIMIT)
    lin = jnp.clip(u[:, D_FF:], -SWIGLU_LIMIT, SWIGLU_LIMIT)
    return glu * jax.nn.sigmoid(SWIGLU_ALPHA * glu) * (lin + 1.0)


MOE_ROWS = 256


def _route_plan(route, cnt, n_tiles):
    eid = route[:, :TOP_K]
    rank = route[:, TOP_K:2 * TOP_K]
    cnt_e = cnt[0, :N_EXPERTS].astype(I32)
    ntile = (cnt_e + MOE_ROWS - 1) // MOE_ROWS
    tile_end = jnp.cumsum(ntile)
    tile_start = tile_end - ntile
    row_start = tile_start * MOE_ROWS
    onehot = eid[:, :, None] == jnp.arange(N_EXPERTS, dtype=I32)[None, None, :]
    pos = jnp.sum(jnp.where(onehot, row_start[None, None, :], 0), axis=-1) + rank
    g = jnp.arange(n_tiles, dtype=I32)
    used = tile_end[-1]
    g_eff = jnp.minimum(g, used - 1)
    tile_e = jnp.minimum(jnp.sum(g_eff[:, None] >= tile_end[None, :], axis=1), N_EXPERTS - 1).astype(I32)
    rows = jnp.clip(cnt_e[tile_e] - (g - tile_start[tile_e]) * MOE_ROWS, 0, MOE_ROWS)
    rows = jnp.where(g < used, rows, 0).astype(I32)
    return pos.astype(I32), tile_e, rows


def _dispatch_kernel(pos_ref, h_ref, xs_in, xs_ref, sem, *, tm):
    del xs_in

    def issue(t, carry):
        for k in range(TOP_K):
            pltpu.make_async_copy(h_ref.at[t], xs_ref.at[pos_ref[0, 0, t * TOP_K + k]], sem).start()
        return carry

    lax.fori_loop(0, tm, issue, 0)
    for k in range(TOP_K):
        pltpu.make_async_copy(h_ref, xs_ref.at[pl.ds(0, tm)], sem).wait()


def _dispatch(h2, pos, n_rows):
    t_all, slabs, _ = h2.shape
    tm = _pick_tile(t_all, 256, 16)
    pos3 = pos.reshape(t_all // tm, 1, tm * TOP_K)
    xs0 = jnp.zeros((n_rows, slabs, LANES), h2.dtype)
    return pl.pallas_call(
        functools.partial(_dispatch_kernel, tm=tm), grid=(t_all // tm,),
        in_specs=[pl.BlockSpec((1, 1, tm * TOP_K), lambda i: (i, 0, 0), memory_space=pltpu.SMEM),
                  pl.BlockSpec((tm, slabs, LANES), lambda i: (i, 0, 0)),
                  pl.BlockSpec(memory_space=pl.ANY)],
        out_specs=pl.BlockSpec(memory_space=pl.ANY),
        out_shape=jax.ShapeDtypeStruct(xs0.shape, xs0.dtype),
        scratch_shapes=[pltpu.SemaphoreType.DMA(())],
        input_output_aliases={2: 0},
        compiler_params=_cparams(1), name="dispatch")(pos3, h2, xs0)


def _experts_kernel(te_ref, rows_ref, xs_ref, wu_ref, bu_ref, wd_ref, bd_ref, ys_ref):
    del te_ref
    g = pl.program_id(0)
    slabs = xs_ref.shape[1]

    @pl.when(rows_ref[g] > 0)
    def _():
        x = jnp.concatenate([xs_ref[:, c, :] for c in range(slabs)], axis=1)
        u = jnp.dot(x, wu_ref[0], preferred_element_type=F32) + bu_ref[0]
        o = jnp.dot(_swiglu(u).astype(BF16), wd_ref[0], preferred_element_type=F32) + bd_ref[0]
        for c in range(slabs):
            ys_ref[:, c, :] = o[:, c * LANES:(c + 1) * LANES]

    @pl.when(rows_ref[g] == 0)
    def _():
        ys_ref[...] = jnp.zeros(ys_ref.shape, F32)


def _experts(xs, tile_e, rows, lw):
    n_rows, slabs, _ = xs.shape
    d = slabs * LANES
    wu = lw['w_up'].astype(BF16)
    wd = lw['w_down'].astype(BF16)
    bu = lw['b_up'].reshape(N_EXPERTS, 1, 2 * D_FF).astype(F32)
    bd = lw['b_down'].reshape(N_EXPERTS, 1, d).astype(F32)
    tile = pl.BlockSpec((MOE_ROWS, slabs, LANES), lambda g, te, rw: (g, 0, 0))
    ex = lambda a: pl.BlockSpec((1,) + a.shape[1:], lambda g, te, rw: (te[g], 0, 0))
    grid_spec = pltpu.PrefetchScalarGridSpec(
        num_scalar_prefetch=2, grid=(n_rows // MOE_ROWS,),
        in_specs=[tile, ex(wu), ex(bu), ex(wd), ex(bd)], out_specs=tile)
    return pl.pallas_call(
        _experts_kernel, grid_spec=grid_spec,
        out_shape=jax.ShapeDtypeStruct((n_rows, slabs, LANES), F32),
        compiler_params=_cparams(1), name="experts")(tile_e, rows, xs, wu, bu, wd, bd)


def _combine_kernel(pos_ref, x1_ref, wts_ref, ys_ref, y_ref, buf_ref, sem, *, tm):
    def issue(t, carry):
        for k in range(TOP_K):
            pltpu.make_async_copy(ys_ref.at[pos_ref[0, 0, t * TOP_K + k]], buf_ref.at[k, t], sem).start()
        return carry

    lax.fori_loop(0, tm, issue, 0)
    for k in range(TOP_K):
        pltpu.make_async_copy(ys_ref.at[pl.ds(0, tm)], buf_ref.at[k], sem).wait()
    w = wts_ref[...]
    for c in range(buf_ref.shape[2]):
        acc = x1_ref[:, c * LANES:(c + 1) * LANES]
        for k in range(TOP_K):
            acc = acc + w[:, k:k + 1] * buf_ref[k, :, c, :]
        y_ref[:, c * LANES:(c + 1) * LANES] = acc


def _combine(x1, wts, ys, pos):
    t_all, d = x1.shape
    slabs = d // LANES
    tm = _pick_tile(t_all, 256, 16)
    pos3 = pos.reshape(t_all // tm, 1, tm * TOP_K)
    return pl.pallas_call(
        functools.partial(_combine_kernel, tm=tm), grid=(t_all // tm,),
        in_specs=[pl.BlockSpec((1, 1, tm * TOP_K), lambda i: (i, 0, 0), memory_space=pltpu.SMEM),
                  pl.BlockSpec((tm, d), lambda i: (i, 0)), pl.BlockSpec((tm, LANES), lambda i: (i, 0)),
                  pl.BlockSpec(memory_space=pl.ANY)],
        out_specs=pl.BlockSpec((tm, d), lambda i: (i, 0)),
        out_shape=jax.ShapeDtypeStruct((t_all, d), F32),
        scratch_shapes=[pltpu.VMEM((TOP_K, tm, slabs, LANES), F32), pltpu.SemaphoreType.DMA(())],
        compiler_params=_cparams(1), name="combine")(pos3, x1, wts, ys)


def _moe(x1, h2, route, wts, cnt, lw):
    t_all = x1.shape[0]
    n_tiles = (TOP_K * t_all) // MOE_ROWS + N_EXPERTS
    pos, tile_e, rows = _route_plan(route, cnt, n_tiles)
    xs = _dispatch(h2, pos, n_tiles * MOE_ROWS)
    ys = _experts(xs, tile_e, rows, lw)
    return _combine(x1, wts, ys, pos)


def _layer(xp, xs, a_k, a_v, a_kidx, b_k, b_v, lw):
    _, s, d = xp.shape
    bs, ts, _ = xs.shape
    p_len = a_k.shape[1]
    t_s = bs * ts
    t_all = s + t_s
    x_all = jnp.concatenate([xp.reshape(s, d), xs.reshape(t_s, d)], axis=0)
    pos_all = jnp.concatenate([jnp.arange(s, dtype=I32), jnp.tile(p_len + jnp.arange(ts, dtype=I32), bs)])

    (q_hm, qi_hm, wi, kaf, vaf, kif, kab, vab, ki2, qb_hm, kbf, vbf, kbb, vbb) = _proj(x_all, pos_all, lw)

    tq_p = _pick_tile(s, 128, CHUNK)
    oa_p = _dsa(q_hm, qi_hm, wi, ki2[None], kab[None], vab[None],
                n_batch=1, tq=tq_p, n_qt=s // tq_p, q_off=0, nk=s, topk=min(TOPK_MAX, s // 4),
                pos_base=0, n_valid=s)
    n_keys = p_len + ts
    nk_s = ((n_keys + 1023) // 1024) * 1024
    pad_s = nk_s - n_keys

    def with_new(cache_bf, new_rows):
        return jnp.concatenate([cache_bf, new_rows.reshape(bs, ts, LANES),
                                jnp.zeros((bs, pad_s, LANES), BF16)], axis=1)

    kidx_c = a_kidx.astype(BF16)
    k_s = with_new(a_k.reshape(bs, p_len, HD_A).astype(BF16), kab[s:])
    v_s = with_new(a_v.reshape(bs, p_len, HD_A).astype(BF16), vab[s:])
    ki2_s = with_new(jnp.concatenate([kidx_c, kidx_c], axis=-1), ki2[s:])
    oa_s = _dsa(q_hm, qi_hm, wi, ki2_s, k_s, v_s,
                n_batch=bs, tq=ts, n_qt=1, q_off=s // ts, nk=nk_s, topk=min(TOPK_MAX, n_keys // 4),
                pos_base=p_len, n_valid=n_keys)

    tq_b = BAND_BACK
    ob_p = _band(qb_hm, kbb[None], kbb[None], vbb[None], vbb[None], lw['rel_bias'],
                 n_batch=1, tq=tq_b, tqo=tq_b, n_qt=s // tq_b, q_off=0, n_own=tq_b, prev_always=False,
                 prev_map=lambda b, i: (0, jnp.maximum(i - 1, 0), 0), own_map=lambda b, i: (0, i, 0))
    own_pad = LANES - ts
    kown_s = jnp.pad(kbb[s:].reshape(bs, ts, WB), ((0, 0), (0, own_pad), (0, 0)))
    vown_s = jnp.pad(vbb[s:].reshape(bs, ts, WB), ((0, 0), (0, own_pad), (0, 0)))
    bk2 = b_k.reshape(bs, BAND_BACK, WB)
    bv2 = b_v.reshape(bs, BAND_BACK, WB)
    ob_s = _band(qb_hm, bk2.astype(BF16), kown_s, bv2.astype(BF16), vown_s, lw['rel_bias'],
                 n_batch=bs, tq=ts, tqo=LANES, n_qt=1, q_off=s // ts, n_own=ts, prev_always=True,
                 prev_map=lambda b, i: (b, 0, 0), own_map=lambda b, i: (b, 0, 0))

    x1, h2, route, wts, cnt = _merge(x_all, oa_p, oa_s, ob_p, ob_s, lw)
    y = _moe(x1, h2, route, wts, cnt, lw)

    keep = min(BAND_BACK, s)
    st_p = (kaf[:s].reshape(1, s, 1, HD_A), vaf[:s].reshape(1, s, 1, HD_A), kif[:s, :D_IDX].reshape(1, s, D_IDX),
            kbf[s - keep:s].reshape(1, keep, N_HEADS_B, HD_B), vbf[s - keep:s].reshape(1, keep, N_HEADS_B, HD_B))
    kb_new = kbf[s:].reshape(bs, ts, N_HEADS_B, HD_B)
    vb_new = vbf[s:].reshape(bs, ts, N_HEADS_B, HD_B)
    st_s = (kaf[s:].reshape(bs, ts, 1, HD_A), vaf[s:].reshape(bs, ts, 1, HD_A),
            kif[s:, :D_IDX].reshape(bs, ts, D_IDX),
            jnp.concatenate([b_k, kb_new], axis=1)[:, ts:], jnp.concatenate([b_v, vb_new], axis=1)[:, ts:])
    return y[:s].reshape(1, s, d), y[s:].reshape(bs, ts, d), st_p, st_s


def kernel(x_prompt, x_sample, cache_a_k, cache_a_v, cache_a_kidx, state_b_k, state_b_v,
           g_mix, w_in, b_gate, g_qa, g_ka, g_ki, g_qb, g_kb, rel_bias, w_br_a, w_br_b, w_out,
           g_ffn, w_router, b_router, w_up, b_up, w_down, b_down):
    assert x_prompt.shape[0] == 1, "prompt batch is folded into the token axis; one stream supported"
    depth = g_mix.shape[0]
    yp, ys = x_prompt, x_sample
    states_p, states_s = [], []
    for l in range(depth):
        lw = dict(g_mix=g_mix[l], w_in=w_in[l], b_gate=b_gate[l], g_qa=g_qa[l], g_ka=g_ka[l], g_ki=g_ki[l],
                  g_qb=g_qb[l], g_kb=g_kb[l], rel_bias=rel_bias[l], w_br_a=w_br_a[l], w_br_b=w_br_b[l],
                  w_out=w_out[l], g_ffn=g_ffn[l], w_router=w_router[l], b_router=b_router[l],
                  w_up=w_up[l], b_up=b_up[l], w_down=w_down[l], b_down=b_down[l])
        yp, ys, st_p, st_s = _layer(yp, ys, cache_a_k[l], cache_a_v[l], cache_a_kidx[l],
                                    state_b_k[l], state_b_v[l], lw)
        states_p.append(st_p)
        states_s.append(st_s)
    a_k_p, a_v_p, a_ki_p, b_k_p, b_v_p = [jnp.stack(t) for t in zip(*states_p)]
    a_k_s, a_v_s, a_ki_s, b_k_s, b_v_s = [jnp.stack(t) for t in zip(*states_s)]
    return (yp, ys, a_k_p, a_v_p, a_ki_p, b_k_p, b_v_p, a_k_s, a_v_s, a_ki_s, b_k_s, b_v_s)
```

```python
import functools

import numpy as np
import jax
import jax.numpy as jnp
from jax import lax
from jax.experimental import pallas as pl
from jax.experimental.pallas import tpu as pltpu

F32 = jnp.float32
BF16 = jnp.bfloat16
I32 = jnp.int32

CHUNK = 64
CHUNK_SHIFT = 6
N_HEADS_A = 8
HD_A = 128
N_HEADS_IDX = 8
D_IDX = 64
TOPK_MAX = 256
N_HEADS_B = 8
HD_B = 64
N_PREV_CHUNKS = 8
BAND_BACK = N_PREV_CHUNKS * CHUNK
MAX_REL = 128
N_EXPERTS = 32
TOP_K = 4
D_FF = 1024
SWIGLU_LIMIT = 7.0
SWIGLU_ALPHA = 1.702
ROPE_THETA = 10000.0
EPS = 1e-6
NEG = -1e30
IDX_SCALE = (D_IDX ** -0.5) * (N_HEADS_IDX ** -0.5)
LOG2E = 1.4426950408889634
QK_SCALE_LOG2E = (HD_A ** -0.5) * LOG2E

LANES = 128
VMEM_LIMIT_BYTES = 56 * 1024 * 1024

WA_Q = N_HEADS_A * HD_A
WI_Q = N_HEADS_IDX * D_IDX
WB = N_HEADS_B * HD_B


def _pick_tile(n, target, mult):
    best = None
    for t in range(mult, min(n, target) + 1, mult):
        if n % t == 0:
            best = t
    return best if best is not None else n


def _cparams(n_axes):
    return pltpu.CompilerParams(dimension_semantics=("arbitrary",) * n_axes,
                                vmem_limit_bytes=VMEM_LIMIT_BYTES)


def _lane_iota(shape):
    return lax.broadcasted_iota(I32, shape, len(shape) - 1)


def _rms(x, g):
    ms = jnp.mean(x * x, axis=-1, keepdims=True)
    return x * lax.rsqrt(ms + EPS) * g


_C_QA = 0
_C_KA = _C_QA + WA_Q
_C_VA = _C_KA + HD_A
_C_QI = _C_VA + HD_A
_C_KI = _C_QI + WI_Q
_C_WI = _C_KI + LANES
_C_QB = _C_WI + LANES
_C_KB = _C_QB + WB
_C_VB = _C_KB + WB
_C_END = _C_VB + WB


def _proj_kernel(x_ref, gmix_ref, w_ref, gqa_ref, gka_ref, gki_ref, gqb_ref, gkb_ref,
                 cosa_ref, sina_ref, cosi_ref, sinia_ref, sinib_ref,
                 q_ref, qi_ref, wi_ref, kaf_ref, vaf_ref, kif_ref, kab_ref, vab_ref, ki2_ref,
                 qb_ref, kbf_ref, vbf_ref, kbb_ref, vbb_ref):
    x = x_ref[...]
    h = _rms(x, gmix_ref[...]).astype(BF16)

    def seg(a, b):
        return jnp.dot(h, w_ref[:, a:b], preferred_element_type=F32)

    cosa = cosa_ref[...]
    sina = sina_ref[...]
    cosi = cosi_ref[...]
    sinia = sinia_ref[...]
    sinib = sinib_ref[...]
    lane = _lane_iota((x.shape[0], LANES))
    lo_half = lane < HD_B

    def rope_a(n):
        return n * cosa + pltpu.roll(n, HD_A // 2, 1) * sina

    def rope_i(n):
        return n * cosi + pltpu.roll(n, LANES - D_IDX // 2, 1) * sinia + pltpu.roll(n, D_IDX // 2, 1) * sinib

    z = seg(_C_QA, _C_KA)
    gqa = gqa_ref[...]
    for hd in range(N_HEADS_A):
        zh = z[:, hd * HD_A:(hd + 1) * HD_A]
        q_ref[hd] = (rope_a(_rms(zh, gqa)) * QK_SCALE_LOG2E).astype(BF16)

    ka = rope_a(_rms(seg(_C_KA, _C_VA), gka_ref[...]))
    kaf_ref[...] = ka
    kab_ref[...] = ka.astype(BF16)
    va = seg(_C_VA, _C_QI)
    vaf_ref[...] = va
    vab_ref[...] = va.astype(BF16)

    zk = seg(_C_KI, _C_WI)
    ms = jnp.sum(zk * zk, axis=-1, keepdims=True) * (1.0 / D_IDX)
    ki = rope_i(zk * lax.rsqrt(ms + EPS) * gki_ref[...])
    kif_ref[...] = ki
    ki2_ref[...] = (ki + pltpu.roll(ki, D_IDX, 1)).astype(BF16)

    z = seg(_C_QI, _C_KI)
    for p in range(N_HEADS_IDX // 2):
        r = rope_i(z[:, p * LANES:(p + 1) * LANES])
        qi_ref[2 * p] = jnp.where(lo_half, r, 0.0).astype(BF16)
        qi_ref[2 * p + 1] = jnp.where(lo_half, 0.0, r).astype(BF16)

    wi_ref[...] = seg(_C_WI, _C_QB) * IDX_SCALE

    def norm_b(zb, g):
        sq = zb * zb
        s_all = jnp.sum(sq, axis=-1, keepdims=True)
        s_lo = jnp.sum(jnp.where(lo_half, sq, 0.0), axis=-1, keepdims=True)
        r_lo = lax.rsqrt(s_lo * (1.0 / HD_B) + EPS)
        r_hi = lax.rsqrt((s_all - s_lo) * (1.0 / HD_B) + EPS)
        return zb * jnp.where(lo_half, r_lo, r_hi) * g

    z = seg(_C_QB, _C_KB)
    gqb = gqb_ref[...]
    for p in range(N_HEADS_B // 2):
        n = norm_b(z[:, p * LANES:(p + 1) * LANES], gqb)
        qb_ref[2 * p] = jnp.where(lo_half, n, 0.0).astype(BF16)
        qb_ref[2 * p + 1] = jnp.where(lo_half, 0.0, n).astype(BF16)
    z = seg(_C_KB, _C_VB)
    gkb = gkb_ref[...]
    for p in range(N_HEADS_B // 2):
        n = norm_b(z[:, p * LANES:(p + 1) * LANES], gkb)
        kbf_ref[:, p * LANES:(p + 1) * LANES] = n
        kbb_ref[:, p * LANES:(p + 1) * LANES] = n.astype(BF16)
    z = seg(_C_VB, _C_END)
    vbf_ref[...] = z
    vbb_ref[...] = z.astype(BF16)


def _proj(x_all, pos_all, lw):
    t_all, d = x_all.shape
    tm = _pick_tile(t_all, 256, 16)
    w_in = lw['w_in']
    offs = np.cumsum((WA_Q, HD_A, HD_A, WI_Q, D_IDX, N_HEADS_IDX, WB, WB, WB))
    qa_w, ka_w, va_w, qi_w, ki_w, wi_w, qb_w, kb_w, vb_w = [
        w_in[:, a:b] for a, b in zip(np.concatenate([[0], offs[:-1]]), offs)]

    def padl(w):
        return jnp.pad(w, ((0, 0), (0, LANES - w.shape[1])))

    w_pack = jnp.concatenate([qa_w, ka_w, va_w, qi_w, padl(ki_w), padl(wi_w), qb_w, kb_w, vb_w],
                             axis=1).astype(BF16)

    posf = pos_all.astype(F32)[:, None]

    def tables(dh):
        inv = ROPE_THETA ** (-jnp.arange(0, dh, 2, dtype=F32) / dh)
        ang = posf * inv[None, :]
        return jnp.cos(ang), jnp.sin(ang)

    ca, sa = tables(HD_A)
    cosa = jnp.concatenate([ca, ca], axis=1)
    sina = jnp.concatenate([-sa, sa], axis=1)
    ci, si = tables(D_IDX)
    zi = jnp.zeros_like(si)
    cosi = jnp.concatenate([ci, ci, ci, ci], axis=1)
    sinia = jnp.concatenate([-si, zi, -si, zi], axis=1)
    sinib = jnp.concatenate([zi, si, zi, si], axis=1)

    row = lambda g: g.reshape(1, -1).astype(F32)
    gki = jnp.pad(lw['g_ki'], (0, LANES - D_IDX)).reshape(1, LANES)
    gqb = jnp.tile(lw['g_qb'], 2).reshape(1, LANES)
    gkb = jnp.tile(lw['g_kb'], 2).reshape(1, LANES)

    tok = lambda w: pl.BlockSpec((tm, w), lambda i: (i, 0))
    full = lambda a: pl.BlockSpec(a.shape, lambda i: (0,) * a.ndim)
    hm = pl.BlockSpec((N_HEADS_A, tm, LANES), lambda i: (0, i, 0))

    ins = [x_all, row(lw['g_mix']), w_pack, row(lw['g_qa']), row(lw['g_ka']), gki, gqb, gkb,
           cosa, sina, cosi, sinia, sinib]
    in_specs = [tok(d), full(ins[1]), full(w_pack), full(ins[3]), full(ins[4]), full(gki), full(gqb),
                full(gkb), tok(LANES), tok(LANES), tok(LANES), tok(LANES), tok(LANES)]
    sds = jax.ShapeDtypeStruct
    out_shape = [
        sds((N_HEADS_A, t_all, LANES), BF16),
        sds((N_HEADS_IDX, t_all, LANES), BF16),
        sds((t_all, LANES), F32),
        sds((t_all, HD_A), F32), sds((t_all, HD_A), F32), sds((t_all, LANES), F32),
        sds((t_all, HD_A), BF16), sds((t_all, HD_A), BF16), sds((t_all, LANES), BF16),
        sds((N_HEADS_B, t_all, LANES), BF16),
        sds((t_all, WB), F32), sds((t_all, WB), F32), sds((t_all, WB), BF16), sds((t_all, WB), BF16),
    ]
    out_specs = [hm, hm, tok(LANES), tok(HD_A), tok(HD_A), tok(LANES), tok(HD_A), tok(HD_A), tok(LANES),
                 hm, tok(WB), tok(WB), tok(WB), tok(WB)]
    return pl.pallas_call(
        _proj_kernel, grid=(t_all // tm,), in_specs=in_specs, out_specs=out_specs, out_shape=out_shape,
        compiler_params=_cparams(1), name="proj")(*ins)


BIS_UNROLL = 4
BIS_MAX_ROUNDS = 80
BIG = 1e38
CAND_DEPTH = 16


def _batcher_pairs(n):
    pairs = []
    p = 1
    while p < n:
        k = p
        while k >= 1:
            for j in range(k % p, n - k, 2 * k):
                for i in range(min(k, n - j - k)):
                    if (i + j) // (2 * p) == (i + j + k) // (2 * p):
                        pairs.append((i + j, i + j + k))
            k //= 2
        p *= 2
    return pairs


def _bitonic_pairs(n):
    pairs = []
    stride = n // 2
    while stride >= 1:
        pairs += [(i, i + stride) for i in range(n) if not i & stride]
        stride //= 2
    return pairs


_SORT16 = _batcher_pairs(CAND_DEPTH)
_BITONIC16 = _bitonic_pairs(CAND_DEPTH)


def _dsa_kernel(q_ref, qi_ref, wi_ref, ki2_ref, k_ref, v_ref, o_ref,
                keys_ref, wb_ref, lohi_ref, cnt_ref, m_ref, alpha_ref, acc_ref, s_ref, p_ref, tiec_ref,
                stat_ref, cand_ref, done_ref, *, tq, tk, nkt_max, topk, pos_base, n_valid):
    i = pl.program_id(1)
    pos0 = pos_base + i * tq
    k_end = ((pos0 + tq - 1) // CHUNK + 1) * CHUNK
    k_lim = jnp.minimum(k_end, n_valid)
    npair = jnp.minimum((k_lim + 2 * tk - 1) // (2 * tk), nkt_max // 2)
    nkt = 2 * npair
    nkt4 = jnp.minimum(((nkt + 3) // 4) * 4, nkt_max)
    ncol = tk // LANES
    nh = N_HEADS_A
    topk_f = float(topk)

    qrow = pos0 + lax.broadcasted_iota(I32, (tq, LANES), 0)
    qchunk = lax.shift_right_logical(qrow, CHUNK_SHIFT)
    lane = _lane_iota((tq, LANES))

    def admissible(j, c):
        kpos = j * tk + c * LANES + lane
        return (lax.shift_right_logical(kpos, CHUNK_SHIFT) <= qchunk) & (kpos < n_valid)

    w = wi_ref[...]
    for h in range(N_HEADS_IDX):
        wb_ref[h] = jnp.broadcast_to(w[:, h:h + 1], (tq, LANES))
    qi2d = qi_ref[...].reshape(N_HEADS_IDX * tq, LANES)

    dn_t = (((1,), (1,)), ((), ()))

    def idx_dots(j):
        kt = ki2_ref[0, pl.ds(pl.multiple_of(j * tk, tk), tk), :]
        return lax.dot_general(qi2d, kt, dn_t, preferred_element_type=F32)

    def score_tile(j, slot):
        for c in range(ncol):
            tot = None
            for h in range(N_HEADS_IDX):
                r = jnp.maximum(s_ref[slot, h * tq:(h + 1) * tq, c * LANES:(c + 1) * LANES], 0.0)
                term = wb_ref[h] * r
                tot = term if tot is None else tot + term
            adm = admissible(j, c)
            sc = jnp.where(adm, tot, NEG)
            keys_ref[j, :, c * LANES:(c + 1) * LANES] = sc
            stat_ref[0] = jnp.maximum(stat_ref[0], sc)
            stat_ref[1] = jnp.minimum(stat_ref[1], jnp.where(adm, tot, BIG))

    stat_ref[0] = jnp.full((tq, LANES), -BIG, F32)
    stat_ref[1] = jnp.full((tq, LANES), BIG, F32)
    s_ref[0] = idx_dots(0)

    def score_pair(jj, carry):
        a = 2 * jj
        s_ref[1] = idx_dots(a + 1)
        score_tile(a, 0)
        s_ref[0] = idx_dots(jnp.minimum(a + 2, nkt - 2))
        score_tile(a + 1, 1)
        return carry

    lax.fori_loop(0, npair, score_pair, 0)

    def fill_tile(j, carry):
        keys_ref[j] = jnp.full((tq, tk), NEG, F32)
        return carry

    lax.fori_loop(nkt, nkt4, fill_tile, 0)

    total_f = (nkt4 * tk).astype(F32)
    n_adm = jnp.minimum(lax.shift_left(qchunk + 1, CHUNK_SHIFT), n_valid).astype(F32)
    enough = n_adm >= topk_f
    rmax = jnp.max(stat_ref[0], axis=1, keepdims=True)
    rmin = jnp.min(stat_ref[1], axis=1, keepdims=True)
    lo0 = jnp.where(enough, rmin, -BIG) + jnp.zeros((tq, LANES), F32)
    hi0 = jnp.where(enough, rmax + jnp.maximum(jnp.abs(rmax), 1e-30) * 1e-6, BIG) + jnp.zeros((tq, LANES), F32)

    def build_candidates(qd, carry):
        for g in range(tq // 8):
            rows = slice(8 * g, 8 * g + 8)
            new = [keys_ref[4 * qd + t, rows, c * LANES:(c + 1) * LANES] for t in range(4) for c in range(ncol)]
            for a, b in _SORT16:
                new[a], new[b] = jnp.maximum(new[a], new[b]), jnp.minimum(new[a], new[b])
            top = [jnp.maximum(cand_ref[rows, b * LANES:(b + 1) * LANES], new[CAND_DEPTH - 1 - b])
                   for b in range(CAND_DEPTH)]
            for a, b in _BITONIC16:
                top[a], top[b] = jnp.maximum(top[a], top[b]), jnp.minimum(top[a], top[b])
            for b in range(CAND_DEPTH):
                cand_ref[rows, b * LANES:(b + 1) * LANES] = top[b]
        return carry

    cand_ref[...] = jnp.full(cand_ref.shape, -BIG, F32)
    lax.fori_loop(0, nkt4 // 4, build_candidates, 0)

    def count_cand(thr, strict):
        acc = jnp.zeros((tq, LANES), F32)
        for b in range(CAND_DEPTH):
            x = cand_ref[:, b * LANES:(b + 1) * LANES]
            acc = acc + jnp.where((x > thr) if strict else (x >= thr), 1.0, 0.0)
        return jnp.sum(acc, axis=1, keepdims=True)

    def count_keys(thr, strict):
        def tile(j, acc):
            for cc in range(ncol):
                x = keys_ref[j, :, cc * LANES:(cc + 1) * LANES]
                acc = acc + jnp.where((x > thr) if strict else (x >= thr), 1.0, 0.0)
            return acc
        return jnp.sum(lax.fori_loop(0, nkt4, tile, jnp.zeros((tq, LANES), F32)), axis=1, keepdims=True)

    def search(count):
        lohi_ref[0] = lo0
        lohi_ref[1] = hi0
        cnt_ref[0] = jnp.where(enough, count(lo0, False), topk_f) + jnp.zeros((tq, LANES), F32)
        cnt_ref[1] = jnp.zeros((tq, LANES), F32)

        def unresolved():
            lo = lohi_ref[0]
            hi = lohi_ref[1]
            mid = lo + (hi - lo) * 0.5
            return (cnt_ref[0] != topk_f) & (mid > lo) & (mid < hi)

        def step():
            lo = lohi_ref[0]
            hi = lohi_ref[1]
            mid = lo + (hi - lo) * 0.5
            active = (cnt_ref[0] != topk_f) & (mid > lo) & (mid < hi) & (done_ref[...] == 0.0)
            cnt = count(mid, False)
            up = active & (cnt >= topk_f)
            dn = active & (cnt < topk_f)
            lohi_ref[0] = jnp.where(up, mid, lo)
            lohi_ref[1] = jnp.where(dn, mid, hi)
            cnt_ref[0] = jnp.where(up, cnt, cnt_ref[0])
            cnt_ref[1] = jnp.where(dn, cnt, cnt_ref[1])

        def body(c):
            it, _ = c
            for _ in range(BIS_UNROLL):
                step()
            ties_only = count(lohi_ref[0], True) == cnt_ref[1]
            done = jnp.where(unresolved() & jnp.logical_not(ties_only), 0.0, 1.0)
            done_ref[...] = done
            return it + 1, jnp.min(done)

        done_ref[...] = jnp.zeros((tq, LANES), F32)
        lax.while_loop(lambda c: (c[1] < 0.5) & (c[0] < BIS_MAX_ROUNDS), body, (jnp.int32(0), jnp.float32(0.0)))

    search(count_cand)
    full_lo = count_keys(lohi_ref[0], False)
    full_hi = count_keys(lohi_ref[1], False)
    agree = jnp.logical_not(enough) | ((full_lo == cnt_ref[0]) & (full_hi == cnt_ref[1]))

    @pl.when(jnp.min(jnp.where(agree, 1.0, 0.0)) < 0.5)
    def _():
        search(count_keys)

    lo = lohi_ref[0]
    hi = lohi_ref[1]
    need = topk_f - cnt_ref[1]
    tie_any = jnp.max(jnp.where(cnt_ref[0] > topk_f, 1.0, 0.0))

    m_ref[...] = jnp.full(m_ref.shape, NEG, F32)
    acc_ref[...] = jnp.zeros(acc_ref.shape, F32)
    tiec_ref[...] = jnp.zeros(tiec_ref.shape, F32)
    q2d = q_ref[...].reshape(nh * tq, LANES)
    ones_col = jnp.where(_lane_iota((tk, LANES)) == 0, 1.0, 0.0).astype(BF16)

    def qk_dots(j):
        kt = k_ref[0, pl.ds(pl.multiple_of(j * tk, tk), tk), :]
        return lax.dot_general(q2d, kt, dn_t, preferred_element_type=F32)

    def softmax_tile(j, slot, tie):
        if tie:
            kk = keys_ref[j]
            cand = [(kk[:, c * LANES:(c + 1) * LANES] >= lo) & (kk[:, c * LANES:(c + 1) * LANES] < hi)
                    for c in range(ncol)]
            candf = jnp.concatenate([jnp.where(cd, 1.0, 0.0) for cd in cand], axis=1)
            r_i = lax.broadcasted_iota(I32, (tk, tk), 0)
            c_i = lax.broadcasted_iota(I32, (tk, tk), 1)
            upper = jnp.where(r_i < c_i, 1.0, 0.0).astype(BF16)
            pref = jnp.dot(candf.astype(BF16), upper, preferred_element_type=F32)
            base = tiec_ref[...]
            sel = []
            for c in range(ncol):
                kc = kk[:, c * LANES:(c + 1) * LANES]
                rank = base + pref[:, c * LANES:(c + 1) * LANES]
                sel.append(((kc >= hi) | (cand[c] & (rank < need))) & admissible(j, c))
            tiec_ref[...] = base + jnp.sum(candf, axis=1, keepdims=True)
        else:
            rb = min(tq, 32)
            lane_r = _lane_iota((rb, LANES))
            for r0 in range(0, tq, rb):
                rs = slice(r0, r0 + rb)
                lo_r = lohi_ref[0, rs]
                qc_r = lax.shift_right_logical(pos0 + r0 + lax.broadcasted_iota(I32, (rb, LANES), 0), CHUNK_SHIFT)
                sel = []
                for c in range(ncol):
                    kpos = j * tk + c * LANES + lane_r
                    adm = (lax.shift_right_logical(kpos, CHUNK_SHIFT) <= qc_r) & (kpos < n_valid)
                    sel.append((keys_ref[j, rs, c * LANES:(c + 1) * LANES] >= lo_r) & adm)
                for h in range(nh):
                    hr = slice(h * tq + r0, h * tq + r0 + rb)
                    m_prev = m_ref[h, rs]
                    xs = [jnp.where(sel[c], s_ref[slot, hr, c * LANES:(c + 1) * LANES], NEG) for c in range(ncol)]
                    m_cur = xs[0]
                    for c in range(1, ncol):
                        m_cur = jnp.maximum(m_cur, xs[c])
                    m_new = jnp.maximum(m_prev, jnp.max(m_cur, axis=1, keepdims=True))
                    alpha_ref[slot, h, rs] = jnp.exp2(m_prev - m_new)
                    for c in range(ncol):
                        p_ref[slot, hr, c * LANES:(c + 1) * LANES] = jnp.exp2(xs[c] - m_new).astype(BF16)
                    m_ref[h, rs] = m_new
            return
        for h in range(nh):
            m_prev = m_ref[h]
            xs = [jnp.where(sel[c], s_ref[slot, h * tq:(h + 1) * tq, c * LANES:(c + 1) * LANES], NEG)
                  for c in range(ncol)]
            m_cur = xs[0]
            for c in range(1, ncol):
                m_cur = jnp.maximum(m_cur, xs[c])
            m_new = jnp.maximum(m_prev, jnp.max(m_cur, axis=1, keepdims=True))
            alpha_ref[slot, h] = jnp.exp2(m_prev - m_new)
            for c in range(ncol):
                p_ref[slot, h * tq:(h + 1) * tq, c * LANES:(c + 1) * LANES] = (
                    jnp.exp2(xs[c] - m_new).astype(BF16))
            m_ref[h] = m_new

    def pv_tile(j, slot):
        vt = v_ref[0, pl.ds(pl.multiple_of(j * tk, tk), tk), :]
        pv = jnp.dot(p_ref[slot], jnp.concatenate([vt, ones_col], axis=1), preferred_element_type=F32)
        for h in range(nh):
            alpha = alpha_ref[slot, h]
            for half in range(2):
                hs = slice(half * HD_A, (half + 1) * HD_A)
                acc_ref[h, :, hs] = acc_ref[h, :, hs] * alpha + pv[h * tq:(h + 1) * tq, hs]

    def attend(tie):
        s_ref[0] = qk_dots(0)
        p_ref[1] = jnp.zeros(p_ref.shape[1:], BF16)
        alpha_ref[1] = jnp.ones(alpha_ref.shape[1:], F32)

        def pair(jj, carry):
            a = 2 * jj
            s_ref[1] = qk_dots(a + 1)
            softmax_tile(a, 0, tie)
            pv_tile(jnp.maximum(a - 1, 0), 1)
            s_ref[0] = qk_dots(jnp.minimum(a + 2, nkt - 2))
            softmax_tile(a + 1, 1, tie)
            pv_tile(a, 0)
            return carry

        lax.fori_loop(0, npair, pair, 0)
        pv_tile(nkt - 1, 1)

    @pl.when(tie_any == 0)
    def _():
        attend(False)

    @pl.when(tie_any != 0)
    def _():
        attend(True)

    for h in range(nh):
        den = acc_ref[h, :, HD_A:HD_A + 1]
        o_ref[:, h * HD_A:(h + 1) * HD_A] = (acc_ref[h, :, :HD_A] / den).astype(o_ref.dtype)


def _dsa(q_hm, qi_hm, wi, ki2, k, v, *, n_batch, tq, n_qt, q_off, nk, topk, pos_base, n_valid):
    tk = _pick_tile(nk, 512, LANES)
    nkt_max = nk // tk
    assert nkt_max % 4 == 0 and 4 * (tk // LANES) == CAND_DEPTH and nk >= topk, "key tiles are merged in fours"
    kern = functools.partial(_dsa_kernel, tq=tq, tk=tk, nkt_max=nkt_max, topk=topk,
                             pos_base=pos_base, n_valid=n_valid)
    qmap = lambda b, i: (0, q_off + b * n_qt + i, 0)
    rmap = lambda b, i: (q_off + b * n_qt + i, 0)
    kmap = lambda b, i: (b, 0, 0)
    in_specs = [pl.BlockSpec((N_HEADS_A, tq, LANES), qmap), pl.BlockSpec((N_HEADS_IDX, tq, LANES), qmap),
                pl.BlockSpec((tq, LANES), rmap),
                pl.BlockSpec((1, nk, LANES), kmap), pl.BlockSpec((1, nk, LANES), kmap),
                pl.BlockSpec((1, nk, LANES), kmap)]
    scratch = [
        pltpu.VMEM((nkt_max, tq, tk), F32),
        pltpu.VMEM((N_HEADS_IDX, tq, LANES), F32),
        pltpu.VMEM((2, tq, LANES), F32),
        pltpu.VMEM((2, tq, LANES), F32),
        pltpu.VMEM((N_HEADS_A, tq, LANES), F32),
        pltpu.VMEM((2, N_HEADS_A, tq, LANES), F32),
        pltpu.VMEM((N_HEADS_A, tq, 2 * HD_A), F32),
        pltpu.VMEM((2, N_HEADS_A * tq, tk), F32),
        pltpu.VMEM((2, N_HEADS_A * tq, tk), BF16),
        pltpu.VMEM((tq, LANES), F32),
        pltpu.VMEM((2, tq, LANES), F32),
        pltpu.VMEM((tq, CAND_DEPTH * LANES), F32),
        pltpu.VMEM((tq, LANES), F32),
    ]
    return pl.pallas_call(
        kern, grid=(n_batch, n_qt), in_specs=in_specs,
        out_specs=pl.BlockSpec((tq, WA_Q), lambda b, i: (b * n_qt + i, 0)),
        out_shape=jax.ShapeDtypeStruct((n_batch * n_qt * tq, WA_Q), BF16),
        scratch_shapes=scratch,
        compiler_params=_cparams(2), name="dsa")(q_hm, qi_hm, wi, ki2, k, v)


def _band_kernel(q_ref, kp_ref, ko_ref, vp_ref, vo_ref, rext_ref, o_ref, bias_ref,
                 *, tq, tqo, n_own, off, prev_always):
    b = pl.program_id(0)
    i = pl.program_id(1)
    w = BAND_BACK + tqo
    scale = HD_B ** -0.5

    @pl.when((b == 0) & (i == 0))
    def _():
        ri = lax.broadcasted_iota(I32, (tq, w), 0)
        ci = lax.broadcasted_iota(I32, (tq, w), 1)
        qc = lax.shift_right_logical(ri, CHUNK_SHIFT)
        jo = ci - BAND_BACK
        valid_prev = (ci < BAND_BACK) & (lax.shift_right_logical(ci, CHUNK_SHIFT) >= qc)
        valid_own = (jo >= 0) & (jo < n_own) & (lax.shift_right_logical(jnp.maximum(jo, 0), CHUNK_SHIFT) <= qc)
        valid = valid_prev | valid_own
        for h in range(N_HEADS_B):
            pat = jnp.broadcast_to(rext_ref[h:h + 1, :], (tq, off + w))
            rolled = pltpu.roll(pat, 0, 1, stride=1, stride_axis=0)
            bias_ref[h] = jnp.where(valid, rolled[:, off:off + w], NEG)

    dead_cols = 0 if prev_always else jnp.where(i > 0, 0, BAND_BACK)
    lane = _lane_iota((tq, LANES))
    lo_half = lane < HD_B
    prev_dead = lax.broadcasted_iota(I32, (tq, w), 1) < dead_cols
    for p in range(N_HEADS_B // 2):
        sl = slice(p * LANES, (p + 1) * LANES)
        kcat = jnp.concatenate([kp_ref[0, :, sl], ko_ref[0, :, sl]], axis=0)
        vcat = jnp.concatenate([vp_ref[0, :, sl], vo_ref[0, :, sl]], axis=0)
        outs = []
        for e in range(2):
            h = 2 * p + e
            s = lax.dot_general(q_ref[h], kcat, (((1,), (1,)), ((), ())), preferred_element_type=F32)
            s = s * scale + bias_ref[h]
            s = jnp.where(prev_dead, NEG, s)
            m = jnp.max(s, axis=1, keepdims=True)
            pexp = jnp.exp(s - m)
            den = jnp.sum(pexp, axis=1, keepdims=True)
            pv = jnp.dot(pexp.astype(BF16), vcat, preferred_element_type=F32)
            outs.append(pv / den)
        o_ref[:, sl] = jnp.where(lo_half, outs[0], outs[1]).astype(o_ref.dtype)


def _band(qb_hm, kprev, kown, vprev, vown, rel_bias, *, n_batch, tq, tqo, n_qt, q_off, n_own,
          prev_always, prev_map, own_map):
    off = max(tq, LANES)
    off = ((off + LANES - 1) // LANES) * LANES
    w = BAND_BACK + tqo
    u = np.arange(off + w)
    idx = np.clip(BAND_BACK + off - u, -MAX_REL, MAX_REL) + MAX_REL
    rext = rel_bias.astype(F32)[:, idx]
    kern = functools.partial(_band_kernel, tq=tq, tqo=tqo, n_own=n_own, off=off, prev_always=prev_always)
    qmap = lambda b, i: (0, q_off + b * n_qt + i, 0)
    in_specs = [pl.BlockSpec((N_HEADS_B, tq, LANES), qmap),
                pl.BlockSpec((1, BAND_BACK, WB), prev_map), pl.BlockSpec((1, tqo, WB), own_map),
                pl.BlockSpec((1, BAND_BACK, WB), prev_map), pl.BlockSpec((1, tqo, WB), own_map),
                pl.BlockSpec(rext.shape, lambda b, i: (0, 0))]
    return pl.pallas_call(
        kern, grid=(n_batch, n_qt), in_specs=in_specs,
        out_specs=pl.BlockSpec((tq, WB), lambda b, i: (b * n_qt + i, 0)),
        out_shape=jax.ShapeDtypeStruct((n_batch * n_qt * tq, WB), BF16),
        scratch_shapes=[pltpu.VMEM((N_HEADS_B, tq, w), F32)],
        compiler_params=_cparams(2), name="band")(qb_hm, kprev, kown, vprev, vown, rext)


def _merge_kernel(x_ref, oap_ref, oas_ref, obp_ref, obs_ref, gmix_ref, wg_ref, bg_ref, wa_ref, wb_ref, wo_ref,
                  gffn_ref, wr_ref, br_ref, x1_ref, h2_ref, route_ref, wts_ref, cnt_ref, carry_ref, oa_ref, ob_ref,
                  *, n_prompt_tiles):
    @pl.when(pl.program_id(0) == 0)
    def _():
        carry_ref[...] = jnp.zeros(carry_ref.shape, F32)

    x = x_ref[...]
    d = x.shape[1]
    h = _rms(x, gmix_ref[...]).astype(BF16)
    gates = jax.nn.sigmoid(jnp.dot(h, wg_ref[...], preferred_element_type=F32) + bg_ref[...])
    @pl.when(pl.program_id(0) < n_prompt_tiles)
    def _():
        oa_ref[...] = oap_ref[...]
        ob_ref[...] = obp_ref[...]

    @pl.when(pl.program_id(0) >= n_prompt_tiles)
    def _():
        oa_ref[...] = oas_ref[...]
        ob_ref[...] = obs_ref[...]

    ya = jnp.dot(oa_ref[...], wa_ref[...], preferred_element_type=F32)
    yb = jnp.dot(ob_ref[...], wb_ref[...], preferred_element_type=F32)
    m = gates[:, :d] * ya + gates[:, d:] * yb
    x1 = x + jnp.dot(m.astype(BF16), wo_ref[...], preferred_element_type=F32)
    x1_ref[...] = x1
    h2 = _rms(x1, gffn_ref[...]).astype(BF16)
    for c in range(d // LANES):
        h2_ref[:, c, :] = h2[:, c * LANES:(c + 1) * LANES]
    logits = jnp.dot(h2, wr_ref[...], preferred_element_type=F32) + br_ref[...]
    tm = logits.shape[0]
    lane = _lane_iota(logits.shape)
    logits = jnp.where(lane < N_EXPERTS, logits, -jnp.inf)
    wts = jnp.zeros(logits.shape, F32)
    route = jnp.zeros(logits.shape, I32)
    onehot = jnp.zeros(logits.shape, F32)
    den = jnp.zeros((tm, 1), F32)
    picks = []
    v0 = None
    for k in range(TOP_K):
        mx = jnp.max(logits, axis=1, keepdims=True)
        idx = jnp.min(jnp.where(logits == mx, lane, LANES), axis=1, keepdims=True)
        pick = lane == idx
        if v0 is None:
            v0 = mx
        e = jnp.exp(mx - v0)
        wts = jnp.where(lane == k, e, wts)
        route = jnp.where(lane == k, idx, route)
        onehot = jnp.where(pick, 1.0, onehot)
        picks.append(pick)
        den = den + e
        logits = jnp.where(pick, -jnp.inf, logits)
    wts_ref[...] = wts / den
    r_i = lax.broadcasted_iota(I32, (tm, tm), 0)
    c_i = lax.broadcasted_iota(I32, (tm, tm), 1)
    earlier = jnp.where(c_i < r_i, 1.0, 0.0).astype(BF16)
    cum = carry_ref[...] + jnp.dot(earlier, onehot.astype(BF16), preferred_element_type=F32)
    for k in range(TOP_K):
        rank = jnp.sum(jnp.where(picks[k], cum, 0.0), axis=1, keepdims=True).astype(I32)
        route = jnp.where(lane == TOP_K + k, rank, route)
    route_ref[...] = route
    total = carry_ref[...] + jnp.sum(onehot, axis=0, keepdims=True)
    carry_ref[...] = total
    cnt_ref[...] = jnp.broadcast_to(total, cnt_ref.shape)


def _merge(x_all, oa_p, oa_s, ob_p, ob_s, lw):
    t_all, d = x_all.shape
    n_p, n_s = oa_p.shape[0], oa_s.shape[0]
    tm = _pick_tile(int(np.gcd(n_p, n_s)), 256, 16)
    n_pt = n_p // tm
    w_in = lw['w_in']
    w_gate = w_in[:, w_in.shape[1] - 2 * d:].astype(BF16)
    wr = jnp.pad(lw['w_router'], ((0, 0), (0, LANES - N_EXPERTS))).astype(BF16)
    br = jnp.pad(lw['b_router'], (0, LANES - N_EXPERTS)).reshape(1, LANES).astype(F32)
    row = lambda g: g.reshape(1, -1).astype(F32)
    ins = [x_all, oa_p, oa_s, ob_p, ob_s, row(lw['g_mix']), w_gate, row(lw['b_gate']), lw['w_br_a'].astype(BF16),
           lw['w_br_b'].astype(BF16), lw['w_out'].astype(BF16), row(lw['g_ffn']), wr, br]
    tok = lambda w: pl.BlockSpec((tm, w), lambda i: (i, 0))
    ptok = lambda w: pl.BlockSpec((tm, w), lambda i: (jnp.minimum(i, n_pt - 1), 0))
    stok = lambda w: pl.BlockSpec((tm, w), lambda i: (jnp.maximum(i - n_pt, 0), 0))
    full = lambda a: pl.BlockSpec(a.shape, lambda i: (0,) * a.ndim)
    in_specs = [tok(d), ptok(WA_Q), stok(WA_Q), ptok(WB), stok(WB)] + [full(a) for a in ins[5:]]
    sds = jax.ShapeDtypeStruct
    slabs = d // LANES
    return pl.pallas_call(
        functools.partial(_merge_kernel, n_prompt_tiles=n_pt), grid=(t_all // tm,), in_specs=in_specs,
        out_specs=[tok(d), pl.BlockSpec((tm, slabs, LANES), lambda i: (i, 0, 0)), tok(LANES), tok(LANES),
                   pl.BlockSpec((8, LANES), lambda i: (0, 0))],
        out_shape=[sds((t_all, d), F32),
                   sds((t_all, slabs, LANES), BF16),
                   sds((t_all, LANES), I32),
                   sds((t_all, LANES), F32),
                   sds((8, LANES), F32)],
        scratch_shapes=[pltpu.VMEM((1, LANES), F32), pltpu.VMEM((tm, WA_Q), BF16), pltpu.VMEM((tm, WB), BF16)],
        compiler_params=_cparams(1), name="merge")(*ins)


def _swiglu(u):
    glu = jnp.minimum(u[:, :D_FF], SWIGLU_LIMIT)
    lin = jnp.clip(u[:, D_FF:], -SWIGLU_LIMIT, SWIGLU_LIMIT)
    return glu * jax.nn.sigmoid(SWIGLU_ALPHA * glu) * (lin + 1.0)


MOE_ROWS = 256


def _route_plan(route, cnt, n_tiles):
    eid = route[:, :TOP_K]
    rank = route[:, TOP_K:2 * TOP_K]
    cnt_e = cnt[0, :N_EXPERTS].astype(I32)
    ntile = (cnt_e + MOE_ROWS - 1) // MOE_ROWS
    tile_end = jnp.cumsum(ntile)
    tile_start = tile_end - ntile
    row_start = tile_start * MOE_ROWS
    onehot = eid[:, :, None] == jnp.arange(N_EXPERTS, dtype=I32)[None, None, :]
    pos = jnp.sum(jnp.where(onehot, row_start[None, None, :], 0), axis=-1) + rank
    g = jnp.arange(n_tiles, dtype=I32)
    used = tile_end[-1]
    g_eff = jnp.minimum(g, used - 1)
    tile_e = jnp.minimum(jnp.sum(g_eff[:, None] >= tile_end[None, :], axis=1), N_EXPERTS - 1).astype(I32)
    rows = jnp.clip(cnt_e[tile_e] - (g - tile_start[tile_e]) * MOE_ROWS, 0, MOE_ROWS)
    rows = jnp.where(g < used, rows, 0).astype(I32)
    return pos.astype(I32), tile_e, rows


def _dispatch_kernel(pos_ref, h_ref, xs_in, xs_ref, sem, *, tm):
    del xs_in

    def issue(t, carry):
        for k in range(TOP_K):
            pltpu.make_async_copy(h_ref.at[t], xs_ref.at[pos_ref[0, 0, t * TOP_K + k]], sem).start()
        return carry

    lax.fori_loop(0, tm, issue, 0)
    for k in range(TOP_K):
        pltpu.make_async_copy(h_ref, xs_ref.at[pl.ds(0, tm)], sem).wait()


def _dispatch(h2, pos, n_rows):
    t_all, slabs, _ = h2.shape
    tm = _pick_tile(t_all, 256, 16)
    pos3 = pos.reshape(t_all // tm, 1, tm * TOP_K)
    xs0 = jnp.zeros((n_rows, slabs, LANES), h2.dtype)
    return pl.pallas_call(
        functools.partial(_dispatch_kernel, tm=tm), grid=(t_all // tm,),
        in_specs=[pl.BlockSpec((1, 1, tm * TOP_K), lambda i: (i, 0, 0), memory_space=pltpu.SMEM),
                  pl.BlockSpec((tm, slabs, LANES), lambda i: (i, 0, 0)),
                  pl.BlockSpec(memory_space=pl.ANY)],
        out_specs=pl.BlockSpec(memory_space=pl.ANY),
        out_shape=jax.ShapeDtypeStruct(xs0.shape, xs0.dtype),
        scratch_shapes=[pltpu.SemaphoreType.DMA(())],
        input_output_aliases={2: 0},
        compiler_params=_cparams(1), name="dispatch")(pos3, h2, xs0)


def _experts_kernel(te_ref, rows_ref, xs_ref, wu_ref, bu_ref, wd_ref, bd_ref, ys_ref, wub_ref, wdb_ref):
    g = pl.program_id(0)
    slabs = xs_ref.shape[1]

    @pl.when((g == 0) | (te_ref[g] != te_ref[jnp.maximum(g - 1, 0)]))
    def _():
        wub_ref[...] = wu_ref[0].astype(BF16)
        wdb_ref[...] = wd_ref[0].astype(BF16)

    @pl.when(rows_ref[g] > 0)
    def _():
        x = jnp.concatenate([xs_ref[:, c, :] for c in range(slabs)], axis=1)
        u = jnp.dot(x, wub_ref[...], preferred_element_type=F32) + bu_ref[0]
        o = jnp.dot(_swiglu(u).astype(BF16), wdb_ref[...], preferred_element_type=F32) + bd_ref[0]
        for c in range(slabs):
            ys_ref[:, c, :] = o[:, c * LANES:(c + 1) * LANES]

    @pl.when(rows_ref[g] == 0)
    def _():
        ys_ref[...] = jnp.zeros(ys_ref.shape, F32)


def _experts(xs, tile_e, rows, lw):
    n_rows, slabs, _ = xs.shape
    d = slabs * LANES
    wu = lw['w_up'].astype(F32)
    wd = lw['w_down'].astype(F32)
    bu = lw['b_up'].reshape(N_EXPERTS, 1, 2 * D_FF).astype(F32)
    bd = lw['b_down'].reshape(N_EXPERTS, 1, d).astype(F32)
    tile = pl.BlockSpec((MOE_ROWS, slabs, LANES), lambda g, te, rw: (g, 0, 0))
    ex = lambda a: pl.BlockSpec((1,) + a.shape[1:], lambda g, te, rw: (te[g], 0, 0))
    grid_spec = pltpu.PrefetchScalarGridSpec(
        num_scalar_prefetch=2, grid=(n_rows // MOE_ROWS,),
        in_specs=[tile, ex(wu), ex(bu), ex(wd), ex(bd)], out_specs=tile,
        scratch_shapes=[pltpu.VMEM(wu.shape[1:], BF16), pltpu.VMEM(wd.shape[1:], BF16)])
    return pl.pallas_call(
        _experts_kernel, grid_spec=grid_spec,
        out_shape=jax.ShapeDtypeStruct((n_rows, slabs, LANES), F32),
        compiler_params=_cparams(1), name="experts")(tile_e, rows, xs, wu, bu, wd, bd)


def _combine_kernel(pos_ref, x1_ref, wts_ref, ys_ref, y_ref, buf_ref, sem, *, tm):
    def issue(t, carry):
        for k in range(TOP_K):
            pltpu.make_async_copy(ys_ref.at[pos_ref[0, 0, t * TOP_K + k]], buf_ref.at[k, t], sem).start()
        return carry

    lax.fori_loop(0, tm, issue, 0)
    for k in range(TOP_K):
        pltpu.make_async_copy(ys_ref.at[pl.ds(0, tm)], buf_ref.at[k], sem).wait()
    w = wts_ref[...]
    for c in range(buf_ref.shape[2]):
        acc = x1_ref[:, c * LANES:(c + 1) * LANES]
        for k in range(TOP_K):
            acc = acc + w[:, k:k + 1] * buf_ref[k, :, c, :]
        y_ref[:, c * LANES:(c + 1) * LANES] = acc


def _combine(x1, wts, ys, pos):
    t_all, d = x1.shape
    slabs = d // LANES
    tm = _pick_tile(t_all, 256, 16)
    pos3 = pos.reshape(t_all // tm, 1, tm * TOP_K)
    return pl.pallas_call(
        functools.partial(_combine_kernel, tm=tm), grid=(t_all // tm,),
        in_specs=[pl.BlockSpec((1, 1, tm * TOP_K), lambda i: (i, 0, 0), memory_space=pltpu.SMEM),
                  pl.BlockSpec((tm, d), lambda i: (i, 0)), pl.BlockSpec((tm, LANES), lambda i: (i, 0)),
                  pl.BlockSpec(memory_space=pl.ANY)],
        out_specs=pl.BlockSpec((tm, d), lambda i: (i, 0)),
        out_shape=jax.ShapeDtypeStruct((t_all, d), F32),
        scratch_shapes=[pltpu.VMEM((TOP_K, tm, slabs, LANES), F32), pltpu.SemaphoreType.DMA(())],
        compiler_params=_cparams(1), name="combine")(pos3, x1, wts, ys)


def _moe(x1, h2, route, wts, cnt, lw):
    t_all = x1.shape[0]
    n_tiles = (TOP_K * t_all) // MOE_ROWS + N_EXPERTS
    pos, tile_e, rows = _route_plan(route, cnt, n_tiles)
    xs = _dispatch(h2, pos, n_tiles * MOE_ROWS)
    ys = _experts(xs, tile_e, rows, lw)
    return _combine(x1, wts, ys, pos)


def _layer(xp, xs, a_k, a_v, a_kidx, b_k, b_v, lw):
    _, s, d = xp.shape
    bs, ts, _ = xs.shape
    p_len = a_k.shape[1]
    t_s = bs * ts
    t_all = s + t_s
    x_all = jnp.concatenate([xp.reshape(s, d), xs.reshape(t_s, d)], axis=0)
    pos_all = jnp.concatenate([jnp.arange(s, dtype=I32), jnp.tile(p_len + jnp.arange(ts, dtype=I32), bs)])

    (q_hm, qi_hm, wi, kaf, vaf, kif, kab, vab, ki2, qb_hm, kbf, vbf, kbb, vbb) = _proj(x_all, pos_all, lw)

    tq_p = _pick_tile(s, 128, CHUNK)
    oa_p = _dsa(q_hm, qi_hm, wi, ki2[None], kab[None], vab[None],
                n_batch=1, tq=tq_p, n_qt=s // tq_p, q_off=0, nk=s, topk=min(TOPK_MAX, s // 4),
                pos_base=0, n_valid=s)
    n_keys = p_len + ts
    nk_s = ((n_keys + 2047) // 2048) * 2048
    pad_s = nk_s - n_keys

    def with_new(cache_bf, new_rows):
        return jnp.concatenate([cache_bf, new_rows.reshape(bs, ts, LANES),
                                jnp.zeros((bs, pad_s, LANES), BF16)], axis=1)

    kidx_c = a_kidx.astype(BF16)
    k_s = with_new(a_k.reshape(bs, p_len, HD_A).astype(BF16), kab[s:])
    v_s = with_new(a_v.reshape(bs, p_len, HD_A).astype(BF16), vab[s:])
    ki2_s = with_new(jnp.concatenate([kidx_c, kidx_c], axis=-1), ki2[s:])
    oa_s = _dsa(q_hm, qi_hm, wi, ki2_s, k_s, v_s,
                n_batch=bs, tq=ts, n_qt=1, q_off=s // ts, nk=nk_s, topk=min(TOPK_MAX, n_keys // 4),
                pos_base=p_len, n_valid=n_keys)

    tq_b = BAND_BACK
    ob_p = _band(qb_hm, kbb[None], kbb[None], vbb[None], vbb[None], lw['rel_bias'],
                 n_batch=1, tq=tq_b, tqo=tq_b, n_qt=s // tq_b, q_off=0, n_own=tq_b, prev_always=False,
                 prev_map=lambda b, i: (0, jnp.maximum(i - 1, 0), 0), own_map=lambda b, i: (0, i, 0))
    own_pad = LANES - ts
    kown_s = jnp.pad(kbb[s:].reshape(bs, ts, WB), ((0, 0), (0, own_pad), (0, 0)))
    vown_s = jnp.pad(vbb[s:].reshape(bs, ts, WB), ((0, 0), (0, own_pad), (0, 0)))
    bk2 = b_k.reshape(bs, BAND_BACK, WB)
    bv2 = b_v.reshape(bs, BAND_BACK, WB)
    ob_s = _band(qb_hm, bk2.astype(BF16), kown_s, bv2.astype(BF16), vown_s, lw['rel_bias'],
                 n_batch=bs, tq=ts, tqo=LANES, n_qt=1, q_off=s // ts, n_own=ts, prev_always=True,
                 prev_map=lambda b, i: (b, 0, 0), own_map=lambda b, i: (b, 0, 0))

    x1, h2, route, wts, cnt = _merge(x_all, oa_p, oa_s, ob_p, ob_s, lw)
    y = _moe(x1, h2, route, wts, cnt, lw)

    keep = min(BAND_BACK, s)
    st_p = (kaf[:s].reshape(1, s, 1, HD_A), vaf[:s].reshape(1, s, 1, HD_A), kif[:s, :D_IDX].reshape(1, s, D_IDX),
            kbf[s - keep:s].reshape(1, keep, N_HEADS_B, HD_B), vbf[s - keep:s].reshape(1, keep, N_HEADS_B, HD_B))
    kb_new = kbf[s:].reshape(bs, ts, N_HEADS_B, HD_B)
    vb_new = vbf[s:].reshape(bs, ts, N_HEADS_B, HD_B)
    st_s = (kaf[s:].reshape(bs, ts, 1, HD_A), vaf[s:].reshape(bs, ts, 1, HD_A),
            kif[s:, :D_IDX].reshape(bs, ts, D_IDX),
            jnp.concatenate([b_k, kb_new], axis=1)[:, ts:], jnp.concatenate([b_v, vb_new], axis=1)[:, ts:])
    return y[:s].reshape(1, s, d), y[s:].reshape(bs, ts, d), st_p, st_s


def kernel(x_prompt, x_sample, cache_a_k, cache_a_v, cache_a_kidx, state_b_k, state_b_v,
           g_mix, w_in, b_gate, g_qa, g_ka, g_ki, g_qb, g_kb, rel_bias, w_br_a, w_br_b, w_out,
           g_ffn, w_router, b_router, w_up, b_up, w_down, b_down):
    assert x_prompt.shape[0] == 1, "prompt batch is folded into the token axis; one stream supported"
    depth = g_mix.shape[0]
    yp, ys = x_prompt, x_sample
    states_p, states_s = [], []
    for l in range(depth):
        lw = dict(g_mix=g_mix[l], w_in=w_in[l], b_gate=b_gate[l], g_qa=g_qa[l], g_ka=g_ka[l], g_ki=g_ki[l],
                  g_qb=g_qb[l], g_kb=g_kb[l], rel_bias=rel_bias[l], w_br_a=w_br_a[l], w_br_b=w_br_b[l],
                  w_out=w_out[l], g_ffn=g_ffn[l], w_router=w_router[l], b_router=b_router[l],
                  w_up=w_up[l], b_up=b_up[l], w_down=w_down[l], b_down=b_down[l])
        yp, ys, st_p, st_s = _layer(yp, ys, cache_a_k[l], cache_a_v[l], cache_a_kidx[l],
                                    state_b_k[l], state_b_v[l], lw)
        states_p.append(st_p)
        states_s.append(st_s)
    a_k_p, a_v_p, a_ki_p, b_k_p, b_v_p = [jnp.stack(t) for t in zip(*states_p)]
    a_k_s, a_v_s, a_ki_s, b_k_s, b_v_s = [jnp.stack(t) for t in zip(*states_s)]
    return (yp, ys, a_k_p, a_v_p, a_ki_p, b_k_p, b_v_p, a_k_s, a_v_s, a_ki_s, b_k_s, b_v_s)
```

```python
import functools

import numpy as np
import jax
import jax.numpy as jnp
from jax import lax
from jax.experimental import pallas as pl
from jax.experimental.pallas import tpu as pltpu

F32 = jnp.float32
BF16 = jnp.bfloat16
I32 = jnp.int32

CHUNK = 64
CHUNK_SHIFT = 6
N_HEADS_A = 8
HD_A = 128
N_HEADS_IDX = 8
D_IDX = 64
TOPK_MAX = 256
N_HEADS_B = 8
HD_B = 64
N_PREV_CHUNKS = 8
BAND_BACK = N_PREV_CHUNKS * CHUNK
MAX_REL = 128
N_EXPERTS = 32
TOP_K = 4
D_FF = 1024
SWIGLU_LIMIT = 7.0
SWIGLU_ALPHA = 1.702
ROPE_THETA = 10000.0
EPS = 1e-6
NEG = -1e30
IDX_SCALE = (D_IDX ** -0.5) * (N_HEADS_IDX ** -0.5)
LOG2E = 1.4426950408889634
QK_SCALE_LOG2E = (HD_A ** -0.5) * LOG2E

LANES = 128
VMEM_LIMIT_BYTES = 56 * 1024 * 1024

WA_Q = N_HEADS_A * HD_A
WI_Q = N_HEADS_IDX * D_IDX
WB = N_HEADS_B * HD_B


def _pick_tile(n, target, mult):
    best = None
    for t in range(mult, min(n, target) + 1, mult):
        if n % t == 0:
            best = t
    return best if best is not None else n


def _cparams(n_axes):
    return pltpu.CompilerParams(dimension_semantics=("arbitrary",) * n_axes,
                                vmem_limit_bytes=VMEM_LIMIT_BYTES)


def _lane_iota(shape):
    return lax.broadcasted_iota(I32, shape, len(shape) - 1)


def _rms(x, g):
    ms = jnp.mean(x * x, axis=-1, keepdims=True)
    return x * lax.rsqrt(ms + EPS) * g


_C_QA = 0
_C_KA = _C_QA + WA_Q
_C_VA = _C_KA + HD_A
_C_QI = _C_VA + HD_A
_C_KI = _C_QI + WI_Q
_C_WI = _C_KI + LANES
_C_QB = _C_WI + LANES
_C_KB = _C_QB + WB
_C_VB = _C_KB + WB
_C_END = _C_VB + WB


def _proj_kernel(x_ref, gmix_ref, w_ref, gqa_ref, gka_ref, gki_ref, gqb_ref, gkb_ref,
                 cosa_ref, sina_ref, cosi_ref, sinia_ref, sinib_ref,
                 q_ref, qi_ref, wi_ref, kaf_ref, vaf_ref, kif_ref, kab_ref, vab_ref, ki2_ref,
                 qb_ref, kbf_ref, vbf_ref, kbb_ref, vbb_ref):
    x = x_ref[...]
    h = _rms(x, gmix_ref[...]).astype(BF16)

    def seg(a, b):
        return jnp.dot(h, w_ref[:, a:b], preferred_element_type=F32)

    cosa = cosa_ref[...]
    sina = sina_ref[...]
    cosi = cosi_ref[...]
    sinia = sinia_ref[...]
    sinib = sinib_ref[...]
    lane = _lane_iota((x.shape[0], LANES))
    lo_half = lane < HD_B

    def rope_a(n):
        return n * cosa + pltpu.roll(n, HD_A // 2, 1) * sina

    def rope_i(n):
        return n * cosi + pltpu.roll(n, LANES - D_IDX // 2, 1) * sinia + pltpu.roll(n, D_IDX // 2, 1) * sinib

    z = seg(_C_QA, _C_KA)
    gqa = gqa_ref[...]
    for hd in range(N_HEADS_A):
        zh = z[:, hd * HD_A:(hd + 1) * HD_A]
        q_ref[hd] = (rope_a(_rms(zh, gqa)) * QK_SCALE_LOG2E).astype(BF16)

    ka = rope_a(_rms(seg(_C_KA, _C_VA), gka_ref[...]))
    kaf_ref[...] = ka
    kab_ref[...] = ka.astype(BF16)
    va = seg(_C_VA, _C_QI)
    vaf_ref[...] = va
    vab_ref[...] = va.astype(BF16)

    zk = seg(_C_KI, _C_WI)
    ms = jnp.sum(zk * zk, axis=-1, keepdims=True) * (1.0 / D_IDX)
    ki = rope_i(zk * lax.rsqrt(ms + EPS) * gki_ref[...])
    kif_ref[...] = ki
    ki2_ref[...] = (ki + pltpu.roll(ki, D_IDX, 1)).astype(BF16)

    z = seg(_C_QI, _C_KI)
    for p in range(N_HEADS_IDX // 2):
        r = rope_i(z[:, p * LANES:(p + 1) * LANES])
        qi_ref[2 * p] = jnp.where(lo_half, r, 0.0).astype(BF16)
        qi_ref[2 * p + 1] = jnp.where(lo_half, 0.0, r).astype(BF16)

    wi_ref[...] = seg(_C_WI, _C_QB) * IDX_SCALE

    def norm_b(zb, g):
        sq = zb * zb
        s_all = jnp.sum(sq, axis=-1, keepdims=True)
        s_lo = jnp.sum(jnp.where(lo_half, sq, 0.0), axis=-1, keepdims=True)
        r_lo = lax.rsqrt(s_lo * (1.0 / HD_B) + EPS)
        r_hi = lax.rsqrt((s_all - s_lo) * (1.0 / HD_B) + EPS)
        return zb * jnp.where(lo_half, r_lo, r_hi) * g

    z = seg(_C_QB, _C_KB)
    gqb = gqb_ref[...]
    for p in range(N_HEADS_B // 2):
        n = norm_b(z[:, p * LANES:(p + 1) * LANES], gqb)
        qb_ref[2 * p] = jnp.where(lo_half, n, 0.0).astype(BF16)
        qb_ref[2 * p + 1] = jnp.where(lo_half, 0.0, n).astype(BF16)
    z = seg(_C_KB, _C_VB)
    gkb = gkb_ref[...]
    for p in range(N_HEADS_B // 2):
        n = norm_b(z[:, p * LANES:(p + 1) * LANES], gkb)
        kbf_ref[:, p * LANES:(p + 1) * LANES] = n
        kbb_ref[:, p * LANES:(p + 1) * LANES] = n.astype(BF16)
    z = seg(_C_VB, _C_END)
    vbf_ref[...] = z
    vbb_ref[...] = z.astype(BF16)


def _proj(x_all, pos_all, lw):
    t_all, d = x_all.shape
    tm = _pick_tile(t_all, 256, 16)
    w_in = lw['w_in']
    offs = np.cumsum((WA_Q, HD_A, HD_A, WI_Q, D_IDX, N_HEADS_IDX, WB, WB, WB))
    qa_w, ka_w, va_w, qi_w, ki_w, wi_w, qb_w, kb_w, vb_w = [
        w_in[:, a:b] for a, b in zip(np.concatenate([[0], offs[:-1]]), offs)]

    def padl(w):
        return jnp.pad(w, ((0, 0), (0, LANES - w.shape[1])))

    w_pack = jnp.concatenate([qa_w, ka_w, va_w, qi_w, padl(ki_w), padl(wi_w), qb_w, kb_w, vb_w],
                             axis=1).astype(BF16)

    posf = pos_all.astype(F32)[:, None]

    def tables(dh):
        inv = ROPE_THETA ** (-jnp.arange(0, dh, 2, dtype=F32) / dh)
        ang = posf * inv[None, :]
        return jnp.cos(ang), jnp.sin(ang)

    ca, sa = tables(HD_A)
    cosa = jnp.concatenate([ca, ca], axis=1)
    sina = jnp.concatenate([-sa, sa], axis=1)
    ci, si = tables(D_IDX)
    zi = jnp.zeros_like(si)
    cosi = jnp.concatenate([ci, ci, ci, ci], axis=1)
    sinia = jnp.concatenate([-si, zi, -si, zi], axis=1)
    sinib = jnp.concatenate([zi, si, zi, si], axis=1)

    row = lambda g: g.reshape(1, -1).astype(F32)
    gki = jnp.pad(lw['g_ki'], (0, LANES - D_IDX)).reshape(1, LANES)
    gqb = jnp.tile(lw['g_qb'], 2).reshape(1, LANES)
    gkb = jnp.tile(lw['g_kb'], 2).reshape(1, LANES)

    tok = lambda w: pl.BlockSpec((tm, w), lambda i: (i, 0))
    full = lambda a: pl.BlockSpec(a.shape, lambda i: (0,) * a.ndim)
    hm = pl.BlockSpec((N_HEADS_A, tm, LANES), lambda i: (0, i, 0))

    ins = [x_all, row(lw['g_mix']), w_pack, row(lw['g_qa']), row(lw['g_ka']), gki, gqb, gkb,
           cosa, sina, cosi, sinia, sinib]
    in_specs = [tok(d), full(ins[1]), full(w_pack), full(ins[3]), full(ins[4]), full(gki), full(gqb),
                full(gkb), tok(LANES), tok(LANES), tok(LANES), tok(LANES), tok(LANES)]
    sds = jax.ShapeDtypeStruct
    out_shape = [
        sds((N_HEADS_A, t_all, LANES), BF16),
        sds((N_HEADS_IDX, t_all, LANES), BF16),
        sds((t_all, LANES), F32),
        sds((t_all, HD_A), F32), sds((t_all, HD_A), F32), sds((t_all, LANES), F32),
        sds((t_all, HD_A), BF16), sds((t_all, HD_A), BF16), sds((t_all, LANES), BF16),
        sds((N_HEADS_B, t_all, LANES), BF16),
        sds((t_all, WB), F32), sds((t_all, WB), F32), sds((t_all, WB), BF16), sds((t_all, WB), BF16),
    ]
    out_specs = [hm, hm, tok(LANES), tok(HD_A), tok(HD_A), tok(LANES), tok(HD_A), tok(HD_A), tok(LANES),
                 hm, tok(WB), tok(WB), tok(WB), tok(WB)]
    return pl.pallas_call(
        _proj_kernel, grid=(t_all // tm,), in_specs=in_specs, out_specs=out_specs, out_shape=out_shape,
        compiler_params=_cparams(1), name="proj")(*ins)


BIS_UNROLL = 4
BIS_MAX_ROUNDS = 80
BIG = 1e38
CAND_DEPTH = 16


def _batcher_pairs(n):
    pairs = []
    p = 1
    while p < n:
        k = p
        while k >= 1:
            for j in range(k % p, n - k, 2 * k):
                for i in range(min(k, n - j - k)):
                    if (i + j) // (2 * p) == (i + j + k) // (2 * p):
                        pairs.append((i + j, i + j + k))
            k //= 2
        p *= 2
    return pairs


def _bitonic_pairs(n):
    pairs = []
    stride = n // 2
    while stride >= 1:
        pairs += [(i, i + stride) for i in range(n) if not i & stride]
        stride //= 2
    return pairs


_SORT16 = _batcher_pairs(CAND_DEPTH)
_BITONIC16 = _bitonic_pairs(CAND_DEPTH)


def _dsa_kernel(q_ref, qi_ref, wi_ref, ki2_ref, k_ref, v_ref, o_ref,
                keys_ref, wb_ref, lohi_ref, cnt_ref, m_ref, alpha_ref, acc_ref, s_ref, p_ref, tiec_ref,
                stat_ref, cand_ref, done_ref, *, tq, tk, nkt_max, topk, pos_base, n_valid):
    i = pl.program_id(1)
    pos0 = pos_base + i * tq
    k_end = ((pos0 + tq - 1) // CHUNK + 1) * CHUNK
    k_lim = jnp.minimum(k_end, n_valid)
    nkt4 = jnp.minimum(((k_lim + 4 * tk - 1) // (4 * tk)) * 4, nkt_max)
    ncol = tk // LANES
    nh = N_HEADS_A
    topk_f = float(topk)

    qrow = pos0 + lax.broadcasted_iota(I32, (tq, LANES), 0)
    qchunk = lax.shift_right_logical(qrow, CHUNK_SHIFT)
    lane = _lane_iota((tq, LANES))

    def admissible(j, c):
        kpos = j * tk + c * LANES + lane
        return (lax.shift_right_logical(kpos, CHUNK_SHIFT) <= qchunk) & (kpos < n_valid)

    w = wi_ref[...]
    for h in range(N_HEADS_IDX):
        wb_ref[h] = jnp.broadcast_to(w[:, h:h + 1], (tq, LANES))
    qi2d = qi_ref[...].reshape(N_HEADS_IDX * tq, LANES)

    dn_t = (((1,), (1,)), ((), ()))

    def idx_dots(j):
        kt = ki2_ref[0, pl.ds(pl.multiple_of(j * tk, tk), tk), :]
        return lax.dot_general(qi2d, kt, dn_t, preferred_element_type=F32)

    def score_tile(j, slot):
        for c in range(ncol):
            tot = None
            for h in range(N_HEADS_IDX):
                r = jnp.maximum(s_ref[slot, h * tq:(h + 1) * tq, c * LANES:(c + 1) * LANES], 0.0)
                term = wb_ref[h] * r
                tot = term if tot is None else tot + term
            adm = admissible(j, c)
            sc = jnp.where(adm, tot, NEG)
            keys_ref[j, :, c * LANES:(c + 1) * LANES] = sc
            stat_ref[0] = jnp.maximum(stat_ref[0], sc)
            stat_ref[1] = jnp.minimum(stat_ref[1], jnp.where(adm, tot, BIG))

    stat_ref[0] = jnp.full((tq, LANES), -BIG, F32)
    stat_ref[1] = jnp.full((tq, LANES), BIG, F32)
    s_ref[0] = idx_dots(0)
    s_ref[1] = idx_dots(1)

    def score_quad(it, carry):
        t0 = 4 * it
        s_ref[2] = idx_dots(t0 + 2)
        s_ref[3] = idx_dots(t0 + 3)
        score_tile(t0, 0)
        score_tile(t0 + 1, 1)
        s_ref[0] = idx_dots(jnp.minimum(t0 + 4, nkt4 - 2))
        s_ref[1] = idx_dots(jnp.minimum(t0 + 5, nkt4 - 1))
        score_tile(t0 + 2, 2)
        score_tile(t0 + 3, 3)
        return carry

    lax.fori_loop(0, nkt4 // 4, score_quad, 0)

    total_f = (nkt4 * tk).astype(F32)
    n_adm = jnp.minimum(lax.shift_left(qchunk + 1, CHUNK_SHIFT), n_valid).astype(F32)
    enough = n_adm >= topk_f
    rmax = jnp.max(stat_ref[0], axis=1, keepdims=True)
    rmin = jnp.min(stat_ref[1], axis=1, keepdims=True)
    lo0 = jnp.where(enough, rmin, -BIG) + jnp.zeros((tq, LANES), F32)
    hi0 = jnp.where(enough, rmax + jnp.maximum(jnp.abs(rmax), 1e-30) * 1e-6, BIG) + jnp.zeros((tq, LANES), F32)

    def build_candidates(qd, carry):
        for g in range(tq // 8):
            rows = slice(8 * g, 8 * g + 8)
            new = [keys_ref[4 * qd + t, rows, c * LANES:(c + 1) * LANES] for t in range(4) for c in range(ncol)]
            for a, b in _SORT16:
                new[a], new[b] = jnp.maximum(new[a], new[b]), jnp.minimum(new[a], new[b])
            top = [jnp.maximum(cand_ref[rows, b * LANES:(b + 1) * LANES], new[CAND_DEPTH - 1 - b])
                   for b in range(CAND_DEPTH)]
            for a, b in _BITONIC16:
                top[a], top[b] = jnp.maximum(top[a], top[b]), jnp.minimum(top[a], top[b])
            for b in range(CAND_DEPTH):
                cand_ref[rows, b * LANES:(b + 1) * LANES] = top[b]
        return carry

    cand_ref[...] = jnp.full(cand_ref.shape, -BIG, F32)
    lax.fori_loop(0, nkt4 // 4, build_candidates, 0)

    def count_cand(thr, strict):
        acc = jnp.zeros((tq, LANES), F32)
        for b in range(CAND_DEPTH):
            x = cand_ref[:, b * LANES:(b + 1) * LANES]
            acc = acc + jnp.where((x > thr) if strict else (x >= thr), 1.0, 0.0)
        return jnp.sum(acc, axis=1, keepdims=True)

    def count_keys(thr, strict):
        def tile(j, acc):
            for cc in range(ncol):
                x = keys_ref[j, :, cc * LANES:(cc + 1) * LANES]
                acc = acc + jnp.where((x > thr) if strict else (x >= thr), 1.0, 0.0)
            return acc
        return jnp.sum(lax.fori_loop(0, nkt4, tile, jnp.zeros((tq, LANES), F32)), axis=1, keepdims=True)

    def search(count):
        lohi_ref[0] = lo0
        lohi_ref[1] = hi0
        cnt_ref[0] = jnp.where(enough, count(lo0, False), topk_f) + jnp.zeros((tq, LANES), F32)
        cnt_ref[1] = jnp.zeros((tq, LANES), F32)

        def unresolved():
            lo = lohi_ref[0]
            hi = lohi_ref[1]
            mid = lo + (hi - lo) * 0.5
            return (cnt_ref[0] != topk_f) & (mid > lo) & (mid < hi)

        def step():
            lo = lohi_ref[0]
            hi = lohi_ref[1]
            mid = lo + (hi - lo) * 0.5
            active = (cnt_ref[0] != topk_f) & (mid > lo) & (mid < hi) & (done_ref[...] == 0.0)
            cnt = count(mid, False)
            up = active & (cnt >= topk_f)
            dn = active & (cnt < topk_f)
            lohi_ref[0] = jnp.where(up, mid, lo)
            lohi_ref[1] = jnp.where(dn, mid, hi)
            cnt_ref[0] = jnp.where(up, cnt, cnt_ref[0])
            cnt_ref[1] = jnp.where(dn, cnt, cnt_ref[1])

        def body(c):
            it, _ = c
            for _ in range(BIS_UNROLL):
                step()
            ties_only = count(lohi_ref[0], True) == cnt_ref[1]
            done = jnp.where(unresolved() & jnp.logical_not(ties_only), 0.0, 1.0)
            done_ref[...] = done
            return it + 1, jnp.min(done)

        done_ref[...] = jnp.zeros((tq, LANES), F32)
        lax.while_loop(lambda c: (c[1] < 0.5) & (c[0] < BIS_MAX_ROUNDS), body, (jnp.int32(0), jnp.float32(0.0)))

    search(count_cand)
    full_lo = count_keys(lohi_ref[0], False)
    full_hi = count_keys(lohi_ref[1], False)
    agree = jnp.logical_not(enough) | ((full_lo == cnt_ref[0]) & (full_hi == cnt_ref[1]))

    @pl.when(jnp.min(jnp.where(agree, 1.0, 0.0)) < 0.5)
    def _():
        search(count_keys)

    lo = lohi_ref[0]
    hi = lohi_ref[1]
    need = topk_f - cnt_ref[1]
    tie_any = jnp.max(jnp.where(cnt_ref[0] > topk_f, 1.0, 0.0))

    m_ref[...] = jnp.full(m_ref.shape, NEG, F32)
    acc_ref[...] = jnp.zeros(acc_ref.shape, F32)
    tiec_ref[...] = jnp.zeros(tiec_ref.shape, F32)
    q2d = q_ref[...].reshape(nh * tq, LANES)
    ones_col = jnp.where(_lane_iota((tk, LANES)) == 0, 1.0, 0.0).astype(BF16)

    def qk_dots(j):
        kt = k_ref[0, pl.ds(pl.multiple_of(j * tk, tk), tk), :]
        return lax.dot_general(q2d, kt, dn_t, preferred_element_type=F32)

    def softmax_tile(j, slot, tie):
        if tie:
            kk = keys_ref[j]
            cand = [(kk[:, c * LANES:(c + 1) * LANES] >= lo) & (kk[:, c * LANES:(c + 1) * LANES] < hi)
                    for c in range(ncol)]
            candf = jnp.concatenate([jnp.where(cd, 1.0, 0.0) for cd in cand], axis=1)
            r_i = lax.broadcasted_iota(I32, (tk, tk), 0)
            c_i = lax.broadcasted_iota(I32, (tk, tk), 1)
            upper = jnp.where(r_i < c_i, 1.0, 0.0).astype(BF16)
            pref = jnp.dot(candf.astype(BF16), upper, preferred_element_type=F32)
            base = tiec_ref[...]
            sel = []
            for c in range(ncol):
                kc = kk[:, c * LANES:(c + 1) * LANES]
                rank = base + pref[:, c * LANES:(c + 1) * LANES]
                sel.append(((kc >= hi) | (cand[c] & (rank < need))) & admissible(j, c))
            tiec_ref[...] = base + jnp.sum(candf, axis=1, keepdims=True)
        else:
            rb = min(tq, 32)
            lane_r = _lane_iota((rb, LANES))
            for r0 in range(0, tq, rb):
                rs = slice(r0, r0 + rb)
                lo_r = lohi_ref[0, rs]
                qc_r = lax.shift_right_logical(pos0 + r0 + lax.broadcasted_iota(I32, (rb, LANES), 0), CHUNK_SHIFT)
                sel = []
                for c in range(ncol):
                    kpos = j * tk + c * LANES + lane_r
                    adm = (lax.shift_right_logical(kpos, CHUNK_SHIFT) <= qc_r) & (kpos < n_valid)
                    sel.append((keys_ref[j, rs, c * LANES:(c + 1) * LANES] >= lo_r) & adm)
                for h in range(nh):
                    hr = slice(h * tq + r0, h * tq + r0 + rb)
                    m_prev = m_ref[h, rs]
                    xs = [jnp.where(sel[c], s_ref[slot, hr, c * LANES:(c + 1) * LANES], NEG) for c in range(ncol)]
                    m_cur = xs[0]
                    for c in range(1, ncol):
                        m_cur = jnp.maximum(m_cur, xs[c])
                    m_new = jnp.maximum(m_prev, jnp.max(m_cur, axis=1, keepdims=True))
                    alpha_ref[slot, h, rs] = jnp.exp2(m_prev - m_new)
                    for c in range(ncol):
                        p_ref[slot, hr, c * LANES:(c + 1) * LANES] = jnp.exp2(xs[c] - m_new).astype(BF16)
                    m_ref[h, rs] = m_new
            return
        for h in range(nh):
            m_prev = m_ref[h]
            xs = [jnp.where(sel[c], s_ref[slot, h * tq:(h + 1) * tq, c * LANES:(c + 1) * LANES], NEG)
                  for c in range(ncol)]
            m_cur = xs[0]
            for c in range(1, ncol):
                m_cur = jnp.maximum(m_cur, xs[c])
            m_new = jnp.maximum(m_prev, jnp.max(m_cur, axis=1, keepdims=True))
            alpha_ref[slot, h] = jnp.exp2(m_prev - m_new)
            for c in range(ncol):
                p_ref[slot, h * tq:(h + 1) * tq, c * LANES:(c + 1) * LANES] = (
                    jnp.exp2(xs[c] - m_new).astype(BF16))
            m_ref[h] = m_new

    def pv_tile(j, slot):
        vt = v_ref[0, pl.ds(pl.multiple_of(j * tk, tk), tk), :]
        pv = jnp.dot(p_ref[slot], jnp.concatenate([vt, ones_col], axis=1), preferred_element_type=F32)
        for h in range(nh):
            alpha = alpha_ref[slot, h]
            for half in range(2):
                hs = slice(half * HD_A, (half + 1) * HD_A)
                acc_ref[h, :, hs] = acc_ref[h, :, hs] * alpha + pv[h * tq:(h + 1) * tq, hs]

    def attend(tie):
        npair = jnp.minimum((k_lim + 2 * tk - 1) // (2 * tk), nkt_max // 2)
        s_ref[0] = qk_dots(0)
        p_ref[1] = jnp.zeros(p_ref.shape[1:], BF16)
        alpha_ref[1] = jnp.ones(alpha_ref.shape[1:], F32)

        def pair(jj, carry):
            a = 2 * jj
            s_ref[1] = qk_dots(a + 1)
            softmax_tile(a, 0, tie)
            pv_tile(jnp.maximum(a - 1, 0), 1)
            s_ref[0] = qk_dots(jnp.minimum(a + 2, 2 * npair - 2))
            softmax_tile(a + 1, 1, tie)
            pv_tile(a, 0)
            return carry

        lax.fori_loop(0, npair, pair, 0)
        pv_tile(2 * npair - 1, 1)

    @pl.when(tie_any == 0)
    def _():
        attend(False)

    @pl.when(tie_any != 0)
    def _():
        attend(True)

    for h in range(nh):
        den = acc_ref[h, :, HD_A:HD_A + 1]
        o_ref[:, h * HD_A:(h + 1) * HD_A] = (acc_ref[h, :, :HD_A] / den).astype(o_ref.dtype)


def _dsa(q_hm, qi_hm, wi, ki2, k, v, *, n_batch, tq, n_qt, q_off, nk, topk, pos_base, n_valid):
    tk = _pick_tile(nk, 512, LANES)
    nkt_max = nk // tk
    assert nkt_max % 4 == 0 and 4 * (tk // LANES) == CAND_DEPTH and nk >= topk, "key tiles are merged in fours"
    kern = functools.partial(_dsa_kernel, tq=tq, tk=tk, nkt_max=nkt_max, topk=topk,
                             pos_base=pos_base, n_valid=n_valid)
    qmap = lambda b, i: (0, q_off + b * n_qt + i, 0)
    rmap = lambda b, i: (q_off + b * n_qt + i, 0)
    kmap = lambda b, i: (b, 0, 0)
    in_specs = [pl.BlockSpec((N_HEADS_A, tq, LANES), qmap), pl.BlockSpec((N_HEADS_IDX, tq, LANES), qmap),
                pl.BlockSpec((tq, LANES), rmap),
                pl.BlockSpec((1, nk, LANES), kmap), pl.BlockSpec((1, nk, LANES), kmap),
                pl.BlockSpec((1, nk, LANES), kmap)]
    scratch = [
        pltpu.VMEM((nkt_max, tq, tk), F32),
        pltpu.VMEM((N_HEADS_IDX, tq, LANES), F32),
        pltpu.VMEM((2, tq, LANES), F32),
        pltpu.VMEM((2, tq, LANES), F32),
        pltpu.VMEM((N_HEADS_A, tq, LANES), F32),
        pltpu.VMEM((2, N_HEADS_A, tq, LANES), F32),
        pltpu.VMEM((N_HEADS_A, tq, 2 * HD_A), F32),
        pltpu.VMEM((4, N_HEADS_A * tq, tk), F32),
        pltpu.VMEM((2, N_HEADS_A * tq, tk), BF16),
        pltpu.VMEM((tq, LANES), F32),
        pltpu.VMEM((2, tq, LANES), F32),
        pltpu.VMEM((tq, CAND_DEPTH * LANES), F32),
        pltpu.VMEM((tq, LANES), F32),
    ]
    return pl.pallas_call(
        kern, grid=(n_batch, n_qt), in_specs=in_specs,
        out_specs=pl.BlockSpec((tq, WA_Q), lambda b, i: (b * n_qt + i, 0)),
        out_shape=jax.ShapeDtypeStruct((n_batch * n_qt * tq, WA_Q), BF16),
        scratch_shapes=scratch,
        compiler_params=_cparams(2), name="dsa")(q_hm, qi_hm, wi, ki2, k, v)


def _band_kernel(q_ref, kp_ref, ko_ref, vp_ref, vo_ref, rext_ref, o_ref, bias_ref,
                 *, tq, tqo, n_own, off, prev_always):
    b = pl.program_id(0)
    i = pl.program_id(1)
    w = BAND_BACK + tqo
    scale = HD_B ** -0.5

    @pl.when((b == 0) & (i == 0))
    def _():
        ri = lax.broadcasted_iota(I32, (tq, w), 0)
        ci = lax.broadcasted_iota(I32, (tq, w), 1)
        qc = lax.shift_right_logical(ri, CHUNK_SHIFT)
        jo = ci - BAND_BACK
        valid_prev = (ci < BAND_BACK) & (lax.shift_right_logical(ci, CHUNK_SHIFT) >= qc)
        valid_own = (jo >= 0) & (jo < n_own) & (lax.shift_right_logical(jnp.maximum(jo, 0), CHUNK_SHIFT) <= qc)
        valid = valid_prev | valid_own
        for h in range(N_HEADS_B):
            pat = jnp.broadcast_to(rext_ref[h:h + 1, :], (tq, off + w))
            rolled = pltpu.roll(pat, 0, 1, stride=1, stride_axis=0)
            bias_ref[h] = jnp.where(valid, rolled[:, off:off + w], NEG)

    dead_cols = 0 if prev_always else jnp.where(i > 0, 0, BAND_BACK)
    lane = _lane_iota((tq, LANES))
    lo_half = lane < HD_B
    prev_dead = lax.broadcasted_iota(I32, (tq, w), 1) < dead_cols
    for p in range(N_HEADS_B // 2):
        sl = slice(p * LANES, (p + 1) * LANES)
        kcat = jnp.concatenate([kp_ref[0, :, sl], ko_ref[0, :, sl]], axis=0)
        vcat = jnp.concatenate([vp_ref[0, :, sl], vo_ref[0, :, sl]], axis=0)
        outs = []
        for e in range(2):
            h = 2 * p + e
            s = lax.dot_general(q_ref[h], kcat, (((1,), (1,)), ((), ())), preferred_element_type=F32)
            s = s * scale + bias_ref[h]
            s = jnp.where(prev_dead, NEG, s)
            m = jnp.max(s, axis=1, keepdims=True)
            pexp = jnp.exp(s - m)
            den = jnp.sum(pexp, axis=1, keepdims=True)
            pv = jnp.dot(pexp.astype(BF16), vcat, preferred_element_type=F32)
            outs.append(pv / den)
        o_ref[:, sl] = jnp.where(lo_half, outs[0], outs[1]).astype(o_ref.dtype)


def _band(qb_hm, kprev, kown, vprev, vown, rel_bias, *, n_batch, tq, tqo, n_qt, q_off, n_own,
          prev_always, prev_map, own_map):
    off = max(tq, LANES)
    off = ((off + LANES - 1) // LANES) * LANES
    w = BAND_BACK + tqo
    u = np.arange(off + w)
    idx = np.clip(BAND_BACK + off - u, -MAX_REL, MAX_REL) + MAX_REL
    rext = rel_bias.astype(F32)[:, idx]
    kern = functools.partial(_band_kernel, tq=tq, tqo=tqo, n_own=n_own, off=off, prev_always=prev_always)
    qmap = lambda b, i: (0, q_off + b * n_qt + i, 0)
    in_specs = [pl.BlockSpec((N_HEADS_B, tq, LANES), qmap),
                pl.BlockSpec((1, BAND_BACK, WB), prev_map), pl.BlockSpec((1, tqo, WB), own_map),
                pl.BlockSpec((1, BAND_BACK, WB), prev_map), pl.BlockSpec((1, tqo, WB), own_map),
                pl.BlockSpec(rext.shape, lambda b, i: (0, 0))]
    return pl.pallas_call(
        kern, grid=(n_batch, n_qt), in_specs=in_specs,
        out_specs=pl.BlockSpec((tq, WB), lambda b, i: (b * n_qt + i, 0)),
        out_shape=jax.ShapeDtypeStruct((n_batch * n_qt * tq, WB), BF16),
        scratch_shapes=[pltpu.VMEM((N_HEADS_B, tq, w), F32)],
        compiler_params=_cparams(2), name="band")(qb_hm, kprev, kown, vprev, vown, rext)


def _merge_kernel(x_ref, oap_ref, oas_ref, obp_ref, obs_ref, gmix_ref, wg_ref, bg_ref, wa_ref, wb_ref, wo_ref,
                  gffn_ref, wr_ref, br_ref, x1_ref, h2_ref, route_ref, wts_ref, cnt_ref, carry_ref, oa_ref, ob_ref,
                  *, n_prompt_tiles):
    @pl.when(pl.program_id(0) == 0)
    def _():
        carry_ref[...] = jnp.zeros(carry_ref.shape, F32)

    x = x_ref[...]
    d = x.shape[1]
    h = _rms(x, gmix_ref[...]).astype(BF16)
    gates = jax.nn.sigmoid(jnp.dot(h, wg_ref[...], preferred_element_type=F32) + bg_ref[...])
    @pl.when(pl.program_id(0) < n_prompt_tiles)
    def _():
        oa_ref[...] = oap_ref[...]
        ob_ref[...] = obp_ref[...]

    @pl.when(pl.program_id(0) >= n_prompt_tiles)
    def _():
        oa_ref[...] = oas_ref[...]
        ob_ref[...] = obs_ref[...]

    ya = jnp.dot(oa_ref[...], wa_ref[...], preferred_element_type=F32)
    yb = jnp.dot(ob_ref[...], wb_ref[...], preferred_element_type=F32)
    m = gates[:, :d] * ya + gates[:, d:] * yb
    x1 = x + jnp.dot(m.astype(BF16), wo_ref[...], preferred_element_type=F32)
    x1_ref[...] = x1
    h2 = _rms(x1, gffn_ref[...]).astype(BF16)
    for c in range(d // LANES):
        h2_ref[:, c, :] = h2[:, c * LANES:(c + 1) * LANES]
    logits = jnp.dot(h2, wr_ref[...], preferred_element_type=F32) + br_ref[...]
    tm = logits.shape[0]
    lane = _lane_iota(logits.shape)
    logits = jnp.where(lane < N_EXPERTS, logits, -jnp.inf)
    wts = jnp.zeros(logits.shape, F32)
    route = jnp.zeros(logits.shape, I32)
    onehot = jnp.zeros(logits.shape, F32)
    den = jnp.zeros((tm, 1), F32)
    picks = []
    v0 = None
    for k in range(TOP_K):
        mx = jnp.max(logits, axis=1, keepdims=True)
        idx = jnp.min(jnp.where(logits == mx, lane, LANES), axis=1, keepdims=True)
        pick = lane == idx
        if v0 is None:
            v0 = mx
        e = jnp.exp(mx - v0)
        wts = jnp.where(lane == k, e, wts)
        route = jnp.where(lane == k, idx, route)
        onehot = jnp.where(pick, 1.0, onehot)
        picks.append(pick)
        den = den + e
        logits = jnp.where(pick, -jnp.inf, logits)
    wts_ref[...] = wts / den
    r_i = lax.broadcasted_iota(I32, (tm, tm), 0)
    c_i = lax.broadcasted_iota(I32, (tm, tm), 1)
    earlier = jnp.where(c_i < r_i, 1.0, 0.0).astype(BF16)
    cum = carry_ref[...] + jnp.dot(earlier, onehot.astype(BF16), preferred_element_type=F32)
    for k in range(TOP_K):
        rank = jnp.sum(jnp.where(picks[k], cum, 0.0), axis=1, keepdims=True).astype(I32)
        route = jnp.where(lane == TOP_K + k, rank, route)
    route_ref[...] = route
    total = carry_ref[...] + jnp.sum(onehot, axis=0, keepdims=True)
    carry_ref[...] = total
    cnt_ref[...] = jnp.broadcast_to(total, cnt_ref.shape)


def _merge(x_all, oa_p, oa_s, ob_p, ob_s, lw):
    t_all, d = x_all.shape
    n_p, n_s = oa_p.shape[0], oa_s.shape[0]
    tm = _pick_tile(int(np.gcd(n_p, n_s)), 256, 16)
    n_pt = n_p // tm
    w_in = lw['w_in']
    w_gate = w_in[:, w_in.shape[1] - 2 * d:].astype(BF16)
    wr = jnp.pad(lw['w_router'], ((0, 0), (0, LANES - N_EXPERTS))).astype(BF16)
    br = jnp.pad(lw['b_router'], (0, LANES - N_EXPERTS)).reshape(1, LANES).astype(F32)
    row = lambda g: g.reshape(1, -1).astype(F32)
    ins = [x_all, oa_p, oa_s, ob_p, ob_s, row(lw['g_mix']), w_gate, row(lw['b_gate']), lw['w_br_a'].astype(BF16),
           lw['w_br_b'].astype(BF16), lw['w_out'].astype(BF16), row(lw['g_ffn']), wr, br]
    tok = lambda w: pl.BlockSpec((tm, w), lambda i: (i, 0))
    ptok = lambda w: pl.BlockSpec((tm, w), lambda i: (jnp.minimum(i, n_pt - 1), 0))
    stok = lambda w: pl.BlockSpec((tm, w), lambda i: (jnp.maximum(i - n_pt, 0), 0))
    full = lambda a: pl.BlockSpec(a.shape, lambda i: (0,) * a.ndim)
    in_specs = [tok(d), ptok(WA_Q), stok(WA_Q), ptok(WB), stok(WB)] + [full(a) for a in ins[5:]]
    sds = jax.ShapeDtypeStruct
    slabs = d // LANES
    return pl.pallas_call(
        functools.partial(_merge_kernel, n_prompt_tiles=n_pt), grid=(t_all // tm,), in_specs=in_specs,
        out_specs=[tok(d), pl.BlockSpec((tm, slabs, LANES), lambda i: (i, 0, 0)), tok(LANES), tok(LANES),
                   pl.BlockSpec((8, LANES), lambda i: (0, 0))],
        out_shape=[sds((t_all, d), F32),
                   sds((t_all, slabs, LANES), BF16),
                   sds((t_all, LANES), I32),
                   sds((t_all, LANES), F32),
                   sds((8, LANES), F32)],
        scratch_shapes=[pltpu.VMEM((1, LANES), F32), pltpu.VMEM((tm, WA_Q), BF16), pltpu.VMEM((tm, WB), BF16)],
        compiler_params=_cparams(1), name="merge")(*ins)


def _swiglu(u):
    glu = jnp.minimum(u[:, :D_FF], SWIGLU_LIMIT)
    lin = jnp.clip(u[:, D_FF:], -SWIGLU_LIMIT, SWIGLU_LIMIT)
    return glu * jax.nn.sigmoid(SWIGLU_ALPHA * glu) * (lin + 1.0)


MOE_ROWS = 512


def _route_plan(route, cnt, n_tiles):
    eid = route[:, :TOP_K]
    rank = route[:, TOP_K:2 * TOP_K]
    cnt_e = cnt[0, :N_EXPERTS].astype(I32)
    ntile = (cnt_e + MOE_ROWS - 1) // MOE_ROWS
    tile_end = jnp.cumsum(ntile)
    tile_start = tile_end - ntile
    row_start = tile_start * MOE_ROWS
    onehot = eid[:, :, None] == jnp.arange(N_EXPERTS, dtype=I32)[None, None, :]
    pos = jnp.sum(jnp.where(onehot, row_start[None, None, :], 0), axis=-1) + rank
    g = jnp.arange(n_tiles, dtype=I32)
    used = tile_end[-1]
    g_eff = jnp.minimum(g, used - 1)
    tile_e = jnp.minimum(jnp.sum(g_eff[:, None] >= tile_end[None, :], axis=1), N_EXPERTS - 1).astype(I32)
    rows = jnp.clip(cnt_e[tile_e] - (g - tile_start[tile_e]) * MOE_ROWS, 0, MOE_ROWS)
    rows = jnp.where(g < used, rows, 0).astype(I32)
    return pos.astype(I32), tile_e, rows


def _dispatch_kernel(pos_ref, h_ref, xs_in, xs_ref, sem, *, tm):
    del xs_in

    def issue(t, carry):
        for k in range(TOP_K):
            pltpu.make_async_copy(h_ref.at[t], xs_ref.at[pos_ref[0, 0, t * TOP_K + k]], sem).start()
        return carry

    lax.fori_loop(0, tm, issue, 0)
    for k in range(TOP_K):
        pltpu.make_async_copy(h_ref, xs_ref.at[pl.ds(0, tm)], sem).wait()


def _dispatch(h2, pos, n_rows):
    t_all, slabs, _ = h2.shape
    tm = _pick_tile(t_all, 256, 16)
    pos3 = pos.reshape(t_all // tm, 1, tm * TOP_K)
    xs0 = jnp.zeros((n_rows, slabs, LANES), h2.dtype)
    return pl.pallas_call(
        functools.partial(_dispatch_kernel, tm=tm), grid=(t_all // tm,),
        in_specs=[pl.BlockSpec((1, 1, tm * TOP_K), lambda i: (i, 0, 0), memory_space=pltpu.SMEM),
                  pl.BlockSpec((tm, slabs, LANES), lambda i: (i, 0, 0)),
                  pl.BlockSpec(memory_space=pl.ANY)],
        out_specs=pl.BlockSpec(memory_space=pl.ANY),
        out_shape=jax.ShapeDtypeStruct(xs0.shape, xs0.dtype),
        scratch_shapes=[pltpu.SemaphoreType.DMA(())],
        input_output_aliases={2: 0},
        compiler_params=_cparams(1), name="dispatch")(pos3, h2, xs0)


def _experts_kernel(te_ref, rows_ref, xs_ref, wu_ref, bu_ref, wd_ref, bd_ref, ys_ref, wub_ref, wdb_ref):
    g = pl.program_id(0)
    slabs = xs_ref.shape[1]

    @pl.when((g == 0) | (te_ref[g] != te_ref[jnp.maximum(g - 1, 0)]))
    def _():
        wub_ref[...] = wu_ref[0].astype(BF16)
        wdb_ref[...] = wd_ref[0].astype(BF16)

    @pl.when(rows_ref[g] > 0)
    def _():
        x = jnp.concatenate([xs_ref[:, c, :] for c in range(slabs)], axis=1)
        u = jnp.dot(x, wub_ref[...], preferred_element_type=F32) + bu_ref[0]
        ys_ref[...] = jnp.dot(_swiglu(u).astype(BF16), wdb_ref[...], preferred_element_type=F32) + bd_ref[0]

    @pl.when(rows_ref[g] == 0)
    def _():
        ys_ref[...] = jnp.zeros(ys_ref.shape, F32)


def _experts(xs, tile_e, rows, lw):
    n_rows, slabs, _ = xs.shape
    d = slabs * LANES
    wu = lw['w_up'].astype(F32)
    wd = lw['w_down'].astype(F32)
    bu = lw['b_up'].reshape(N_EXPERTS, 1, 2 * D_FF).astype(F32)
    bd = lw['b_down'].reshape(N_EXPERTS, 1, d).astype(F32)
    tile = pl.BlockSpec((MOE_ROWS, slabs, LANES), lambda g, te, rw: (g, 0, 0))
    ex = lambda a: pl.BlockSpec((1,) + a.shape[1:], lambda g, te, rw: (te[g], 0, 0))
    grid_spec = pltpu.PrefetchScalarGridSpec(
        num_scalar_prefetch=2, grid=(n_rows // MOE_ROWS,),
        in_specs=[tile, ex(wu), ex(bu), ex(wd), ex(bd)],
        out_specs=pl.BlockSpec((MOE_ROWS, d), lambda g, te, rw: (g, 0)),
        scratch_shapes=[pltpu.VMEM(wu.shape[1:], BF16), pltpu.VMEM(wd.shape[1:], BF16)])
    return pl.pallas_call(
        _experts_kernel, grid_spec=grid_spec,
        out_shape=jax.ShapeDtypeStruct((n_rows, d), F32),
        compiler_params=_cparams(1), name="experts")(tile_e, rows, xs, wu, bu, wd, bd)


def _combine_kernel(pos_ref, x1_ref, wts_ref, ys_ref, yp_ref, ys_out_ref, buf_ref, sem, *, tm, n_prompt_tiles):
    def issue(t, carry):
        for k in range(TOP_K):
            pltpu.make_async_copy(ys_ref.at[pl.ds(pos_ref[0, 0, t * TOP_K + k], 1)],
                                  buf_ref.at[k, pl.ds(t, 1)], sem).start()
        return carry

    lax.fori_loop(0, tm, issue, 0)
    for k in range(TOP_K):
        pltpu.make_async_copy(ys_ref.at[pl.ds(0, tm)], buf_ref.at[k], sem).wait()
    w = wts_ref[...]
    acc = x1_ref[...]
    for k in range(TOP_K):
        acc = acc + w[:, k:k + 1] * buf_ref[k]

    @pl.when(pl.program_id(0) < n_prompt_tiles)
    def _():
        yp_ref[...] = acc

    @pl.when(pl.program_id(0) >= n_prompt_tiles)
    def _():
        ys_out_ref[...] = acc


def _combine(x1, wts, ys, pos, n_prompt):
    t_all, d = x1.shape
    n_s = t_all - n_prompt
    tm = _pick_tile(int(np.gcd(n_prompt, n_s)), 256, 16)
    n_pt = n_prompt // tm
    pos3 = pos.reshape(t_all // tm, 1, tm * TOP_K)
    return pl.pallas_call(
        functools.partial(_combine_kernel, tm=tm, n_prompt_tiles=n_pt), grid=(t_all // tm,),
        in_specs=[pl.BlockSpec((1, 1, tm * TOP_K), lambda i: (i, 0, 0), memory_space=pltpu.SMEM),
                  pl.BlockSpec((tm, d), lambda i: (i, 0)), pl.BlockSpec((tm, LANES), lambda i: (i, 0)),
                  pl.BlockSpec(memory_space=pl.ANY)],
        out_specs=[pl.BlockSpec((tm, d), lambda i: (jnp.minimum(i, n_pt - 1), 0)),
                   pl.BlockSpec((tm, d), lambda i: (jnp.maximum(i - n_pt, 0), 0))],
        out_shape=[jax.ShapeDtypeStruct((n_prompt, d), F32), jax.ShapeDtypeStruct((n_s, d), F32)],
        scratch_shapes=[pltpu.VMEM((TOP_K, tm, d), F32), pltpu.SemaphoreType.DMA(())],
        compiler_params=_cparams(1), name="combine")(pos3, x1, wts, ys)


def _moe(x1, h2, route, wts, cnt, lw, n_prompt):
    t_all = x1.shape[0]
    n_tiles = (TOP_K * t_all) // MOE_ROWS + N_EXPERTS
    pos, tile_e, rows = _route_plan(route, cnt, n_tiles)
    xs = _dispatch(h2, pos, n_tiles * MOE_ROWS)
    ys = _experts(xs, tile_e, rows, lw)
    return _combine(x1, wts, ys, pos, n_prompt)


def _layer(xp, xs, a_k, a_v, a_kidx, b_k, b_v, lw):
    _, s, d = xp.shape
    bs, ts, _ = xs.shape
    p_len = a_k.shape[1]
    t_s = bs * ts
    t_all = s + t_s
    x_all = jnp.concatenate([xp.reshape(s, d), xs.reshape(t_s, d)], axis=0)
    pos_all = jnp.concatenate([jnp.arange(s, dtype=I32), jnp.tile(p_len + jnp.arange(ts, dtype=I32), bs)])

    (q_hm, qi_hm, wi, kaf, vaf, kif, kab, vab, ki2, qb_hm, kbf, vbf, kbb, vbb) = _proj(x_all, pos_all, lw)

    tq_p = _pick_tile(s, 128, CHUNK)
    oa_p = _dsa(q_hm, qi_hm, wi, ki2[None], kab[None], vab[None],
                n_batch=1, tq=tq_p, n_qt=s // tq_p, q_off=0, nk=s, topk=min(TOPK_MAX, s // 4),
                pos_base=0, n_valid=s)
    n_keys = p_len + ts
    nk_s = ((n_keys + 2047) // 2048) * 2048
    pad_s = nk_s - n_keys

    def with_new(cache_bf, new_rows):
        return jnp.concatenate([cache_bf, new_rows.reshape(bs, ts, LANES),
                                jnp.zeros((bs, pad_s, LANES), BF16)], axis=1)

    kidx_c = a_kidx.astype(BF16)
    k_s = with_new(a_k.reshape(bs, p_len, HD_A).astype(BF16), kab[s:])
    v_s = with_new(a_v.reshape(bs, p_len, HD_A).astype(BF16), vab[s:])
    ki2_s = with_new(jnp.concatenate([kidx_c, kidx_c], axis=-1), ki2[s:])
    oa_s = _dsa(q_hm, qi_hm, wi, ki2_s, k_s, v_s,
                n_batch=bs, tq=ts, n_qt=1, q_off=s // ts, nk=nk_s, topk=min(TOPK_MAX, n_keys // 4),
                pos_base=p_len, n_valid=n_keys)

    tq_b = BAND_BACK
    ob_p = _band(qb_hm, kbb[None], kbb[None], vbb[None], vbb[None], lw['rel_bias'],
                 n_batch=1, tq=tq_b, tqo=tq_b, n_qt=s // tq_b, q_off=0, n_own=tq_b, prev_always=False,
                 prev_map=lambda b, i: (0, jnp.maximum(i - 1, 0), 0), own_map=lambda b, i: (0, i, 0))
    own_pad = LANES - ts
    kown_s = jnp.pad(kbb[s:].reshape(bs, ts, WB), ((0, 0), (0, own_pad), (0, 0)))
    vown_s = jnp.pad(vbb[s:].reshape(bs, ts, WB), ((0, 0), (0, own_pad), (0, 0)))
    bk2 = b_k.reshape(bs, BAND_BACK, WB)
    bv2 = b_v.reshape(bs, BAND_BACK, WB)
    ob_s = _band(qb_hm, bk2.astype(BF16), kown_s, bv2.astype(BF16), vown_s, lw['rel_bias'],
                 n_batch=bs, tq=ts, tqo=LANES, n_qt=1, q_off=s // ts, n_own=ts, prev_always=True,
                 prev_map=lambda b, i: (b, 0, 0), own_map=lambda b, i: (b, 0, 0))

    x1, h2, route, wts, cnt = _merge(x_all, oa_p, oa_s, ob_p, ob_s, lw)
    y_p, y_s = _moe(x1, h2, route, wts, cnt, lw, s)

    keep = min(BAND_BACK, s)
    st_p = (kaf[:s].reshape(1, s, 1, HD_A), vaf[:s].reshape(1, s, 1, HD_A), kif[:s, :D_IDX].reshape(1, s, D_IDX),
            kbf[s - keep:s].reshape(1, keep, N_HEADS_B, HD_B), vbf[s - keep:s].reshape(1, keep, N_HEADS_B, HD_B))
    kb_new = kbf[s:].reshape(bs, ts, N_HEADS_B, HD_B)
    vb_new = vbf[s:].reshape(bs, ts, N_HEADS_B, HD_B)
    st_s = (kaf[s:].reshape(bs, ts, 1, HD_A), vaf[s:].reshape(bs, ts, 1, HD_A),
            kif[s:, :D_IDX].reshape(bs, ts, D_IDX),
            jnp.concatenate([b_k, kb_new], axis=1)[:, ts:], jnp.concatenate([b_v, vb_new], axis=1)[:, ts:])
    return y_p.reshape(1, s, d), y_s.reshape(bs, ts, d), st_p, st_s


def kernel(x_prompt, x_sample, cache_a_k, cache_a_v, cache_a_kidx, state_b_k, state_b_v,
           g_mix, w_in, b_gate, g_qa, g_ka, g_ki, g_qb, g_kb, rel_bias, w_br_a, w_br_b, w_out,
           g_ffn, w_router, b_router, w_up, b_up, w_down, b_down):
    assert x_prompt.shape[0] == 1, "prompt batch is folded into the token axis; one stream supported"
    depth = g_mix.shape[0]
    yp, ys = x_prompt, x_sample
    states_p, states_s = [], []
    for l in range(depth):
        lw = dict(g_mix=g_mix[l], w_in=w_in[l], b_gate=b_gate[l], g_qa=g_qa[l], g_ka=g_ka[l], g_ki=g_ki[l],
                  g_qb=g_qb[l], g_kb=g_kb[l], rel_bias=rel_bias[l], w_br_a=w_br_a[l], w_br_b=w_br_b[l],
                  w_out=w_out[l], g_ffn=g_ffn[l], w_router=w_router[l], b_router=b_router[l],
                  w_up=w_up[l], b_up=b_up[l], w_down=w_down[l], b_down=b_down[l])
        yp, ys, st_p, st_s = _layer(yp, ys, cache_a_k[l], cache_a_v[l], cache_a_kidx[l],
                                    state_b_k[l], state_b_v[l], lw)
        states_p.append(st_p)
        states_s.append(st_s)
    a_k_p, a_v_p, a_ki_p, b_k_p, b_v_p = [jnp.stack(t) for t in zip(*states_p)]
    a_k_s, a_v_s, a_ki_s, b_k_s, b_v_s = [jnp.stack(t) for t in zip(*states_s)]
    return (yp, ys, a_k_p, a_v_p, a_ki_p, b_k_p, b_v_p, a_k_s, a_v_s, a_ki_s, b_k_s, b_v_s)
```

```python
import functools

import numpy as np
import jax
import jax.numpy as jnp
from jax import lax
from jax.experimental import pallas as pl
from jax.experimental.pallas import tpu as pltpu

F32 = jnp.float32
BF16 = jnp.bfloat16
I32 = jnp.int32

CHUNK = 64
CHUNK_SHIFT = 6
N_HEADS_A = 8
HD_A = 128
N_HEADS_IDX = 8
D_IDX = 64
TOPK_MAX = 256
N_HEADS_B = 8
HD_B = 64
N_PREV_CHUNKS = 8
BAND_BACK = N_PREV_CHUNKS * CHUNK
MAX_REL = 128
N_EXPERTS = 32
TOP_K = 4
D_FF = 1024
SWIGLU_LIMIT = 7.0
SWIGLU_ALPHA = 1.702
ROPE_THETA = 10000.0
EPS = 1e-6
NEG = -1e30
IDX_SCALE = (D_IDX ** -0.5) * (N_HEADS_IDX ** -0.5)
LOG2E = 1.4426950408889634
QK_SCALE_LOG2E = (HD_A ** -0.5) * LOG2E

LANES = 128
VMEM_LIMIT_BYTES = 56 * 1024 * 1024

WA_Q = N_HEADS_A * HD_A
WI_Q = N_HEADS_IDX * D_IDX
WB = N_HEADS_B * HD_B


def _pick_tile(n, target, mult):
    best = None
    for t in range(mult, min(n, target) + 1, mult):
        if n % t == 0:
            best = t
    return best if best is not None else n


def _cparams(n_axes):
    return pltpu.CompilerParams(dimension_semantics=("arbitrary",) * n_axes,
                                vmem_limit_bytes=VMEM_LIMIT_BYTES)


def _lane_iota(shape):
    return lax.broadcasted_iota(I32, shape, len(shape) - 1)


def _rms(x, g):
    ms = jnp.mean(x * x, axis=-1, keepdims=True)
    return x * lax.rsqrt(ms + EPS) * g


_C_QA = 0
_C_KA = _C_QA + WA_Q
_C_VA = _C_KA + HD_A
_C_QI = _C_VA + HD_A
_C_KI = _C_QI + WI_Q
_C_WI = _C_KI + LANES
_C_QB = _C_WI + LANES
_C_KB = _C_QB + WB
_C_VB = _C_KB + WB
_C_END = _C_VB + WB


def _proj_kernel(x_ref, gmix_ref, w_ref, gqa_ref, gka_ref, gki_ref, gqb_ref, gkb_ref,
                 cosa_ref, sina_ref, cosi_ref, sinia_ref, sinib_ref,
                 q_ref, qi_ref, wi_ref, kaf_ref, vaf_ref, kif_ref, kab_ref, vab_ref, ki2_ref,
                 qb_ref, kbf_ref, vbf_ref, kbb_ref, vbb_ref):
    x = x_ref[...]
    h = _rms(x, gmix_ref[...]).astype(BF16)

    def seg(a, b):
        return jnp.dot(h, w_ref[:, a:b], preferred_element_type=F32)

    cosa = cosa_ref[...]
    sina = sina_ref[...]
    cosi = cosi_ref[...]
    sinia = sinia_ref[...]
    sinib = sinib_ref[...]
    lane = _lane_iota((x.shape[0], LANES))
    lo_half = lane < HD_B

    def rope_a(n):
        return n * cosa + pltpu.roll(n, HD_A // 2, 1) * sina

    def rope_i(n):
        return n * cosi + pltpu.roll(n, LANES - D_IDX // 2, 1) * sinia + pltpu.roll(n, D_IDX // 2, 1) * sinib

    z = seg(_C_QA, _C_KA)
    gqa = gqa_ref[...]
    for hd in range(N_HEADS_A):
        zh = z[:, hd * HD_A:(hd + 1) * HD_A]
        q_ref[hd] = (rope_a(_rms(zh, gqa)) * QK_SCALE_LOG2E).astype(BF16)

    ka = rope_a(_rms(seg(_C_KA, _C_VA), gka_ref[...]))
    kaf_ref[...] = ka
    kab_ref[...] = ka.astype(BF16)
    va = seg(_C_VA, _C_QI)
    vaf_ref[...] = va
    vab_ref[...] = va.astype(BF16)

    zk = seg(_C_KI, _C_WI)
    ms = jnp.sum(zk * zk, axis=-1, keepdims=True) * (1.0 / D_IDX)
    ki = rope_i(zk * lax.rsqrt(ms + EPS) * gki_ref[...])
    kif_ref[...] = ki
    ki2_ref[...] = (ki + pltpu.roll(ki, D_IDX, 1)).astype(BF16)

    z = seg(_C_QI, _C_KI)
    for p in range(N_HEADS_IDX // 2):
        r = rope_i(z[:, p * LANES:(p + 1) * LANES])
        qi_ref[2 * p] = jnp.where(lo_half, r, 0.0).astype(BF16)
        qi_ref[2 * p + 1] = jnp.where(lo_half, 0.0, r).astype(BF16)

    wi_ref[...] = seg(_C_WI, _C_QB) * IDX_SCALE

    def norm_b(zb, g):
        sq = zb * zb
        s_all = jnp.sum(sq, axis=-1, keepdims=True)
        s_lo = jnp.sum(jnp.where(lo_half, sq, 0.0), axis=-1, keepdims=True)
        r_lo = lax.rsqrt(s_lo * (1.0 / HD_B) + EPS)
        r_hi = lax.rsqrt((s_all - s_lo) * (1.0 / HD_B) + EPS)
        return zb * jnp.where(lo_half, r_lo, r_hi) * g

    z = seg(_C_QB, _C_KB)
    gqb = gqb_ref[...]
    for p in range(N_HEADS_B // 2):
        n = norm_b(z[:, p * LANES:(p + 1) * LANES], gqb)
        qb_ref[2 * p] = jnp.where(lo_half, n, 0.0).astype(BF16)
        qb_ref[2 * p + 1] = jnp.where(lo_half, 0.0, n).astype(BF16)
    z = seg(_C_KB, _C_VB)
    gkb = gkb_ref[...]
    for p in range(N_HEADS_B // 2):
        n = norm_b(z[:, p * LANES:(p + 1) * LANES], gkb)
        kbf_ref[:, p * LANES:(p + 1) * LANES] = n
        kbb_ref[:, p * LANES:(p + 1) * LANES] = n.astype(BF16)
    z = seg(_C_VB, _C_END)
    vbf_ref[...] = z
    vbb_ref[...] = z.astype(BF16)


def _proj(x_all, pos_all, lw):
    t_all, d = x_all.shape
    tm = _pick_tile(t_all, 256, 16)
    w_in = lw['w_in']
    offs = np.cumsum((WA_Q, HD_A, HD_A, WI_Q, D_IDX, N_HEADS_IDX, WB, WB, WB))
    qa_w, ka_w, va_w, qi_w, ki_w, wi_w, qb_w, kb_w, vb_w = [
        w_in[:, a:b] for a, b in zip(np.concatenate([[0], offs[:-1]]), offs)]

    def padl(w):
        return jnp.pad(w, ((0, 0), (0, LANES - w.shape[1])))

    w_pack = jnp.concatenate([qa_w, ka_w, va_w, qi_w, padl(ki_w), padl(wi_w), qb_w, kb_w, vb_w],
                             axis=1).astype(BF16)

    posf = pos_all.astype(F32)[:, None]

    def tables(dh):
        inv = ROPE_THETA ** (-jnp.arange(0, dh, 2, dtype=F32) / dh)
        ang = posf * inv[None, :]
        return jnp.cos(ang), jnp.sin(ang)

    ca, sa = tables(HD_A)
    cosa = jnp.concatenate([ca, ca], axis=1)
    sina = jnp.concatenate([-sa, sa], axis=1)
    ci, si = tables(D_IDX)
    zi = jnp.zeros_like(si)
    cosi = jnp.concatenate([ci, ci, ci, ci], axis=1)
    sinia = jnp.concatenate([-si, zi, -si, zi], axis=1)
    sinib = jnp.concatenate([zi, si, zi, si], axis=1)

    row = lambda g: g.reshape(1, -1).astype(F32)
    gki = jnp.pad(lw['g_ki'], (0, LANES - D_IDX)).reshape(1, LANES)
    gqb = jnp.tile(lw['g_qb'], 2).reshape(1, LANES)
    gkb = jnp.tile(lw['g_kb'], 2).reshape(1, LANES)

    tok = lambda w: pl.BlockSpec((tm, w), lambda i: (i, 0))
    full = lambda a: pl.BlockSpec(a.shape, lambda i: (0,) * a.ndim)
    hm = pl.BlockSpec((N_HEADS_A, tm, LANES), lambda i: (0, i, 0))

    ins = [x_all, row(lw['g_mix']), w_pack, row(lw['g_qa']), row(lw['g_ka']), gki, gqb, gkb,
           cosa, sina, cosi, sinia, sinib]
    in_specs = [tok(d), full(ins[1]), full(w_pack), full(ins[3]), full(ins[4]), full(gki), full(gqb),
                full(gkb), tok(LANES), tok(LANES), tok(LANES), tok(LANES), tok(LANES)]
    sds = jax.ShapeDtypeStruct
    out_shape = [
        sds((N_HEADS_A, t_all, LANES), BF16),
        sds((N_HEADS_IDX, t_all, LANES), BF16),
        sds((t_all, LANES), F32),
        sds((t_all, HD_A), F32), sds((t_all, HD_A), F32), sds((t_all, LANES), F32),
        sds((t_all, HD_A), BF16), sds((t_all, HD_A), BF16), sds((t_all, LANES), BF16),
        sds((N_HEADS_B, t_all, LANES), BF16),
        sds((t_all, WB), F32), sds((t_all, WB), F32), sds((t_all, WB), BF16), sds((t_all, WB), BF16),
    ]
    out_specs = [hm, hm, tok(LANES), tok(HD_A), tok(HD_A), tok(LANES), tok(HD_A), tok(HD_A), tok(LANES),
                 hm, tok(WB), tok(WB), tok(WB), tok(WB)]
    return pl.pallas_call(
        _proj_kernel, grid=(t_all // tm,), in_specs=in_specs, out_specs=out_specs, out_shape=out_shape,
        compiler_params=_cparams(1), name="proj")(*ins)


BIS_UNROLL = 4
BIS_MAX_ROUNDS = 80
BIG = 1e38
CAND_DEPTH = 16


def _batcher_pairs(n):
    pairs = []
    p = 1
    while p < n:
        k = p
        while k >= 1:
            for j in range(k % p, n - k, 2 * k):
                for i in range(min(k, n - j - k)):
                    if (i + j) // (2 * p) == (i + j + k) // (2 * p):
                        pairs.append((i + j, i + j + k))
            k //= 2
        p *= 2
    return pairs


def _bitonic_pairs(n):
    pairs = []
    stride = n // 2
    while stride >= 1:
        pairs += [(i, i + stride) for i in range(n) if not i & stride]
        stride //= 2
    return pairs


_SORT16 = _batcher_pairs(CAND_DEPTH)
_BITONIC16 = _bitonic_pairs(CAND_DEPTH)


def _dsa_kernel(q_ref, qi_ref, wi_ref, ki2_ref, k_ref, v_ref, o_ref,
                keys_ref, wb_ref, lohi_ref, cnt_ref, m_ref, alpha_ref, acc_ref, s_ref, p_ref, tiec_ref,
                cand_ref, done_ref, *, tq, tk, nkt_max, topk, pos_base, n_valid):
    i = pl.program_id(1)
    pos0 = pos_base + i * tq
    k_end = ((pos0 + tq - 1) // CHUNK + 1) * CHUNK
    k_lim = jnp.minimum(k_end, n_valid)
    nkt4 = jnp.minimum(((k_lim + 4 * tk - 1) // (4 * tk)) * 4, nkt_max)
    ncol = tk // LANES
    nh = N_HEADS_A
    topk_f = float(topk)

    qrow = pos0 + lax.broadcasted_iota(I32, (tq, LANES), 0)
    qchunk = lax.shift_right_logical(qrow, CHUNK_SHIFT)
    lane = _lane_iota((tq, LANES))

    klim = jnp.minimum(lax.shift_left(qchunk + 1, CHUNK_SHIFT), n_valid)

    def admissible(j, c):
        return (j * tk + c * LANES + lane) < klim

    w = wi_ref[...]
    for h in range(N_HEADS_IDX):
        wb_ref[h] = jnp.broadcast_to(w[:, h:h + 1], (tq, LANES))
    qi2d = qi_ref[...].reshape(N_HEADS_IDX * tq, LANES)

    dn_t = (((1,), (1,)), ((), ()))

    def idx_dots(j):
        kt = ki2_ref[0, pl.ds(pl.multiple_of(j * tk, tk), tk), :]
        return lax.dot_general(qi2d, kt, dn_t, preferred_element_type=F32)

    def score_tile(j, slot):
        for c in range(ncol):
            tot = None
            for h in range(N_HEADS_IDX):
                r = jnp.maximum(s_ref[slot, h * tq:(h + 1) * tq, c * LANES:(c + 1) * LANES], 0.0)
                term = wb_ref[h] * r
                tot = term if tot is None else tot + term
            keys_ref[j, :, c * LANES:(c + 1) * LANES] = jnp.where(admissible(j, c), tot, NEG)

    s_ref[0] = idx_dots(0)
    s_ref[1] = idx_dots(1)

    def score_quad(it, carry):
        t0 = 4 * it
        s_ref[2] = idx_dots(t0 + 2)
        s_ref[3] = idx_dots(t0 + 3)
        score_tile(t0, 0)
        score_tile(t0 + 1, 1)
        s_ref[0] = idx_dots(jnp.minimum(t0 + 4, nkt4 - 2))
        s_ref[1] = idx_dots(jnp.minimum(t0 + 5, nkt4 - 1))
        score_tile(t0 + 2, 2)
        score_tile(t0 + 3, 3)
        return carry

    lax.fori_loop(0, nkt4 // 4, score_quad, 0)

    enough = klim.astype(F32) >= topk_f

    def build_candidates(qd, carry):
        for g in range(tq // 8):
            rows = slice(8 * g, 8 * g + 8)
            new = [keys_ref[4 * qd + t, rows, c * LANES:(c + 1) * LANES] for t in range(4) for c in range(ncol)]
            for a, b in _SORT16:
                new[a], new[b] = jnp.maximum(new[a], new[b]), jnp.minimum(new[a], new[b])
            top = [jnp.maximum(cand_ref[rows, b * LANES:(b + 1) * LANES], new[CAND_DEPTH - 1 - b])
                   for b in range(CAND_DEPTH)]
            for a, b in _BITONIC16:
                top[a], top[b] = jnp.maximum(top[a], top[b]), jnp.minimum(top[a], top[b])
            for b in range(CAND_DEPTH):
                cand_ref[rows, b * LANES:(b + 1) * LANES] = top[b]
        return carry

    cand_ref[...] = jnp.full(cand_ref.shape, -BIG, F32)
    lax.fori_loop(0, nkt4 // 4, build_candidates, 0)

    def count_cand(thr, strict):
        acc = jnp.zeros((tq, LANES), F32)
        for b in range(CAND_DEPTH):
            x = cand_ref[:, b * LANES:(b + 1) * LANES]
            acc = acc + jnp.where((x > thr) if strict else (x >= thr), 1.0, 0.0)
        return jnp.sum(acc, axis=1, keepdims=True)

    def count_keys(thr, strict):
        def tile(j, acc):
            for cc in range(ncol):
                x = keys_ref[j, :, cc * LANES:(cc + 1) * LANES]
                acc = acc + jnp.where((x > thr) if strict else (x >= thr), 1.0, 0.0)
            return acc
        return jnp.sum(lax.fori_loop(0, nkt4, tile, jnp.zeros((tq, LANES), F32)), axis=1, keepdims=True)

    zeros = jnp.zeros((tq, LANES), F32)
    rmax = jnp.max(cand_ref[:, 0:LANES], axis=1, keepdims=True)
    hi0 = jnp.where(enough, rmax + jnp.maximum(jnp.abs(rmax), 1e-30) * 1e-6, BIG) + zeros
    head = None
    for b in range(-(-topk // LANES)):
        x = cand_ref[:, b * LANES:(b + 1) * LANES]
        x = jnp.where(x > 0.5 * NEG, x, BIG)
        head = x if head is None else jnp.minimum(head, x)
    lo_try = jnp.min(head, axis=1, keepdims=True) + zeros

    def search(count):
        lo0 = jnp.where(enough & (count(lo_try, False) >= topk_f), lo_try, -BIG)
        lohi_ref[0] = lo0
        lohi_ref[1] = hi0
        cnt_ref[0] = jnp.where(enough, count(lo0, False), topk_f) + zeros
        cnt_ref[1] = zeros

        def unresolved():
            lo = lohi_ref[0]
            hi = lohi_ref[1]
            mid = lo + (hi - lo) * 0.5
            return (cnt_ref[0] != topk_f) & (mid > lo) & (mid < hi)

        def step():
            lo = lohi_ref[0]
            hi = lohi_ref[1]
            mid = lo + (hi - lo) * 0.5
            active = (cnt_ref[0] != topk_f) & (mid > lo) & (mid < hi) & (done_ref[...] == 0.0)
            cnt = count(mid, False)
            up = active & (cnt >= topk_f)
            dn = active & (cnt < topk_f)
            lohi_ref[0] = jnp.where(up, mid, lo)
            lohi_ref[1] = jnp.where(dn, mid, hi)
            cnt_ref[0] = jnp.where(up, cnt, cnt_ref[0])
            cnt_ref[1] = jnp.where(dn, cnt, cnt_ref[1])

        def body(c):
            it, _ = c
            for _ in range(BIS_UNROLL):
                step()
            ties_only = count(lohi_ref[0], True) == cnt_ref[1]
            done = jnp.where(unresolved() & jnp.logical_not(ties_only), 0.0, 1.0)
            done_ref[...] = done
            return it + 1, jnp.min(done)

        done_ref[...] = jnp.zeros((tq, LANES), F32)
        lax.while_loop(lambda c: (c[1] < 0.5) & (c[0] < BIS_MAX_ROUNDS), body, (jnp.int32(0), jnp.float32(0.0)))

    search(count_cand)
    full_lo = count_keys(lohi_ref[0], False)
    full_hi = count_keys(lohi_ref[1], False)
    agree = jnp.logical_not(enough) | ((full_lo == cnt_ref[0]) & (full_hi == cnt_ref[1]))

    @pl.when(jnp.min(jnp.where(agree, 1.0, 0.0)) < 0.5)
    def _():
        search(count_keys)

    lo = lohi_ref[0]
    hi = lohi_ref[1]
    need = topk_f - cnt_ref[1]
    tie_any = jnp.max(jnp.where(cnt_ref[0] > topk_f, 1.0, 0.0))

    m_ref[...] = jnp.full(m_ref.shape, NEG, F32)
    acc_ref[...] = jnp.zeros(acc_ref.shape, F32)
    tiec_ref[...] = jnp.zeros(tiec_ref.shape, F32)
    q2d = q_ref[...].reshape(nh * tq, LANES)
    ones_col = jnp.where(_lane_iota((tk, LANES)) == 0, 1.0, 0.0).astype(BF16)

    def qk_dots(j):
        kt = k_ref[0, pl.ds(pl.multiple_of(j * tk, tk), tk), :]
        return lax.dot_general(q2d, kt, dn_t, preferred_element_type=F32)

    def softmax_tile(j, slot, tie):
        if tie:
            kk = keys_ref[j]
            cand = [(kk[:, c * LANES:(c + 1) * LANES] >= lo) & (kk[:, c * LANES:(c + 1) * LANES] < hi)
                    for c in range(ncol)]
            candf = jnp.concatenate([jnp.where(cd, 1.0, 0.0) for cd in cand], axis=1)
            r_i = lax.broadcasted_iota(I32, (tk, tk), 0)
            c_i = lax.broadcasted_iota(I32, (tk, tk), 1)
            upper = jnp.where(r_i < c_i, 1.0, 0.0).astype(BF16)
            pref = jnp.dot(candf.astype(BF16), upper, preferred_element_type=F32)
            base = tiec_ref[...]
            sel = []
            for c in range(ncol):
                kc = kk[:, c * LANES:(c + 1) * LANES]
                rank = base + pref[:, c * LANES:(c + 1) * LANES]
                sel.append(((kc >= hi) | (cand[c] & (rank < need))) & admissible(j, c))
            tiec_ref[...] = base + jnp.sum(candf, axis=1, keepdims=True)
        else:
            rb = min(tq, 32)
            lane_r = _lane_iota((rb, LANES))
            for r0 in range(0, tq, rb):
                rs = slice(r0, r0 + rb)
                lo_r = lohi_ref[0, rs]
                qc_r = lax.shift_right_logical(pos0 + r0 + lax.broadcasted_iota(I32, (rb, LANES), 0), CHUNK_SHIFT)
                klim_r = jnp.minimum(lax.shift_left(qc_r + 1, CHUNK_SHIFT), n_valid)
                sel = []
                for c in range(ncol):
                    adm = (j * tk + c * LANES + lane_r) < klim_r
                    sel.append((keys_ref[j, rs, c * LANES:(c + 1) * LANES] >= lo_r) & adm)
                for h in range(nh):
                    hr = slice(h * tq + r0, h * tq + r0 + rb)
                    m_prev = m_ref[h, rs]
                    xs = [jnp.where(sel[c], s_ref[slot, hr, c * LANES:(c + 1) * LANES], NEG) for c in range(ncol)]
                    m_cur = xs[0]
                    for c in range(1, ncol):
                        m_cur = jnp.maximum(m_cur, xs[c])
                    m_new = jnp.maximum(m_prev, jnp.max(m_cur, axis=1, keepdims=True))
                    alpha_ref[slot, h, rs] = jnp.exp2(m_prev - m_new)
                    for c in range(ncol):
                        p_ref[slot, hr, c * LANES:(c + 1) * LANES] = jnp.exp2(xs[c] - m_new).astype(BF16)
                    m_ref[h, rs] = m_new
            return
        for h in range(nh):
            m_prev = m_ref[h]
            xs = [jnp.where(sel[c], s_ref[slot, h * tq:(h + 1) * tq, c * LANES:(c + 1) * LANES], NEG)
                  for c in range(ncol)]
            m_cur = xs[0]
            for c in range(1, ncol):
                m_cur = jnp.maximum(m_cur, xs[c])
            m_new = jnp.maximum(m_prev, jnp.max(m_cur, axis=1, keepdims=True))
            alpha_ref[slot, h] = jnp.exp2(m_prev - m_new)
            for c in range(ncol):
                p_ref[slot, h * tq:(h + 1) * tq, c * LANES:(c + 1) * LANES] = (
                    jnp.exp2(xs[c] - m_new).astype(BF16))
            m_ref[h] = m_new

    def pv_tile(j, slot):
        vt = v_ref[0, pl.ds(pl.multiple_of(j * tk, tk), tk), :]
        pv = jnp.dot(p_ref[slot], jnp.concatenate([vt, ones_col], axis=1), preferred_element_type=F32)
        for h in range(nh):
            alpha = alpha_ref[slot, h]
            for half in range(2):
                hs = slice(half * HD_A, (half + 1) * HD_A)
                acc_ref[h, :, hs] = acc_ref[h, :, hs] * alpha + pv[h * tq:(h + 1) * tq, hs]

    def attend(tie):
        npair = jnp.minimum((k_lim + 2 * tk - 1) // (2 * tk), nkt_max // 2)
        s_ref[0] = qk_dots(0)
        p_ref[1] = jnp.zeros(p_ref.shape[1:], BF16)
        alpha_ref[1] = jnp.ones(alpha_ref.shape[1:], F32)

        def pair(jj, carry):
            a = 2 * jj
            s_ref[1] = qk_dots(a + 1)
            softmax_tile(a, 0, tie)
            pv_tile(jnp.maximum(a - 1, 0), 1)
            s_ref[0] = qk_dots(jnp.minimum(a + 2, 2 * npair - 2))
            softmax_tile(a + 1, 1, tie)
            pv_tile(a, 0)
            return carry

        lax.fori_loop(0, npair, pair, 0)
        pv_tile(2 * npair - 1, 1)

    @pl.when(tie_any == 0)
    def _():
        attend(False)

    @pl.when(tie_any != 0)
    def _():
        attend(True)

    for h in range(nh):
        den = acc_ref[h, :, HD_A:HD_A + 1]
        o_ref[:, h * HD_A:(h + 1) * HD_A] = (acc_ref[h, :, :HD_A] / den).astype(o_ref.dtype)


def _dsa(q_hm, qi_hm, wi, ki2, k, v, *, n_batch, tq, n_qt, q_off, nk, topk, pos_base, n_valid):
    tk = _pick_tile(nk, 512, LANES)
    nkt_max = nk // tk
    assert nkt_max % 4 == 0 and 4 * (tk // LANES) == CAND_DEPTH and nk >= topk, "key tiles are merged in fours"
    kern = functools.partial(_dsa_kernel, tq=tq, tk=tk, nkt_max=nkt_max, topk=topk,
                             pos_base=pos_base, n_valid=n_valid)
    qmap = lambda b, i: (0, q_off + b * n_qt + i, 0)
    rmap = lambda b, i: (q_off + b * n_qt + i, 0)
    kmap = lambda b, i: (b, 0, 0)
    in_specs = [pl.BlockSpec((N_HEADS_A, tq, LANES), qmap), pl.BlockSpec((N_HEADS_IDX, tq, LANES), qmap),
                pl.BlockSpec((tq, LANES), rmap),
                pl.BlockSpec((1, nk, LANES), kmap), pl.BlockSpec((1, nk, LANES), kmap),
                pl.BlockSpec((1, nk, LANES), kmap)]
    scratch = [
        pltpu.VMEM((nkt_max, tq, tk), F32),
        pltpu.VMEM((N_HEADS_IDX, tq, LANES), F32),
        pltpu.VMEM((2, tq, LANES), F32),
        pltpu.VMEM((2, tq, LANES), F32),
        pltpu.VMEM((N_HEADS_A, tq, LANES), F32),
        pltpu.VMEM((2, N_HEADS_A, tq, LANES), F32),
        pltpu.VMEM((N_HEADS_A, tq, 2 * HD_A), F32),
        pltpu.VMEM((4, N_HEADS_A * tq, tk), F32),
        pltpu.VMEM((2, N_HEADS_A * tq, tk), BF16),
        pltpu.VMEM((tq, LANES), F32),
        pltpu.VMEM((tq, CAND_DEPTH * LANES), F32),
        pltpu.VMEM((tq, LANES), F32),
    ]
    return pl.pallas_call(
        kern, grid=(n_batch, n_qt), in_specs=in_specs,
        out_specs=pl.BlockSpec((tq, WA_Q), lambda b, i: (b * n_qt + i, 0)),
        out_shape=jax.ShapeDtypeStruct((n_batch * n_qt * tq, WA_Q), BF16),
        scratch_shapes=scratch,
        compiler_params=_cparams(2), name="dsa")(q_hm, qi_hm, wi, ki2, k, v)


def _band_kernel(q_ref, kp_ref, ko_ref, vp_ref, vo_ref, rext_ref, o_ref, bias_ref,
                 *, tq, tqo, n_own, off, prev_always):
    b = pl.program_id(0)
    i = pl.program_id(1)
    w = BAND_BACK + tqo
    scale = HD_B ** -0.5

    @pl.when((b == 0) & (i == 0))
    def _():
        ri = lax.broadcasted_iota(I32, (tq, w), 0)
        ci = lax.broadcasted_iota(I32, (tq, w), 1)
        qc = lax.shift_right_logical(ri, CHUNK_SHIFT)
        jo = ci - BAND_BACK
        valid_prev = (ci < BAND_BACK) & (lax.shift_right_logical(ci, CHUNK_SHIFT) >= qc)
        valid_own = (jo >= 0) & (jo < n_own) & (lax.shift_right_logical(jnp.maximum(jo, 0), CHUNK_SHIFT) <= qc)
        valid = valid_prev | valid_own
        for h in range(N_HEADS_B):
            pat = jnp.broadcast_to(rext_ref[h:h + 1, :], (tq, off + w))
            rolled = pltpu.roll(pat, 0, 1, stride=1, stride_axis=0)
            bias_ref[h] = jnp.where(valid, rolled[:, off:off + w], NEG)

    dead_cols = 0 if prev_always else jnp.where(i > 0, 0, BAND_BACK)
    lane = _lane_iota((tq, LANES))
    lo_half = lane < HD_B
    prev_dead = lax.broadcasted_iota(I32, (tq, w), 1) < dead_cols
    for p in range(N_HEADS_B // 2):
        sl = slice(p * LANES, (p + 1) * LANES)
        kcat = jnp.concatenate([kp_ref[0, :, sl], ko_ref[0, :, sl]], axis=0)
        vcat = jnp.concatenate([vp_ref[0, :, sl], vo_ref[0, :, sl]], axis=0)
        outs = []
        for e in range(2):
            h = 2 * p + e
            s = lax.dot_general(q_ref[h], kcat, (((1,), (1,)), ((), ())), preferred_element_type=F32)
            s = s * scale + bias_ref[h]
            s = jnp.where(prev_dead, NEG, s)
            m = jnp.max(s, axis=1, keepdims=True)
            pexp = jnp.exp(s - m)
            den = jnp.sum(pexp, axis=1, keepdims=True)
            pv = jnp.dot(pexp.astype(BF16), vcat, preferred_element_type=F32)
            outs.append(pv / den)
        o_ref[:, sl] = jnp.where(lo_half, outs[0], outs[1]).astype(o_ref.dtype)


def _band(qb_hm, kprev, kown, vprev, vown, rel_bias, *, n_batch, tq, tqo, n_qt, q_off, n_own,
          prev_always, prev_map, own_map):
    off = max(tq, LANES)
    off = ((off + LANES - 1) // LANES) * LANES
    w = BAND_BACK + tqo
    u = np.arange(off + w)
    idx = np.clip(BAND_BACK + off - u, -MAX_REL, MAX_REL) + MAX_REL
    rext = rel_bias.astype(F32)[:, idx]
    kern = functools.partial(_band_kernel, tq=tq, tqo=tqo, n_own=n_own, off=off, prev_always=prev_always)
    qmap = lambda b, i: (0, q_off + b * n_qt + i, 0)
    in_specs = [pl.BlockSpec((N_HEADS_B, tq, LANES), qmap),
                pl.BlockSpec((1, BAND_BACK, WB), prev_map), pl.BlockSpec((1, tqo, WB), own_map),
                pl.BlockSpec((1, BAND_BACK, WB), prev_map), pl.BlockSpec((1, tqo, WB), own_map),
                pl.BlockSpec(rext.shape, lambda b, i: (0, 0))]
    return pl.pallas_call(
        kern, grid=(n_batch, n_qt), in_specs=in_specs,
        out_specs=pl.BlockSpec((tq, WB), lambda b, i: (b * n_qt + i, 0)),
        out_shape=jax.ShapeDtypeStruct((n_batch * n_qt * tq, WB), BF16),
        scratch_shapes=[pltpu.VMEM((N_HEADS_B, tq, w), F32)],
        compiler_params=_cparams(2), name="band")(qb_hm, kprev, kown, vprev, vown, rext)


def _merge_kernel(x_ref, oap_ref, oas_ref, obp_ref, obs_ref, gmix_ref, wg_ref, bg_ref, wa_ref, wb_ref, wo_ref,
                  gffn_ref, wr_ref, br_ref, x1_ref, h2_ref, route_ref, wts_ref, cnt_ref, carry_ref, oa_ref, ob_ref,
                  *, n_prompt_tiles):
    @pl.when(pl.program_id(0) == 0)
    def _():
        carry_ref[...] = jnp.zeros(carry_ref.shape, F32)

    x = x_ref[...]
    d = x.shape[1]
    h = _rms(x, gmix_ref[...]).astype(BF16)
    gates = jax.nn.sigmoid(jnp.dot(h, wg_ref[...], preferred_element_type=F32) + bg_ref[...])
    @pl.when(pl.program_id(0) < n_prompt_tiles)
    def _():
        oa_ref[...] = oap_ref[...]
        ob_ref[...] = obp_ref[...]

    @pl.when(pl.program_id(0) >= n_prompt_tiles)
    def _():
        oa_ref[...] = oas_ref[...]
        ob_ref[...] = obs_ref[...]

    ya = jnp.dot(oa_ref[...], wa_ref[...], preferred_element_type=F32)
    yb = jnp.dot(ob_ref[...], wb_ref[...], preferred_element_type=F32)
    m = gates[:, :d] * ya + gates[:, d:] * yb
    x1 = x + jnp.dot(m.astype(BF16), wo_ref[...], preferred_element_type=F32)
    x1_ref[...] = x1
    h2 = _rms(x1, gffn_ref[...]).astype(BF16)
    for c in range(d // LANES):
        h2_ref[:, c, :] = h2[:, c * LANES:(c + 1) * LANES]
    logits = jnp.dot(h2, wr_ref[...], preferred_element_type=F32) + br_ref[...]
    tm = logits.shape[0]
    lane = _lane_iota(logits.shape)
    logits = jnp.where(lane < N_EXPERTS, logits, -jnp.inf)
    wts = jnp.zeros(logits.shape, F32)
    route = jnp.zeros(logits.shape, I32)
    onehot = jnp.zeros(logits.shape, F32)
    den = jnp.zeros((tm, 1), F32)
    picks = []
    v0 = None
    for k in range(TOP_K):
        mx = jnp.max(logits, axis=1, keepdims=True)
        idx = jnp.min(jnp.where(logits == mx, lane, LANES), axis=1, keepdims=True)
        pick = lane == idx
        if v0 is None:
            v0 = mx
        e = jnp.exp(mx - v0)
        wts = jnp.where(lane == k, e, wts)
        route = jnp.where(lane == k, idx, route)
        onehot = jnp.where(pick, 1.0, onehot)
        picks.append(pick)
        den = den + e
        logits = jnp.where(pick, -jnp.inf, logits)
    wts_ref[...] = wts / den
    r_i = lax.broadcasted_iota(I32, (tm, tm), 0)
    c_i = lax.broadcasted_iota(I32, (tm, tm), 1)
    earlier = jnp.where(c_i < r_i, 1.0, 0.0).astype(BF16)
    cum = carry_ref[...] + jnp.dot(earlier, onehot.astype(BF16), preferred_element_type=F32)
    for k in range(TOP_K):
        rank = jnp.sum(jnp.where(picks[k], cum, 0.0), axis=1, keepdims=True).astype(I32)
        route = jnp.where(lane == TOP_K + k, rank, route)
    route_ref[...] = route
    total = carry_ref[...] + jnp.sum(onehot, axis=0, keepdims=True)
    carry_ref[...] = total
    cnt_ref[...] = jnp.broadcast_to(total, cnt_ref.shape)


def _merge(x_all, oa_p, oa_s, ob_p, ob_s, lw):
    t_all, d = x_all.shape
    n_p, n_s = oa_p.shape[0], oa_s.shape[0]
    tm = _pick_tile(int(np.gcd(n_p, n_s)), 256, 16)
    n_pt = n_p // tm
    w_in = lw['w_in']
    w_gate = w_in[:, w_in.shape[1] - 2 * d:].astype(BF16)
    wr = jnp.pad(lw['w_router'], ((0, 0), (0, LANES - N_EXPERTS))).astype(BF16)
    br = jnp.pad(lw['b_router'], (0, LANES - N_EXPERTS)).reshape(1, LANES).astype(F32)
    row = lambda g: g.reshape(1, -1).astype(F32)
    ins = [x_all, oa_p, oa_s, ob_p, ob_s, row(lw['g_mix']), w_gate, row(lw['b_gate']), lw['w_br_a'].astype(BF16),
           lw['w_br_b'].astype(BF16), lw['w_out'].astype(BF16), row(lw['g_ffn']), wr, br]
    tok = lambda w: pl.BlockSpec((tm, w), lambda i: (i, 0))
    ptok = lambda w: pl.BlockSpec((tm, w), lambda i: (jnp.minimum(i, n_pt - 1), 0))
    stok = lambda w: pl.BlockSpec((tm, w), lambda i: (jnp.maximum(i - n_pt, 0), 0))
    full = lambda a: pl.BlockSpec(a.shape, lambda i: (0,) * a.ndim)
    in_specs = [tok(d), ptok(WA_Q), stok(WA_Q), ptok(WB), stok(WB)] + [full(a) for a in ins[5:]]
    sds = jax.ShapeDtypeStruct
    slabs = d // LANES
    return pl.pallas_call(
        functools.partial(_merge_kernel, n_prompt_tiles=n_pt), grid=(t_all // tm,), in_specs=in_specs,
        out_specs=[tok(d), pl.BlockSpec((tm, slabs, LANES), lambda i: (i, 0, 0)), tok(LANES), tok(LANES),
                   pl.BlockSpec((8, LANES), lambda i: (0, 0))],
        out_shape=[sds((t_all, d), F32),
                   sds((t_all, slabs, LANES), BF16),
                   sds((t_all, LANES), I32),
                   sds((t_all, LANES), F32),
                   sds((8, LANES), F32)],
        scratch_shapes=[pltpu.VMEM((1, LANES), F32), pltpu.VMEM((tm, WA_Q), BF16), pltpu.VMEM((tm, WB), BF16)],
        compiler_params=_cparams(1), name="merge")(*ins)


def _swiglu(u):
    glu = jnp.minimum(u[:, :D_FF], SWIGLU_LIMIT)
    lin = jnp.clip(u[:, D_FF:], -SWIGLU_LIMIT, SWIGLU_LIMIT)
    return glu * jax.nn.sigmoid(SWIGLU_ALPHA * glu) * (lin + 1.0)


MOE_ROWS = 512


def _route_plan(route, cnt, n_tiles):
    eid = route[:, :TOP_K]
    rank = route[:, TOP_K:2 * TOP_K]
    cnt_e = cnt[0, :N_EXPERTS].astype(I32)
    ntile = (cnt_e + MOE_ROWS - 1) // MOE_ROWS
    tile_end = jnp.cumsum(ntile)
    tile_start = tile_end - ntile
    row_start = tile_start * MOE_ROWS
    onehot = eid[:, :, None] == jnp.arange(N_EXPERTS, dtype=I32)[None, None, :]
    pos = jnp.sum(jnp.where(onehot, row_start[None, None, :], 0), axis=-1) + rank
    g = jnp.arange(n_tiles, dtype=I32)
    used = tile_end[-1]
    g_eff = jnp.minimum(g, used - 1)
    tile_e = jnp.minimum(jnp.sum(g_eff[:, None] >= tile_end[None, :], axis=1), N_EXPERTS - 1).astype(I32)
    rows = jnp.clip(cnt_e[tile_e] - (g - tile_start[tile_e]) * MOE_ROWS, 0, MOE_ROWS)
    rows = jnp.where(g < used, rows, 0).astype(I32)
    return pos.astype(I32), tile_e, rows


def _dispatch_kernel(pos_ref, h_ref, xs_in, xs_ref, sem, *, tm):
    del xs_in

    def issue(t, carry):
        for k in range(TOP_K):
            pltpu.make_async_copy(h_ref.at[t], xs_ref.at[pos_ref[0, 0, t * TOP_K + k]], sem).start()
        return carry

    lax.fori_loop(0, tm, issue, 0)
    for k in range(TOP_K):
        pltpu.make_async_copy(h_ref, xs_ref.at[pl.ds(0, tm)], sem).wait()


def _dispatch(h2, pos, n_rows):
    t_all, slabs, _ = h2.shape
    tm = _pick_tile(t_all, 256, 16)
    pos3 = pos.reshape(t_all // tm, 1, tm * TOP_K)
    xs0 = jnp.zeros((n_rows, slabs, LANES), h2.dtype)
    return pl.pallas_call(
        functools.partial(_dispatch_kernel, tm=tm), grid=(t_all // tm,),
        in_specs=[pl.BlockSpec((1, 1, tm * TOP_K), lambda i: (i, 0, 0), memory_space=pltpu.SMEM),
                  pl.BlockSpec((tm, slabs, LANES), lambda i: (i, 0, 0)),
                  pl.BlockSpec(memory_space=pl.ANY)],
        out_specs=pl.BlockSpec(memory_space=pl.ANY),
        out_shape=jax.ShapeDtypeStruct(xs0.shape, xs0.dtype),
        scratch_shapes=[pltpu.SemaphoreType.DMA(())],
        input_output_aliases={2: 0},
        compiler_params=_cparams(1), name="dispatch")(pos3, h2, xs0)


def _experts_kernel(te_ref, rows_ref, xs_ref, wu_ref, bu_ref, wd_ref, bd_ref, ys_ref, wub_ref, wdb_ref):
    g = pl.program_id(0)
    slabs = xs_ref.shape[1]

    @pl.when((g == 0) | (te_ref[g] != te_ref[jnp.maximum(g - 1, 0)]))
    def _():
        wub_ref[...] = wu_ref[0].astype(BF16)
        wdb_ref[...] = wd_ref[0].astype(BF16)

    @pl.when(rows_ref[g] > 0)
    def _():
        half = xs_ref.shape[0] // 2
        for r0 in (0, half):
            x = jnp.concatenate([xs_ref[r0:r0 + half, c, :] for c in range(slabs)], axis=1)
            u = jnp.dot(x, wub_ref[...], preferred_element_type=F32) + bu_ref[0]
            ys_ref[r0:r0 + half, :] = (
                jnp.dot(_swiglu(u).astype(BF16), wdb_ref[...], preferred_element_type=F32) + bd_ref[0])

    @pl.when(rows_ref[g] == 0)
    def _():
        ys_ref[...] = jnp.zeros(ys_ref.shape, F32)


def _experts(xs, tile_e, rows, lw):
    n_rows, slabs, _ = xs.shape
    d = slabs * LANES
    wu = lw['w_up'].astype(F32)
    wd = lw['w_down'].astype(F32)
    bu = lw['b_up'].reshape(N_EXPERTS, 1, 2 * D_FF).astype(F32)
    bd = lw['b_down'].reshape(N_EXPERTS, 1, d).astype(F32)
    tile = pl.BlockSpec((MOE_ROWS, slabs, LANES), lambda g, te, rw: (g, 0, 0))
    ex = lambda a: pl.BlockSpec((1,) + a.shape[1:], lambda g, te, rw: (te[g], 0, 0))
    grid_spec = pltpu.PrefetchScalarGridSpec(
        num_scalar_prefetch=2, grid=(n_rows // MOE_ROWS,),
        in_specs=[tile, ex(wu), ex(bu), ex(wd), ex(bd)],
        out_specs=pl.BlockSpec((MOE_ROWS, d), lambda g, te, rw: (g, 0)),
        scratch_shapes=[pltpu.VMEM(wu.shape[1:], BF16), pltpu.VMEM(wd.shape[1:], BF16)])
    return pl.pallas_call(
        _experts_kernel, grid_spec=grid_spec,
        out_shape=jax.ShapeDtypeStruct((n_rows, d), F32),
        compiler_params=_cparams(1), name="experts")(tile_e, rows, xs, wu, bu, wd, bd)


def _combine_kernel(pos_ref, posn_ref, x1_ref, wts_ref, ys_ref, yp_ref, ys_out_ref, buf_ref, sem,
                    *, tm, n_prompt_tiles):
    i = pl.program_id(0)
    slot = lax.rem(i, 2)

    def fetch(p_ref, sl):
        def issue(t, carry):
            for k in range(TOP_K):
                pltpu.make_async_copy(ys_ref.at[pl.ds(p_ref[0, 0, t * TOP_K + k], 1)],
                                      buf_ref.at[sl, k, pl.ds(t, 1)], sem.at[sl]).start()
            return carry

        lax.fori_loop(0, tm, issue, 0)

    @pl.when(i == 0)
    def _():
        fetch(pos_ref, 0)

    @pl.when(i + 1 < pl.num_programs(0))
    def _():
        fetch(posn_ref, 1 - slot)

    for k in range(TOP_K):
        pltpu.make_async_copy(ys_ref.at[pl.ds(0, tm)], buf_ref.at[slot, k], sem.at[slot]).wait()
    w = wts_ref[...]
    acc = x1_ref[...]
    for k in range(TOP_K):
        acc = acc + w[:, k:k + 1] * buf_ref[slot, k]

    @pl.when(pl.program_id(0) < n_prompt_tiles)
    def _():
        yp_ref[...] = acc

    @pl.when(pl.program_id(0) >= n_prompt_tiles)
    def _():
        ys_out_ref[...] = acc


def _combine(x1, wts, ys, pos, n_prompt):
    t_all, d = x1.shape
    n_s = t_all - n_prompt
    tm = _pick_tile(int(np.gcd(n_prompt, n_s)), 256, 16)
    n_pt = n_prompt // tm
    n_t = t_all // tm
    pos3 = pos.reshape(n_t, 1, tm * TOP_K)
    return pl.pallas_call(
        functools.partial(_combine_kernel, tm=tm, n_prompt_tiles=n_pt), grid=(n_t,),
        in_specs=[pl.BlockSpec((1, 1, tm * TOP_K), lambda i: (i, 0, 0), memory_space=pltpu.SMEM),
                  pl.BlockSpec((1, 1, tm * TOP_K), lambda i: (jnp.minimum(i + 1, n_t - 1), 0, 0),
                               memory_space=pltpu.SMEM),
                  pl.BlockSpec((tm, d), lambda i: (i, 0)), pl.BlockSpec((tm, LANES), lambda i: (i, 0)),
                  pl.BlockSpec(memory_space=pl.ANY)],
        out_specs=[pl.BlockSpec((tm, d), lambda i: (jnp.minimum(i, n_pt - 1), 0)),
                   pl.BlockSpec((tm, d), lambda i: (jnp.maximum(i - n_pt, 0), 0))],
        out_shape=[jax.ShapeDtypeStruct((n_prompt, d), F32), jax.ShapeDtypeStruct((n_s, d), F32)],
        scratch_shapes=[pltpu.VMEM((2, TOP_K, tm, d), F32), pltpu.SemaphoreType.DMA((2,))],
        compiler_params=_cparams(1), name="combine")(pos3, pos3, x1, wts, ys)


def _moe(x1, h2, route, wts, cnt, lw, n_prompt):
    t_all = x1.shape[0]
    n_tiles = (TOP_K * t_all) // MOE_ROWS + N_EXPERTS
    pos, tile_e, rows = _route_plan(route, cnt, n_tiles)
    xs = _dispatch(h2, pos, n_tiles * MOE_ROWS)
    ys = _experts(xs, tile_e, rows, lw)
    return _combine(x1, wts, ys, pos, n_prompt)


def _layer(xp, xs, a_k, a_v, a_kidx, b_k, b_v, lw):
    _, s, d = xp.shape
    bs, ts, _ = xs.shape
    p_len = a_k.shape[1]
    t_s = bs * ts
    t_all = s + t_s
    x_all = jnp.concatenate([xp.reshape(s, d), xs.reshape(t_s, d)], axis=0)
    pos_all = jnp.concatenate([jnp.arange(s, dtype=I32), jnp.tile(p_len + jnp.arange(ts, dtype=I32), bs)])

    (q_hm, qi_hm, wi, kaf, vaf, kif, kab, vab, ki2, qb_hm, kbf, vbf, kbb, vbb) = _proj(x_all, pos_all, lw)

    tq_p = _pick_tile(s, 128, CHUNK)
    oa_p = _dsa(q_hm, qi_hm, wi, ki2[None], kab[None], vab[None],
                n_batch=1, tq=tq_p, n_qt=s // tq_p, q_off=0, nk=s, topk=min(TOPK_MAX, s // 4),
                pos_base=0, n_valid=s)
    n_keys = p_len + ts
    nk_s = ((n_keys + 2047) // 2048) * 2048
    pad_s = nk_s - n_keys

    def with_new(cache_bf, new_rows):
        return jnp.concatenate([cache_bf, new_rows.reshape(bs, ts, LANES),
                                jnp.zeros((bs, pad_s, LANES), BF16)], axis=1)

    kidx_c = a_kidx.astype(BF16)
    k_s = with_new(a_k.reshape(bs, p_len, HD_A).astype(BF16), kab[s:])
    v_s = with_new(a_v.reshape(bs, p_len, HD_A).astype(BF16), vab[s:])
    ki2_s = with_new(jnp.concatenate([kidx_c, kidx_c], axis=-1), ki2[s:])
    oa_s = _dsa(q_hm, qi_hm, wi, ki2_s, k_s, v_s,
                n_batch=bs, tq=ts, n_qt=1, q_off=s // ts, nk=nk_s, topk=min(TOPK_MAX, n_keys // 4),
                pos_base=p_len, n_valid=n_keys)

    tq_b = BAND_BACK
    ob_p = _band(qb_hm, kbb[None], kbb[None], vbb[None], vbb[None], lw['rel_bias'],
                 n_batch=1, tq=tq_b, tqo=tq_b, n_qt=s // tq_b, q_off=0, n_own=tq_b, prev_always=False,
                 prev_map=lambda b, i: (0, jnp.maximum(i - 1, 0), 0), own_map=lambda b, i: (0, i, 0))
    own_pad = LANES - ts
    kown_s = jnp.pad(kbb[s:].reshape(bs, ts, WB), ((0, 0), (0, own_pad), (0, 0)))
    vown_s = jnp.pad(vbb[s:].reshape(bs, ts, WB), ((0, 0), (0, own_pad), (0, 0)))
    bk2 = b_k.reshape(bs, BAND_BACK, WB)
    bv2 = b_v.reshape(bs, BAND_BACK, WB)
    ob_s = _band(qb_hm, bk2.astype(BF16), kown_s, bv2.astype(BF16), vown_s, lw['rel_bias'],
                 n_batch=bs, tq=ts, tqo=LANES, n_qt=1, q_off=s // ts, n_own=ts, prev_always=True,
                 prev_map=lambda b, i: (b, 0, 0), own_map=lambda b, i: (b, 0, 0))

    x1, h2, route, wts, cnt = _merge(x_all, oa_p, oa_s, ob_p, ob_s, lw)
    y_p, y_s = _moe(x1, h2, route, wts, cnt, lw, s)

    keep = min(BAND_BACK, s)
    st_p = (kaf[:s].reshape(1, s, 1, HD_A), vaf[:s].reshape(1, s, 1, HD_A), kif[:s, :D_IDX].reshape(1, s, D_IDX),
            kbf[s - keep:s].reshape(1, keep, N_HEADS_B, HD_B), vbf[s - keep:s].reshape(1, keep, N_HEADS_B, HD_B))
    kb_new = kbf[s:].reshape(bs, ts, N_HEADS_B, HD_B)
    vb_new = vbf[s:].reshape(bs, ts, N_HEADS_B, HD_B)
    st_s = (kaf[s:].reshape(bs, ts, 1, HD_A), vaf[s:].reshape(bs, ts, 1, HD_A),
            kif[s:, :D_IDX].reshape(bs, ts, D_IDX),
            jnp.concatenate([b_k, kb_new], axis=1)[:, ts:], jnp.concatenate([b_v, vb_new], axis=1)[:, ts:])
    return y_p.reshape(1, s, d), y_s.reshape(bs, ts, d), st_p, st_s


def kernel(x_prompt, x_sample, cache_a_k, cache_a_v, cache_a_kidx, state_b_k, state_b_v,
           g_mix, w_in, b_gate, g_qa, g_ka, g_ki, g_qb, g_kb, rel_bias, w_br_a, w_br_b, w_out,
           g_ffn, w_router, b_router, w_up, b_up, w_down, b_down):
    assert x_prompt.shape[0] == 1, "prompt batch is folded into the token axis; one stream supported"
    depth = g_mix.shape[0]
    yp, ys = x_prompt, x_sample
    states_p, states_s = [], []
    for l in range(depth):
        lw = dict(g_mix=g_mix[l], w_in=w_in[l], b_gate=b_gate[l], g_qa=g_qa[l], g_ka=g_ka[l], g_ki=g_ki[l],
                  g_qb=g_qb[l], g_kb=g_kb[l], rel_bias=rel_bias[l], w_br_a=w_br_a[l], w_br_b=w_br_b[l],
                  w_out=w_out[l], g_ffn=g_ffn[l], w_router=w_router[l], b_router=b_router[l],
                  w_up=w_up[l], b_up=b_up[l], w_down=w_down[l], b_down=b_down[l])
        yp, ys, st_p, st_s = _layer(yp, ys, cache_a_k[l], cache_a_v[l], cache_a_kidx[l],
                                    state_b_k[l], state_b_v[l], lw)
        states_p.append(st_p)
        states_s.append(st_s)
    a_k_p, a_v_p, a_ki_p, b_k_p, b_v_p = [jnp.stack(t) for t in zip(*states_p)]
    a_k_s, a_v_s, a_ki_s, b_k_s, b_v_s = [jnp.stack(t) for t in zip(*states_s)]
    return (yp, ys, a_k_p, a_v_p, a_ki_p, b_k_p, b_v_p, a_k_s, a_v_s, a_ki_s, b_k_s, b_v_s)
```

```python
import functools

import numpy as np
import jax
import jax.numpy as jnp
from jax import lax
from jax.experimental import pallas as pl
from jax.experimental.pallas import tpu as pltpu

F32 = jnp.float32
BF16 = jnp.bfloat16
I32 = jnp.int32

CHUNK = 64
CHUNK_SHIFT = 6
N_HEADS_A = 8
HD_A = 128
N_HEADS_IDX = 8
D_IDX = 64
TOPK_MAX = 256
N_HEADS_B = 8
HD_B = 64
N_PREV_CHUNKS = 8
BAND_BACK = N_PREV_CHUNKS * CHUNK
MAX_REL = 128
N_EXPERTS = 32
TOP_K = 4
D_FF = 1024
SWIGLU_LIMIT = 7.0
SWIGLU_ALPHA = 1.702
ROPE_THETA = 10000.0
EPS = 1e-6
NEG = -1e30
IDX_SCALE = (D_IDX ** -0.5) * (N_HEADS_IDX ** -0.5)
LOG2E = 1.4426950408889634
QK_SCALE_LOG2E = (HD_A ** -0.5) * LOG2E

LANES = 128
VMEM_LIMIT_BYTES = 56 * 1024 * 1024

WA_Q = N_HEADS_A * HD_A
WI_Q = N_HEADS_IDX * D_IDX
WB = N_HEADS_B * HD_B


def _pick_tile(n, target, mult):
    best = None
    for t in range(mult, min(n, target) + 1, mult):
        if n % t == 0:
            best = t
    return best if best is not None else n


def _cparams(n_axes):
    return pltpu.CompilerParams(dimension_semantics=("arbitrary",) * n_axes,
                                vmem_limit_bytes=VMEM_LIMIT_BYTES)


def _lane_iota(shape):
    return lax.broadcasted_iota(I32, shape, len(shape) - 1)


def _rms(x, g):
    ms = jnp.mean(x * x, axis=-1, keepdims=True)
    return x * lax.rsqrt(ms + EPS) * g


_C_QA = 0
_C_KA = _C_QA + WA_Q
_C_VA = _C_KA + HD_A
_C_QI = _C_VA + HD_A
_C_KI = _C_QI + WI_Q
_C_WI = _C_KI + LANES
_C_QB = _C_WI + LANES
_C_KB = _C_QB + WB
_C_VB = _C_KB + WB
_C_END = _C_VB + WB


def _proj_kernel(xp_ref, xs_ref, gmix_ref, w_ref, gqa_ref, gka_ref, gki_ref, gqb_ref, gkb_ref,
                 bca_ref, bsa_ref, oca_ref, osa_ref, bci_ref, bsi_ref, oci_ref, osi_ref,
                 q_ref, qi_ref, wi_ref, kaf_ref, vaf_ref, kif_ref, kab_ref, vab_ref, ki2_ref,
                 qb_ref, kbf_ref, vbf_ref, kbb_ref, vbb_ref, x_ref, *, n_prompt_tiles):
    @pl.when(pl.program_id(0) < n_prompt_tiles)
    def _():
        x_ref[...] = xp_ref[...]

    @pl.when(pl.program_id(0) >= n_prompt_tiles)
    def _():
        x_ref[...] = xs_ref[...]

    x = x_ref[...]
    h = _rms(x, gmix_ref[...]).astype(BF16)

    def seg(a, b):
        return jnp.dot(h, w_ref[:, a:b], preferred_element_type=F32)

    lane = _lane_iota((x.shape[0], LANES))
    lo_half = lane < HD_B
    ca = bca_ref[0] * oca_ref[0] - bsa_ref[0] * osa_ref[0]
    sa = bsa_ref[0] * oca_ref[0] + bca_ref[0] * osa_ref[0]
    cosa = ca
    sina = jnp.where(lo_half, -sa, sa)
    ci = bci_ref[0] * oci_ref[0] - bsi_ref[0] * osi_ref[0]
    si = bsi_ref[0] * oci_ref[0] + bci_ref[0] * osi_ref[0]
    first_half = jnp.bitwise_and(lane, D_IDX - 1) < D_IDX // 2
    cosi = ci
    sinia = jnp.where(first_half, -si, 0.0)
    sinib = jnp.where(first_half, 0.0, si)

    def rope_a(n):
        return n * cosa + pltpu.roll(n, HD_A // 2, 1) * sina

    def rope_i(n):
        return n * cosi + pltpu.roll(n, LANES - D_IDX // 2, 1) * sinia + pltpu.roll(n, D_IDX // 2, 1) * sinib

    z = seg(_C_QA, _C_KA)
    gqa = gqa_ref[...]
    for hd in range(N_HEADS_A):
        zh = z[:, hd * HD_A:(hd + 1) * HD_A]
        q_ref[hd] = (rope_a(_rms(zh, gqa)) * QK_SCALE_LOG2E).astype(BF16)

    ka = rope_a(_rms(seg(_C_KA, _C_VA), gka_ref[...]))
    kaf_ref[...] = ka
    kab_ref[...] = ka.astype(BF16)
    va = seg(_C_VA, _C_QI)
    vaf_ref[...] = va
    vab_ref[...] = va.astype(BF16)

    zk = seg(_C_KI, _C_WI)
    ms = jnp.sum(zk * zk, axis=-1, keepdims=True) * (1.0 / D_IDX)
    ki = rope_i(zk * lax.rsqrt(ms + EPS) * gki_ref[...])
    kif_ref[...] = ki
    ki2_ref[...] = (ki + pltpu.roll(ki, D_IDX, 1)).astype(BF16)

    z = seg(_C_QI, _C_KI)
    for p in range(N_HEADS_IDX // 2):
        r = rope_i(z[:, p * LANES:(p + 1) * LANES])
        qi_ref[2 * p] = jnp.where(lo_half, r, 0.0).astype(BF16)
        qi_ref[2 * p + 1] = jnp.where(lo_half, 0.0, r).astype(BF16)

    wi_ref[...] = seg(_C_WI, _C_QB) * IDX_SCALE

    def norm_b(zb, g):
        sq = zb * zb
        s_all = jnp.sum(sq, axis=-1, keepdims=True)
        s_lo = jnp.sum(jnp.where(lo_half, sq, 0.0), axis=-1, keepdims=True)
        r_lo = lax.rsqrt(s_lo * (1.0 / HD_B) + EPS)
        r_hi = lax.rsqrt((s_all - s_lo) * (1.0 / HD_B) + EPS)
        return zb * jnp.where(lo_half, r_lo, r_hi) * g

    z = seg(_C_QB, _C_KB)
    gqb = gqb_ref[...]
    for p in range(N_HEADS_B // 2):
        n = norm_b(z[:, p * LANES:(p + 1) * LANES], gqb)
        qb_ref[2 * p] = jnp.where(lo_half, n, 0.0).astype(BF16)
        qb_ref[2 * p + 1] = jnp.where(lo_half, 0.0, n).astype(BF16)
    z = seg(_C_KB, _C_VB)
    gkb = gkb_ref[...]
    for p in range(N_HEADS_B // 2):
        n = norm_b(z[:, p * LANES:(p + 1) * LANES], gkb)
        kbf_ref[:, p * LANES:(p + 1) * LANES] = n
        kbb_ref[:, p * LANES:(p + 1) * LANES] = n.astype(BF16)
    z = seg(_C_VB, _C_END)
    vbf_ref[...] = z
    vbb_ref[...] = z.astype(BF16)


def _proj(x_p, x_s, p_len, ts, lw):
    n_p, d = x_p.shape
    n_s = x_s.shape[0]
    t_all = n_p + n_s
    tm = _pick_tile(int(np.gcd(n_p, n_s)), 256, 16)
    assert tm % ts == 0, "a sample tile holds whole streams"
    n_pt, n_t = n_p // tm, t_all // tm
    w_in = lw['w_in']
    offs = np.cumsum((WA_Q, HD_A, HD_A, WI_Q, D_IDX, N_HEADS_IDX, WB, WB, WB))
    qa_w, ka_w, va_w, qi_w, ki_w, wi_w, qb_w, kb_w, vb_w = [
        w_in[:, a:b] for a, b in zip(np.concatenate([[0], offs[:-1]]), offs)]

    def padl(w):
        return jnp.pad(w, ((0, 0), (0, LANES - w.shape[1])))

    w_pack = jnp.concatenate([qa_w, ka_w, va_w, qi_w, padl(ki_w), padl(wi_w), qb_w, kb_w, vb_w],
                             axis=1).astype(BF16)

    base = np.concatenate([np.arange(n_pt) * tm, np.full(n_t - n_pt, p_len)]).astype(np.float32)
    offs_rows = np.stack([np.arange(tm), np.arange(tm) % ts]).astype(np.float32)

    def tables(dh):
        inv = ROPE_THETA ** (-jnp.arange(0, dh, 2, dtype=F32) / dh)
        inv = jnp.tile(inv, LANES // inv.shape[0])
        ang_b = jnp.asarray(base)[:, None, None] * inv[None, None, :]
        ang_o = jnp.asarray(offs_rows)[:, :, None] * inv[None, None, :]
        return jnp.cos(ang_b), jnp.sin(ang_b), jnp.cos(ang_o), jnp.sin(ang_o)

    rope_tabs = list(tables(HD_A)) + list(tables(D_IDX))

    row = lambda g: g.reshape(1, -1).astype(F32)
    gki = jnp.pad(lw['g_ki'], (0, LANES - D_IDX)).reshape(1, LANES)
    gqb = jnp.tile(lw['g_qb'], 2).reshape(1, LANES)
    gkb = jnp.tile(lw['g_kb'], 2).reshape(1, LANES)

    tok = lambda w: pl.BlockSpec((tm, w), lambda i: (i, 0))
    full = lambda a: pl.BlockSpec(a.shape, lambda i: (0,) * a.ndim)
    hm = pl.BlockSpec((N_HEADS_A, tm, LANES), lambda i: (0, i, 0))
    base_spec = pl.BlockSpec((1, 1, LANES), lambda i: (i, 0, 0))
    offs_spec = pl.BlockSpec((1, tm, LANES), lambda i: (jnp.where(i < n_pt, 0, 1), 0, 0))

    ins = [x_p, x_s, row(lw['g_mix']), w_pack, row(lw['g_qa']), row(lw['g_ka']), gki, gqb, gkb] + rope_tabs
    in_specs = [pl.BlockSpec((tm, d), lambda i: (jnp.minimum(i, n_pt - 1), 0)),
                pl.BlockSpec((tm, d), lambda i: (jnp.maximum(i - n_pt, 0), 0)),
                full(ins[2]), full(w_pack), full(ins[4]), full(ins[5]), full(gki), full(gqb), full(gkb),
                base_spec, base_spec, offs_spec, offs_spec, base_spec, base_spec, offs_spec, offs_spec]
    sds = jax.ShapeDtypeStruct
    out_shape = [
        sds((N_HEADS_A, t_all, LANES), BF16),
        sds((N_HEADS_IDX, t_all, LANES), BF16),
        sds((t_all, LANES), F32),
        sds((t_all, HD_A), F32), sds((t_all, HD_A), F32), sds((t_all, LANES), F32),
        sds((t_all, HD_A), BF16), sds((t_all, HD_A), BF16), sds((t_all, LANES), BF16),
        sds((N_HEADS_B, t_all, LANES), BF16),
        sds((t_all, WB), F32), sds((t_all, WB), F32), sds((t_all, WB), BF16), sds((t_all, WB), BF16),
    ]
    out_specs = [hm, hm, tok(LANES), tok(HD_A), tok(HD_A), tok(LANES), tok(HD_A), tok(HD_A), tok(LANES),
                 hm, tok(WB), tok(WB), tok(WB), tok(WB)]
    return pl.pallas_call(
        functools.partial(_proj_kernel, n_prompt_tiles=n_pt), grid=(n_t,), in_specs=in_specs,
        out_specs=out_specs, out_shape=out_shape, scratch_shapes=[pltpu.VMEM((tm, d), F32)],
        compiler_params=_cparams(1), name="proj")(*ins)


BIS_UNROLL = 4
BIS_MAX_ROUNDS = 80
BIG = 1e38
CAND_DEPTH = 16


def _batcher_pairs(n):
    pairs = []
    p = 1
    while p < n:
        k = p
        while k >= 1:
            for j in range(k % p, n - k, 2 * k):
                for i in range(min(k, n - j - k)):
                    if (i + j) // (2 * p) == (i + j + k) // (2 * p):
                        pairs.append((i + j, i + j + k))
            k //= 2
        p *= 2
    return pairs


def _bitonic_pairs(n):
    pairs = []
    stride = n // 2
    while stride >= 1:
        pairs += [(i, i + stride) for i in range(n) if not i & stride]
        stride //= 2
    return pairs


_SORT16 = _batcher_pairs(CAND_DEPTH)
_BITONIC16 = _bitonic_pairs(CAND_DEPTH)


def _dsa_kernel(q_ref, qi_ref, wi_ref, ki2_ref, k_ref, v_ref, o_ref,
                keys_ref, wb_ref, lohi_ref, cnt_ref, m_ref, alpha_ref, acc_ref, s_ref, p_ref, tiec_ref,
                cand_ref, done_ref, *, tq, tk, nkt_max, topk, pos_base, n_valid):
    i = pl.program_id(1)
    pos0 = pos_base + i * tq
    k_end = ((pos0 + tq - 1) // CHUNK + 1) * CHUNK
    k_lim = jnp.minimum(k_end, n_valid)
    nkt4 = jnp.minimum(((k_lim + 4 * tk - 1) // (4 * tk)) * 4, nkt_max)
    ncol = tk // LANES
    nh = N_HEADS_A
    topk_f = float(topk)

    qrow = pos0 + lax.broadcasted_iota(I32, (tq, LANES), 0)
    qchunk = lax.shift_right_logical(qrow, CHUNK_SHIFT)
    lane = _lane_iota((tq, LANES))

    klim = jnp.minimum(lax.shift_left(qchunk + 1, CHUNK_SHIFT), n_valid)

    def admissible(j, c):
        return (j * tk + c * LANES + lane) < klim

    w = wi_ref[...]
    for h in range(N_HEADS_IDX):
        wb_ref[h] = jnp.broadcast_to(w[:, h:h + 1], (tq, LANES))
    qi2d = qi_ref[...].reshape(N_HEADS_IDX * tq, LANES)

    dn_t = (((1,), (1,)), ((), ()))

    def idx_dots(j):
        kt = ki2_ref[0, pl.ds(pl.multiple_of(j * tk, tk), tk), :]
        return lax.dot_general(qi2d, kt, dn_t, preferred_element_type=F32)

    def score_tile(j, slot):
        for c in range(ncol):
            tot = None
            for h in range(N_HEADS_IDX):
                r = jnp.maximum(s_ref[slot, h * tq:(h + 1) * tq, c * LANES:(c + 1) * LANES], 0.0)
                term = wb_ref[h] * r
                tot = term if tot is None else tot + term
            keys_ref[j, :, c * LANES:(c + 1) * LANES] = jnp.where(admissible(j, c), tot, NEG)

    s_ref[0] = idx_dots(0)
    s_ref[1] = idx_dots(1)

    def score_quad(it, carry):
        t0 = 4 * it
        s_ref[2] = idx_dots(t0 + 2)
        s_ref[3] = idx_dots(t0 + 3)
        score_tile(t0, 0)
        score_tile(t0 + 1, 1)
        s_ref[0] = idx_dots(jnp.minimum(t0 + 4, nkt4 - 2))
        s_ref[1] = idx_dots(jnp.minimum(t0 + 5, nkt4 - 1))
        score_tile(t0 + 2, 2)
        score_tile(t0 + 3, 3)
        return carry

    lax.fori_loop(0, nkt4 // 4, score_quad, 0)

    enough = klim.astype(F32) >= topk_f

    def build_candidates(qd, carry):
        for g in range(tq // 8):
            rows = slice(8 * g, 8 * g + 8)
            new = [keys_ref[4 * qd + t, rows, c * LANES:(c + 1) * LANES] for t in range(4) for c in range(ncol)]
            for a, b in _SORT16:
                new[a], new[b] = jnp.maximum(new[a], new[b]), jnp.minimum(new[a], new[b])
            top = [jnp.maximum(cand_ref[rows, b * LANES:(b + 1) * LANES], new[CAND_DEPTH - 1 - b])
                   for b in range(CAND_DEPTH)]
            for a, b in _BITONIC16:
                top[a], top[b] = jnp.maximum(top[a], top[b]), jnp.minimum(top[a], top[b])
            for b in range(CAND_DEPTH):
                cand_ref[rows, b * LANES:(b + 1) * LANES] = top[b]
        return carry

    cand_ref[...] = jnp.full(cand_ref.shape, -BIG, F32)
    lax.fori_loop(0, nkt4 // 4, build_candidates, 0)

    def count_cand(thr, strict):
        acc = jnp.zeros((tq, LANES), F32)
        for b in range(CAND_DEPTH):
            x = cand_ref[:, b * LANES:(b + 1) * LANES]
            acc = acc + jnp.where((x > thr) if strict else (x >= thr), 1.0, 0.0)
        return jnp.sum(acc, axis=1, keepdims=True)

    def count_keys(thr, strict):
        def tile(j, acc):
            for cc in range(ncol):
                x = keys_ref[j, :, cc * LANES:(cc + 1) * LANES]
                acc = acc + jnp.where((x > thr) if strict else (x >= thr), 1.0, 0.0)
            return acc
        return jnp.sum(lax.fori_loop(0, nkt4, tile, jnp.zeros((tq, LANES), F32)), axis=1, keepdims=True)

    zeros = jnp.zeros((tq, LANES), F32)
    rmax = jnp.max(cand_ref[:, 0:LANES], axis=1, keepdims=True)
    hi0 = jnp.where(enough, rmax + jnp.maximum(jnp.abs(rmax), 1e-30) * 1e-6, BIG) + zeros
    head = None
    for b in range(-(-topk // LANES)):
        x = cand_ref[:, b * LANES:(b + 1) * LANES]
        x = jnp.where(x > 0.5 * NEG, x, BIG)
        head = x if head is None else jnp.minimum(head, x)
    lo_try = jnp.min(head, axis=1, keepdims=True) + zeros

    def search(count):
        lo0 = jnp.where(enough & (count(lo_try, False) >= topk_f), lo_try, -BIG)
        lohi_ref[0] = lo0
        lohi_ref[1] = hi0
        cnt_ref[0] = jnp.where(enough, count(lo0, False), topk_f) + zeros
        cnt_ref[1] = zeros

        def unresolved():
            lo = lohi_ref[0]
            hi = lohi_ref[1]
            mid = lo + (hi - lo) * 0.5
            return (cnt_ref[0] != topk_f) & (mid > lo) & (mid < hi)

        def step():
            lo = lohi_ref[0]
            hi = lohi_ref[1]
            mid = lo + (hi - lo) * 0.5
            active = (cnt_ref[0] != topk_f) & (mid > lo) & (mid < hi) & (done_ref[...] == 0.0)
            cnt = count(mid, False)
            up = active & (cnt >= topk_f)
            dn = active & (cnt < topk_f)
            lohi_ref[0] = jnp.where(up, mid, lo)
            lohi_ref[1] = jnp.where(dn, mid, hi)
            cnt_ref[0] = jnp.where(up, cnt, cnt_ref[0])
            cnt_ref[1] = jnp.where(dn, cnt, cnt_ref[1])

        def body(c):
            it, _ = c
            for _ in range(BIS_UNROLL):
                step()
            ties_only = count(lohi_ref[0], True) == cnt_ref[1]
            done = jnp.where(unresolved() & jnp.logical_not(ties_only), 0.0, 1.0)
            done_ref[...] = done
            return it + 1, jnp.min(done)

        done_ref[...] = jnp.zeros((tq, LANES), F32)
        lax.while_loop(lambda c: (c[1] < 0.5) & (c[0] < BIS_MAX_ROUNDS), body, (jnp.int32(0), jnp.float32(0.0)))

    search(count_cand)
    full_lo = count_keys(lohi_ref[0], False)
    full_hi = count_keys(lohi_ref[1], False)
    agree = jnp.logical_not(enough) | ((full_lo == cnt_ref[0]) & (full_hi == cnt_ref[1]))

    @pl.when(jnp.min(jnp.where(agree, 1.0, 0.0)) < 0.5)
    def _():
        search(count_keys)

    lo = lohi_ref[0]
    hi = lohi_ref[1]
    need = topk_f - cnt_ref[1]
    tie_any = jnp.max(jnp.where(cnt_ref[0] > topk_f, 1.0, 0.0))

    m_ref[...] = jnp.full(m_ref.shape, NEG, F32)
    acc_ref[...] = jnp.zeros(acc_ref.shape, F32)
    tiec_ref[...] = jnp.zeros(tiec_ref.shape, F32)
    q2d = q_ref[...].reshape(nh * tq, LANES)
    ones_col = jnp.where(_lane_iota((tk, LANES)) == 0, 1.0, 0.0).astype(BF16)

    def qk_dots(j):
        kt = k_ref[0, pl.ds(pl.multiple_of(j * tk, tk), tk), :]
        return lax.dot_general(q2d, kt, dn_t, preferred_element_type=F32)

    def softmax_tile(j, slot, tie):
        if tie:
            kk = keys_ref[j]
            cand = [(kk[:, c * LANES:(c + 1) * LANES] >= lo) & (kk[:, c * LANES:(c + 1) * LANES] < hi)
                    for c in range(ncol)]
            candf = jnp.concatenate([jnp.where(cd, 1.0, 0.0) for cd in cand], axis=1)
            r_i = lax.broadcasted_iota(I32, (tk, tk), 0)
            c_i = lax.broadcasted_iota(I32, (tk, tk), 1)
            upper = jnp.where(r_i < c_i, 1.0, 0.0).astype(BF16)
            pref = jnp.dot(candf.astype(BF16), upper, preferred_element_type=F32)
            base = tiec_ref[...]
            sel = []
            for c in range(ncol):
                kc = kk[:, c * LANES:(c + 1) * LANES]
                rank = base + pref[:, c * LANES:(c + 1) * LANES]
                sel.append(((kc >= hi) | (cand[c] & (rank < need))) & admissible(j, c))
            tiec_ref[...] = base + jnp.sum(candf, axis=1, keepdims=True)
        else:
            rb = min(tq, 32)
            lane_r = _lane_iota((rb, LANES))
            for r0 in range(0, tq, rb):
                rs = slice(r0, r0 + rb)
                lo_r = lohi_ref[0, rs]
                qc_r = lax.shift_right_logical(pos0 + r0 + lax.broadcasted_iota(I32, (rb, LANES), 0), CHUNK_SHIFT)
                klim_r = jnp.minimum(lax.shift_left(qc_r + 1, CHUNK_SHIFT), n_valid)
                sel = []
                for c in range(ncol):
                    adm = (j * tk + c * LANES + lane_r) < klim_r
                    sel.append((keys_ref[j, rs, c * LANES:(c + 1) * LANES] >= lo_r) & adm)
                for h in range(nh):
                    hr = slice(h * tq + r0, h * tq + r0 + rb)
                    m_prev = m_ref[h, rs]
                    xs = [jnp.where(sel[c], s_ref[slot, hr, c * LANES:(c + 1) * LANES], NEG) for c in range(ncol)]
                    m_cur = xs[0]
                    for c in range(1, ncol):
                        m_cur = jnp.maximum(m_cur, xs[c])
                    m_new = jnp.maximum(m_prev, jnp.max(m_cur, axis=1, keepdims=True))
                    alpha_ref[slot, h, rs] = jnp.exp2(m_prev - m_new)
                    for c in range(ncol):
                        p_ref[slot, hr, c * LANES:(c + 1) * LANES] = jnp.exp2(xs[c] - m_new).astype(BF16)
                    m_ref[h, rs] = m_new
            return
        for h in range(nh):
            m_prev = m_ref[h]
            xs = [jnp.where(sel[c], s_ref[slot, h * tq:(h + 1) * tq, c * LANES:(c + 1) * LANES], NEG)
                  for c in range(ncol)]
            m_cur = xs[0]
            for c in range(1, ncol):
                m_cur = jnp.maximum(m_cur, xs[c])
            m_new = jnp.maximum(m_prev, jnp.max(m_cur, axis=1, keepdims=True))
            alpha_ref[slot, h] = jnp.exp2(m_prev - m_new)
            for c in range(ncol):
                p_ref[slot, h * tq:(h + 1) * tq, c * LANES:(c + 1) * LANES] = (
                    jnp.exp2(xs[c] - m_new).astype(BF16))
            m_ref[h] = m_new

    def pv_tile(j, slot):
        vt = v_ref[0, pl.ds(pl.multiple_of(j * tk, tk), tk), :]
        pv = jnp.dot(p_ref[slot], jnp.concatenate([vt, ones_col], axis=1), preferred_element_type=F32)
        for h in range(nh):
            alpha = alpha_ref[slot, h]
            for half in range(2):
                hs = slice(half * HD_A, (half + 1) * HD_A)
                acc_ref[h, :, hs] = acc_ref[h, :, hs] * alpha + pv[h * tq:(h + 1) * tq, hs]

    def attend(tie):
        npair = jnp.minimum((k_lim + 2 * tk - 1) // (2 * tk), nkt_max // 2)
        s_ref[0] = qk_dots(0)
        p_ref[1] = jnp.zeros(p_ref.shape[1:], BF16)
        alpha_ref[1] = jnp.ones(alpha_ref.shape[1:], F32)

        def pair(jj, carry):
            a = 2 * jj
            s_ref[1] = qk_dots(a + 1)
            softmax_tile(a, 0, tie)
            pv_tile(jnp.maximum(a - 1, 0), 1)
            s_ref[0] = qk_dots(jnp.minimum(a + 2, 2 * npair - 2))
            softmax_tile(a + 1, 1, tie)
            pv_tile(a, 0)
            return carry

        lax.fori_loop(0, npair, pair, 0)
        pv_tile(2 * npair - 1, 1)

    @pl.when(tie_any == 0)
    def _():
        attend(False)

    @pl.when(tie_any != 0)
    def _():
        attend(True)

    for h in range(nh):
        den = acc_ref[h, :, HD_A:HD_A + 1]
        o_ref[:, h * HD_A:(h + 1) * HD_A] = (acc_ref[h, :, :HD_A] / den).astype(o_ref.dtype)


def _dsa(q_hm, qi_hm, wi, ki2, k, v, *, n_batch, tq, n_qt, q_off, nk, topk, pos_base, n_valid):
    tk = _pick_tile(nk, 512, LANES)
    nkt_max = nk // tk
    assert nkt_max % 4 == 0 and 4 * (tk // LANES) == CAND_DEPTH and nk >= topk, "key tiles are merged in fours"
    kern = functools.partial(_dsa_kernel, tq=tq, tk=tk, nkt_max=nkt_max, topk=topk,
                             pos_base=pos_base, n_valid=n_valid)
    qmap = lambda b, i: (0, q_off + b * n_qt + i, 0)
    rmap = lambda b, i: (q_off + b * n_qt + i, 0)
    kmap = lambda b, i: (b, 0, 0)
    in_specs = [pl.BlockSpec((N_HEADS_A, tq, LANES), qmap), pl.BlockSpec((N_HEADS_IDX, tq, LANES), qmap),
                pl.BlockSpec((tq, LANES), rmap),
                pl.BlockSpec((1, nk, LANES), kmap), pl.BlockSpec((1, nk, LANES), kmap),
                pl.BlockSpec((1, nk, LANES), kmap)]
    scratch = [
        pltpu.VMEM((nkt_max, tq, tk), F32),
        pltpu.VMEM((N_HEADS_IDX, tq, LANES), F32),
        pltpu.VMEM((2, tq, LANES), F32),
        pltpu.VMEM((2, tq, LANES), F32),
        pltpu.VMEM((N_HEADS_A, tq, LANES), F32),
        pltpu.VMEM((2, N_HEADS_A, tq, LANES), F32),
        pltpu.VMEM((N_HEADS_A, tq, 2 * HD_A), F32),
        pltpu.VMEM((4, N_HEADS_A * tq, tk), F32),
        pltpu.VMEM((2, N_HEADS_A * tq, tk), BF16),
        pltpu.VMEM((tq, LANES), F32),
        pltpu.VMEM((tq, CAND_DEPTH * LANES), F32),
        pltpu.VMEM((tq, LANES), F32),
    ]
    return pl.pallas_call(
        kern, grid=(n_batch, n_qt), in_specs=in_specs,
        out_specs=pl.BlockSpec((tq, WA_Q), lambda b, i: (b * n_qt + i, 0)),
        out_shape=jax.ShapeDtypeStruct((n_batch * n_qt * tq, WA_Q), BF16),
        scratch_shapes=scratch,
        compiler_params=_cparams(2), name="dsa")(q_hm, qi_hm, wi, ki2, k, v)


def _band_kernel(q_ref, kp_ref, ko_ref, vp_ref, vo_ref, rext_ref, o_ref, bias_ref,
                 *, tq, tqo, n_own, off, prev_always):
    b = pl.program_id(0)
    i = pl.program_id(1)
    w = BAND_BACK + tqo
    scale = HD_B ** -0.5

    @pl.when((b == 0) & (i == 0))
    def _():
        ri = lax.broadcasted_iota(I32, (tq, w), 0)
        ci = lax.broadcasted_iota(I32, (tq, w), 1)
        qc = lax.shift_right_logical(ri, CHUNK_SHIFT)
        jo = ci - BAND_BACK
        valid_prev = (ci < BAND_BACK) & (lax.shift_right_logical(ci, CHUNK_SHIFT) >= qc)
        valid_own = (jo >= 0) & (jo < n_own) & (lax.shift_right_logical(jnp.maximum(jo, 0), CHUNK_SHIFT) <= qc)
        valid = valid_prev | valid_own
        for h in range(N_HEADS_B):
            pat = jnp.broadcast_to(rext_ref[h:h + 1, :], (tq, off + w))
            rolled = pltpu.roll(pat, 0, 1, stride=1, stride_axis=0)
            bias_ref[h] = jnp.where(valid, rolled[:, off:off + w], NEG)

    dead_cols = 0 if prev_always else jnp.where(i > 0, 0, BAND_BACK)
    lane = _lane_iota((tq, LANES))
    lo_half = lane < HD_B
    prev_dead = lax.broadcasted_iota(I32, (tq, w), 1) < dead_cols
    for p in range(N_HEADS_B // 2):
        sl = slice(p * LANES, (p + 1) * LANES)
        kcat = jnp.concatenate([kp_ref[0, :, sl], ko_ref[0, :, sl]], axis=0)
        vcat = jnp.concatenate([vp_ref[0, :, sl], vo_ref[0, :, sl]], axis=0)
        outs = []
        for e in range(2):
            h = 2 * p + e
            s = lax.dot_general(q_ref[h], kcat, (((1,), (1,)), ((), ())), preferred_element_type=F32)
            s = s * scale + bias_ref[h]
            s = jnp.where(prev_dead, NEG, s)
            m = jnp.max(s, axis=1, keepdims=True)
            pexp = jnp.exp(s - m)
            den = jnp.sum(pexp, axis=1, keepdims=True)
            pv = jnp.dot(pexp.astype(BF16), vcat, preferred_element_type=F32)
            outs.append(pv / den)
        o_ref[:, sl] = jnp.where(lo_half, outs[0], outs[1]).astype(o_ref.dtype)


def _band(qb_hm, kprev, kown, vprev, vown, rel_bias, *, n_batch, tq, tqo, n_qt, q_off, n_own,
          prev_always, prev_map, own_map):
    off = max(tq, LANES)
    off = ((off + LANES - 1) // LANES) * LANES
    w = BAND_BACK + tqo
    u = np.arange(off + w)
    idx = np.clip(BAND_BACK + off - u, -MAX_REL, MAX_REL) + MAX_REL
    rext = rel_bias.astype(F32)[:, idx]
    kern = functools.partial(_band_kernel, tq=tq, tqo=tqo, n_own=n_own, off=off, prev_always=prev_always)
    qmap = lambda b, i: (0, q_off + b * n_qt + i, 0)
    in_specs = [pl.BlockSpec((N_HEADS_B, tq, LANES), qmap),
                pl.BlockSpec((1, BAND_BACK, WB), prev_map), pl.BlockSpec((1, tqo, WB), own_map),
                pl.BlockSpec((1, BAND_BACK, WB), prev_map), pl.BlockSpec((1, tqo, WB), own_map),
                pl.BlockSpec(rext.shape, lambda b, i: (0, 0))]
    return pl.pallas_call(
        kern, grid=(n_batch, n_qt), in_specs=in_specs,
        out_specs=pl.BlockSpec((tq, WB), lambda b, i: (b * n_qt + i, 0)),
        out_shape=jax.ShapeDtypeStruct((n_batch * n_qt * tq, WB), BF16),
        scratch_shapes=[pltpu.VMEM((N_HEADS_B, tq, w), F32)],
        compiler_params=_cparams(2), name="band")(qb_hm, kprev, kown, vprev, vown, rext)


def _merge_kernel(xp_ref, xs_ref, oap_ref, oas_ref, obp_ref, obs_ref, gmix_ref, wg_ref, bg_ref, wa_ref, wb_ref,
                  wo_ref, gffn_ref, wr_ref, br_ref, x1_ref, h2_ref, route_ref, wts_ref, cnt_ref,
                  carry_ref, oa_ref, ob_ref, x_ref, *, n_prompt_tiles):
    @pl.when(pl.program_id(0) == 0)
    def _():
        carry_ref[...] = jnp.zeros(carry_ref.shape, F32)

    @pl.when(pl.program_id(0) < n_prompt_tiles)
    def _():
        x_ref[...] = xp_ref[...]
        oa_ref[...] = oap_ref[...]
        ob_ref[...] = obp_ref[...]

    @pl.when(pl.program_id(0) >= n_prompt_tiles)
    def _():
        x_ref[...] = xs_ref[...]
        oa_ref[...] = oas_ref[...]
        ob_ref[...] = obs_ref[...]

    x = x_ref[...]
    d = x.shape[1]
    h = _rms(x, gmix_ref[...]).astype(BF16)
    gates = jax.nn.sigmoid(jnp.dot(h, wg_ref[...], preferred_element_type=F32) + bg_ref[...])

    ya = jnp.dot(oa_ref[...], wa_ref[...], preferred_element_type=F32)
    yb = jnp.dot(ob_ref[...], wb_ref[...], preferred_element_type=F32)
    m = gates[:, :d] * ya + gates[:, d:] * yb
    x1 = x + jnp.dot(m.astype(BF16), wo_ref[...], preferred_element_type=F32)
    x1_ref[...] = x1
    h2 = _rms(x1, gffn_ref[...]).astype(BF16)
    for c in range(d // LANES):
        h2_ref[:, c, :] = h2[:, c * LANES:(c + 1) * LANES]
    logits = jnp.dot(h2, wr_ref[...], preferred_element_type=F32) + br_ref[...]
    tm = logits.shape[0]
    lane = _lane_iota(logits.shape)
    logits = jnp.where(lane < N_EXPERTS, logits, -jnp.inf)
    wts = jnp.zeros(logits.shape, F32)
    route = jnp.zeros(logits.shape, I32)
    onehot = jnp.zeros(logits.shape, F32)
    den = jnp.zeros((tm, 1), F32)
    picks = []
    v0 = None
    for k in range(TOP_K):
        mx = jnp.max(logits, axis=1, keepdims=True)
        idx = jnp.min(jnp.where(logits == mx, lane, LANES), axis=1, keepdims=True)
        pick = lane == idx
        if v0 is None:
            v0 = mx
        e = jnp.exp(mx - v0)
        wts = jnp.where(lane == k, e, wts)
        route = jnp.where(lane == k, idx, route)
        onehot = jnp.where(pick, 1.0, onehot)
        picks.append(pick)
        den = den + e
        logits = jnp.where(pick, -jnp.inf, logits)
    wts_ref[...] = wts / den
    r_i = lax.broadcasted_iota(I32, (tm, tm), 0)
    c_i = lax.broadcasted_iota(I32, (tm, tm), 1)
    earlier = jnp.where(c_i < r_i, 1.0, 0.0).astype(BF16)
    cum = carry_ref[...] + jnp.dot(earlier, onehot.astype(BF16), preferred_element_type=F32)
    for k in range(TOP_K):
        rank = jnp.sum(jnp.where(picks[k], cum, 0.0), axis=1, keepdims=True).astype(I32)
        route = jnp.where(lane == TOP_K + k, rank, route)
    route_ref[...] = route
    total = carry_ref[...] + jnp.sum(onehot, axis=0, keepdims=True)
    carry_ref[...] = total
    cnt_ref[...] = jnp.broadcast_to(total, cnt_ref.shape)


def _merge(x_p, x_s, oa_p, oa_s, ob_p, ob_s, lw):
    d = x_p.shape[1]
    n_p, n_s = oa_p.shape[0], oa_s.shape[0]
    t_all = n_p + n_s
    tm = _pick_tile(int(np.gcd(n_p, n_s)), 256, 16)
    n_pt = n_p // tm
    w_in = lw['w_in']
    w_gate = w_in[:, w_in.shape[1] - 2 * d:].astype(BF16)
    wr = jnp.pad(lw['w_router'], ((0, 0), (0, LANES - N_EXPERTS))).astype(BF16)
    br = jnp.pad(lw['b_router'], (0, LANES - N_EXPERTS)).reshape(1, LANES).astype(F32)
    row = lambda g: g.reshape(1, -1).astype(F32)
    ins = [x_p, x_s, oa_p, oa_s, ob_p, ob_s, row(lw['g_mix']), w_gate, row(lw['b_gate']),
           lw['w_br_a'].astype(BF16), lw['w_br_b'].astype(BF16), lw['w_out'].astype(BF16), row(lw['g_ffn']), wr, br]
    tok = lambda w: pl.BlockSpec((tm, w), lambda i: (i, 0))
    ptok = lambda w: pl.BlockSpec((tm, w), lambda i: (jnp.minimum(i, n_pt - 1), 0))
    stok = lambda w: pl.BlockSpec((tm, w), lambda i: (jnp.maximum(i - n_pt, 0), 0))
    full = lambda a: pl.BlockSpec(a.shape, lambda i: (0,) * a.ndim)
    in_specs = [ptok(d), stok(d), ptok(WA_Q), stok(WA_Q), ptok(WB), stok(WB)] + [full(a) for a in ins[6:]]
    sds = jax.ShapeDtypeStruct
    slabs = d // LANES
    return pl.pallas_call(
        functools.partial(_merge_kernel, n_prompt_tiles=n_pt), grid=(t_all // tm,), in_specs=in_specs,
        out_specs=[tok(d), pl.BlockSpec((tm, slabs, LANES), lambda i: (i, 0, 0)), tok(LANES), tok(LANES),
                   pl.BlockSpec((8, LANES), lambda i: (0, 0))],
        out_shape=[sds((t_all, d), F32),
                   sds((t_all, slabs, LANES), BF16),
                   sds((t_all, LANES), I32),
                   sds((t_all, LANES), F32),
                   sds((8, LANES), F32)],
        scratch_shapes=[pltpu.VMEM((1, LANES), F32), pltpu.VMEM((tm, WA_Q), BF16), pltpu.VMEM((tm, WB), BF16),
                        pltpu.VMEM((tm, d), F32)],
        compiler_params=_cparams(1), name="merge")(*ins)


def _swiglu(u):
    glu = jnp.minimum(u[:, :D_FF], SWIGLU_LIMIT)
    lin = jnp.clip(u[:, D_FF:], -SWIGLU_LIMIT, SWIGLU_LIMIT)
    return glu * jax.nn.sigmoid(SWIGLU_ALPHA * glu) * (lin + 1.0)


MOE_ROWS = 512


def _route_plan(route, cnt, n_tiles):
    eid = route[:, :TOP_K]
    rank = route[:, TOP_K:2 * TOP_K]
    cnt_e = cnt[0, :N_EXPERTS].astype(I32)
    ntile = (cnt_e + MOE_ROWS - 1) // MOE_ROWS
    tile_end = jnp.cumsum(ntile)
    tile_start = tile_end - ntile
    row_start = tile_start * MOE_ROWS
    onehot = eid[:, :, None] == jnp.arange(N_EXPERTS, dtype=I32)[None, None, :]
    pos = jnp.sum(jnp.where(onehot, row_start[None, None, :], 0), axis=-1) + rank
    g = jnp.arange(n_tiles, dtype=I32)
    used = tile_end[-1]
    g_eff = jnp.minimum(g, used - 1)
    tile_e = jnp.minimum(jnp.sum(g_eff[:, None] >= tile_end[None, :], axis=1), N_EXPERTS - 1).astype(I32)
    rows = jnp.clip(cnt_e[tile_e] - (g - tile_start[tile_e]) * MOE_ROWS, 0, MOE_ROWS)
    rows = jnp.where(g < used, rows, 0).astype(I32)
    return pos.astype(I32), tile_e, rows


def _dispatch_kernel(pos_ref, h_ref, xs_in, xs_ref, sem, *, tm):
    del xs_in

    def issue(t, carry):
        for k in range(TOP_K):
            pltpu.make_async_copy(h_ref.at[t], xs_ref.at[pos_ref[0, 0, t * TOP_K + k]], sem).start()
        return carry

    lax.fori_loop(0, tm, issue, 0)
    for k in range(TOP_K):
        pltpu.make_async_copy(h_ref, xs_ref.at[pl.ds(0, tm)], sem).wait()


def _dispatch(h2, pos, n_rows):
    t_all, slabs, _ = h2.shape
    tm = _pick_tile(t_all, 256, 16)
    pos3 = pos.reshape(t_all // tm, 1, tm * TOP_K)
    xs0 = jnp.zeros((n_rows, slabs, LANES), h2.dtype)
    return pl.pallas_call(
        functools.partial(_dispatch_kernel, tm=tm), grid=(t_all // tm,),
        in_specs=[pl.BlockSpec((1, 1, tm * TOP_K), lambda i: (i, 0, 0), memory_space=pltpu.SMEM),
                  pl.BlockSpec((tm, slabs, LANES), lambda i: (i, 0, 0)),
                  pl.BlockSpec(memory_space=pl.ANY)],
        out_specs=pl.BlockSpec(memory_space=pl.ANY),
        out_shape=jax.ShapeDtypeStruct(xs0.shape, xs0.dtype),
        scratch_shapes=[pltpu.SemaphoreType.DMA(())],
        input_output_aliases={2: 0},
        compiler_params=_cparams(1), name="dispatch")(pos3, h2, xs0)


def _experts_kernel(te_ref, rows_ref, xs_ref, wu_ref, bu_ref, wd_ref, bd_ref, ys_ref, wub_ref, wdb_ref):
    g = pl.program_id(0)
    slabs = xs_ref.shape[1]

    @pl.when((g == 0) | (te_ref[g] != te_ref[jnp.maximum(g - 1, 0)]))
    def _():
        wub_ref[...] = wu_ref[0].astype(BF16)
        wdb_ref[...] = wd_ref[0].astype(BF16)

    @pl.when(rows_ref[g] > 0)
    def _():
        half = xs_ref.shape[0] // 2
        for r0 in (0, half):
            x = jnp.concatenate([xs_ref[r0:r0 + half, c, :] for c in range(slabs)], axis=1)
            u = jnp.dot(x, wub_ref[...], preferred_element_type=F32) + bu_ref[0]
            ys_ref[r0:r0 + half, :] = (
                jnp.dot(_swiglu(u).astype(BF16), wdb_ref[...], preferred_element_type=F32) + bd_ref[0])

    @pl.when(rows_ref[g] == 0)
    def _():
        ys_ref[...] = jnp.zeros(ys_ref.shape, F32)


def _experts(xs, tile_e, rows, lw):
    n_rows, slabs, _ = xs.shape
    d = slabs * LANES
    wu = lw['w_up'].astype(F32)
    wd = lw['w_down'].astype(F32)
    bu = lw['b_up'].reshape(N_EXPERTS, 1, 2 * D_FF).astype(F32)
    bd = lw['b_down'].reshape(N_EXPERTS, 1, d).astype(F32)
    tile = pl.BlockSpec((MOE_ROWS, slabs, LANES), lambda g, te, rw: (g, 0, 0))
    ex = lambda a: pl.BlockSpec((1,) + a.shape[1:], lambda g, te, rw: (te[g], 0, 0))
    grid_spec = pltpu.PrefetchScalarGridSpec(
        num_scalar_prefetch=2, grid=(n_rows // MOE_ROWS,),
        in_specs=[tile, ex(wu), ex(bu), ex(wd), ex(bd)],
        out_specs=pl.BlockSpec((MOE_ROWS, d), lambda g, te, rw: (g, 0)),
        scratch_shapes=[pltpu.VMEM(wu.shape[1:], BF16), pltpu.VMEM(wd.shape[1:], BF16)])
    return pl.pallas_call(
        _experts_kernel, grid_spec=grid_spec,
        out_shape=jax.ShapeDtypeStruct((n_rows, d), F32),
        compiler_params=_cparams(1), name="experts")(tile_e, rows, xs, wu, bu, wd, bd)


def _combine_kernel(pos_ref, posn_ref, x1_ref, wts_ref, ys_ref, yp_ref, ys_out_ref, buf_ref, sem,
                    *, tm, n_prompt_tiles):
    i = pl.program_id(0)
    slot = lax.rem(i, 2)

    def fetch(p_ref, sl):
        def issue(t, carry):
            for k in range(TOP_K):
                pltpu.make_async_copy(ys_ref.at[pl.ds(p_ref[0, 0, t * TOP_K + k], 1)],
                                      buf_ref.at[sl, k, pl.ds(t, 1)], sem.at[sl]).start()
            return carry

        lax.fori_loop(0, tm, issue, 0)

    @pl.when(i == 0)
    def _():
        fetch(pos_ref, 0)

    @pl.when(i + 1 < pl.num_programs(0))
    def _():
        fetch(posn_ref, 1 - slot)

    for k in range(TOP_K):
        pltpu.make_async_copy(ys_ref.at[pl.ds(0, tm)], buf_ref.at[slot, k], sem.at[slot]).wait()
    w = wts_ref[...]
    acc = x1_ref[...]
    for k in range(TOP_K):
        acc = acc + w[:, k:k + 1] * buf_ref[slot, k]

    @pl.when(pl.program_id(0) < n_prompt_tiles)
    def _():
        yp_ref[...] = acc

    @pl.when(pl.program_id(0) >= n_prompt_tiles)
    def _():
        ys_out_ref[...] = acc


def _combine(x1, wts, ys, pos, n_prompt):
    t_all, d = x1.shape
    n_s = t_all - n_prompt
    tm = _pick_tile(int(np.gcd(n_prompt, n_s)), 256, 16)
    n_pt = n_prompt // tm
    n_t = t_all // tm
    pos3 = pos.reshape(n_t, 1, tm * TOP_K)
    return pl.pallas_call(
        functools.partial(_combine_kernel, tm=tm, n_prompt_tiles=n_pt), grid=(n_t,),
        in_specs=[pl.BlockSpec((1, 1, tm * TOP_K), lambda i: (i, 0, 0), memory_space=pltpu.SMEM),
                  pl.BlockSpec((1, 1, tm * TOP_K), lambda i: (jnp.minimum(i + 1, n_t - 1), 0, 0),
                               memory_space=pltpu.SMEM),
                  pl.BlockSpec((tm, d), lambda i: (i, 0)), pl.BlockSpec((tm, LANES), lambda i: (i, 0)),
                  pl.BlockSpec(memory_space=pl.ANY)],
        out_specs=[pl.BlockSpec((tm, d), lambda i: (jnp.minimum(i, n_pt - 1), 0)),
                   pl.BlockSpec((tm, d), lambda i: (jnp.maximum(i - n_pt, 0), 0))],
        out_shape=[jax.ShapeDtypeStruct((n_prompt, d), F32), jax.ShapeDtypeStruct((n_s, d), F32)],
        scratch_shapes=[pltpu.VMEM((2, TOP_K, tm, d), F32), pltpu.SemaphoreType.DMA((2,))],
        compiler_params=_cparams(1), name="combine")(pos3, pos3, x1, wts, ys)


def _moe(x1, h2, route, wts, cnt, lw, n_prompt):
    t_all = x1.shape[0]
    n_tiles = (TOP_K * t_all) // MOE_ROWS + N_EXPERTS
    pos, tile_e, rows = _route_plan(route, cnt, n_tiles)
    xs = _dispatch(h2, pos, n_tiles * MOE_ROWS)
    ys = _experts(xs, tile_e, rows, lw)
    return _combine(x1, wts, ys, pos, n_prompt)


def _layer(xp, xs, a_k, a_v, a_kidx, b_k, b_v, lw):
    _, s, d = xp.shape
    bs, ts, _ = xs.shape
    p_len = a_k.shape[1]
    t_s = bs * ts
    x_p = xp.reshape(s, d)
    x_s = xs.reshape(t_s, d)

    (q_hm, qi_hm, wi, kaf, vaf, kif, kab, vab, ki2, qb_hm, kbf, vbf, kbb, vbb) = _proj(x_p, x_s, p_len, ts, lw)

    tq_p = _pick_tile(s, 128, CHUNK)
    oa_p = _dsa(q_hm, qi_hm, wi, ki2[None], kab[None], vab[None],
                n_batch=1, tq=tq_p, n_qt=s // tq_p, q_off=0, nk=s, topk=min(TOPK_MAX, s // 4),
                pos_base=0, n_valid=s)
    n_keys = p_len + ts
    nk_s = ((n_keys + 2047) // 2048) * 2048
    pad_s = nk_s - n_keys

    def with_new(cache_bf, new_rows):
        return jnp.concatenate([cache_bf, new_rows.reshape(bs, ts, LANES),
                                jnp.zeros((bs, pad_s, LANES), BF16)], axis=1)

    kidx_c = a_kidx.astype(BF16)
    k_s = with_new(a_k.reshape(bs, p_len, HD_A).astype(BF16), kab[s:])
    v_s = with_new(a_v.reshape(bs, p_len, HD_A).astype(BF16), vab[s:])
    ki2_s = with_new(jnp.concatenate([kidx_c, kidx_c], axis=-1), ki2[s:])
    oa_s = _dsa(q_hm, qi_hm, wi, ki2_s, k_s, v_s,
                n_batch=bs, tq=ts, n_qt=1, q_off=s // ts, nk=nk_s, topk=min(TOPK_MAX, n_keys // 4),
                pos_base=p_len, n_valid=n_keys)

    tq_b = BAND_BACK
    ob_p = _band(qb_hm, kbb[None], kbb[None], vbb[None], vbb[None], lw['rel_bias'],
                 n_batch=1, tq=tq_b, tqo=tq_b, n_qt=s // tq_b, q_off=0, n_own=tq_b, prev_always=False,
                 prev_map=lambda b, i: (0, jnp.maximum(i - 1, 0), 0), own_map=lambda b, i: (0, i, 0))
    own_pad = LANES - ts
    kown_s = jnp.pad(kbb[s:].reshape(bs, ts, WB), ((0, 0), (0, own_pad), (0, 0)))
    vown_s = jnp.pad(vbb[s:].reshape(bs, ts, WB), ((0, 0), (0, own_pad), (0, 0)))
    bk2 = b_k.reshape(bs, BAND_BACK, WB)
    bv2 = b_v.reshape(bs, BAND_BACK, WB)
    ob_s = _band(qb_hm, bk2.astype(BF16), kown_s, bv2.astype(BF16), vown_s, lw['rel_bias'],
                 n_batch=bs, tq=ts, tqo=LANES, n_qt=1, q_off=s // ts, n_own=ts, prev_always=True,
                 prev_map=lambda b, i: (b, 0, 0), own_map=lambda b, i: (b, 0, 0))

    x1, h2, route, wts, cnt = _merge(x_p, x_s, oa_p, oa_s, ob_p, ob_s, lw)
    y_p, y_s = _moe(x1, h2, route, wts, cnt, lw, s)

    keep = min(BAND_BACK, s)
    st_p = (kaf[:s].reshape(1, s, 1, HD_A), vaf[:s].reshape(1, s, 1, HD_A), kif[:s, :D_IDX].reshape(1, s, D_IDX),
            kbf[s - keep:s].reshape(1, keep, N_HEADS_B, HD_B), vbf[s - keep:s].reshape(1, keep, N_HEADS_B, HD_B))
    kb_new = kbf[s:].reshape(bs, ts, N_HEADS_B, HD_B)
    vb_new = vbf[s:].reshape(bs, ts, N_HEADS_B, HD_B)
    st_s = (kaf[s:].reshape(bs, ts, 1, HD_A), vaf[s:].reshape(bs, ts, 1, HD_A),
            kif[s:, :D_IDX].reshape(bs, ts, D_IDX),
            jnp.concatenate([b_k, kb_new], axis=1)[:, ts:], jnp.concatenate([b_v, vb_new], axis=1)[:, ts:])
    return y_p.reshape(1, s, d), y_s.reshape(bs, ts, d), st_p, st_s


def kernel(x_prompt, x_sample, cache_a_k, cache_a_v, cache_a_kidx, state_b_k, state_b_v,
           g_mix, w_in, b_gate, g_qa, g_ka, g_ki, g_qb, g_kb, rel_bias, w_br_a, w_br_b, w_out,
           g_ffn, w_router, b_router, w_up, b_up, w_down, b_down):
    assert x_prompt.shape[0] == 1, "prompt batch is folded into the token axis; one stream supported"
    depth = g_mix.shape[0]
    yp, ys = x_prompt, x_sample
    states_p, states_s = [], []
    for l in range(depth):
        lw = dict(g_mix=g_mix[l], w_in=w_in[l], b_gate=b_gate[l], g_qa=g_qa[l], g_ka=g_ka[l], g_ki=g_ki[l],
                  g_qb=g_qb[l], g_kb=g_kb[l], rel_bias=rel_bias[l], w_br_a=w_br_a[l], w_br_b=w_br_b[l],
                  w_out=w_out[l], g_ffn=g_ffn[l], w_router=w_router[l], b_router=b_router[l],
                  w_up=w_up[l], b_up=b_up[l], w_down=w_down[l], b_down=b_down[l])
        yp, ys, st_p, st_s = _layer(yp, ys, cache_a_k[l], cache_a_v[l], cache_a_kidx[l],
                                    state_b_k[l], state_b_v[l], lw)
        states_p.append(st_p)
        states_s.append(st_s)
    a_k_p, a_v_p, a_ki_p, b_k_p, b_v_p = [jnp.stack(t) for t in zip(*states_p)]
    a_k_s, a_v_s, a_ki_s, b_k_s, b_v_s = [jnp.stack(t) for t in zip(*states_s)]
    return (yp, ys, a_k_p, a_v_p, a_ki_p, b_k_p, b_v_p, a_k_s, a_v_s, a_ki_s, b_k_s, b_v_s)
```

```python
import functools

import numpy as np
import jax
import jax.numpy as jnp
from jax import lax
from jax.experimental import pallas as pl
from jax.experimental.pallas import tpu as pltpu

F32 = jnp.float32
BF16 = jnp.bfloat16
I32 = jnp.int32

CHUNK = 64
CHUNK_SHIFT = 6
N_HEADS_A = 8
HD_A = 128
N_HEADS_IDX = 8
D_IDX = 64
TOPK_MAX = 256
N_HEADS_B = 8
HD_B = 64
N_PREV_CHUNKS = 8
BAND_BACK = N_PREV_CHUNKS * CHUNK
MAX_REL = 128
N_EXPERTS = 32
TOP_K = 4
D_FF = 1024
SWIGLU_LIMIT = 7.0
SWIGLU_ALPHA = 1.702
ROPE_THETA = 10000.0
EPS = 1e-6
NEG = -1e30
IDX_SCALE = (D_IDX ** -0.5) * (N_HEADS_IDX ** -0.5)
LOG2E = 1.4426950408889634
QK_SCALE_LOG2E = (HD_A ** -0.5) * LOG2E

LANES = 128
VMEM_LIMIT_BYTES = 56 * 1024 * 1024

WA_Q = N_HEADS_A * HD_A
WI_Q = N_HEADS_IDX * D_IDX
WB = N_HEADS_B * HD_B


def _pick_tile(n, target, mult):
    best = None
    for t in range(mult, min(n, target) + 1, mult):
        if n % t == 0:
            best = t
    return best if best is not None else n


def _cparams(n_axes):
    return pltpu.CompilerParams(dimension_semantics=("arbitrary",) * n_axes,
                                vmem_limit_bytes=VMEM_LIMIT_BYTES)


def _lane_iota(shape):
    return lax.broadcasted_iota(I32, shape, len(shape) - 1)


def _rms(x, g):
    ms = jnp.mean(x * x, axis=-1, keepdims=True)
    return x * lax.rsqrt(ms + EPS) * g


_C_QA = 0
_C_KA = _C_QA + WA_Q
_C_VA = _C_KA + HD_A
_C_QI = _C_VA + HD_A
_C_KI = _C_QI + WI_Q
_C_WI = _C_KI + LANES
_C_QB = _C_WI + LANES
_C_KB = _C_QB + WB
_C_VB = _C_KB + WB
_C_END = _C_VB + WB


def _proj_kernel(xp_ref, xs_ref, gmix_ref, w_ref, gqa_ref, gka_ref, gki_ref, gqb_ref, gkb_ref,
                 bca_ref, bsa_ref, oca_ref, osa_ref, bci_ref, bsi_ref, oci_ref, osi_ref,
                 q_ref, qi_ref, wi_ref, kaf_ref, vaf_ref, kif_ref, kab_ref, vab_ref, ki2_ref,
                 qb_ref, kbf_ref, vbf_ref, kbb_ref, vbb_ref, x_ref, *, n_prompt_tiles):
    @pl.when(pl.program_id(0) < n_prompt_tiles)
    def _():
        x_ref[...] = xp_ref[...]

    @pl.when(pl.program_id(0) >= n_prompt_tiles)
    def _():
        x_ref[...] = xs_ref[...]

    x = x_ref[...]
    h = _rms(x, gmix_ref[...]).astype(BF16)

    def seg(a, b):
        return jnp.dot(h, w_ref[:, a:b], preferred_element_type=F32)

    lane = _lane_iota((x.shape[0], LANES))
    lo_half = lane < HD_B
    ca = bca_ref[0] * oca_ref[0] - bsa_ref[0] * osa_ref[0]
    sa = bsa_ref[0] * oca_ref[0] + bca_ref[0] * osa_ref[0]
    cosa = ca
    sina = jnp.where(lo_half, -sa, sa)
    ci = bci_ref[0] * oci_ref[0] - bsi_ref[0] * osi_ref[0]
    si = bsi_ref[0] * oci_ref[0] + bci_ref[0] * osi_ref[0]
    first_half = jnp.bitwise_and(lane, D_IDX - 1) < D_IDX // 2
    cosi = ci
    sinia = jnp.where(first_half, -si, 0.0)
    sinib = jnp.where(first_half, 0.0, si)

    def rope_a(n):
        return n * cosa + pltpu.roll(n, HD_A // 2, 1) * sina

    def rope_i(n):
        return n * cosi + pltpu.roll(n, LANES - D_IDX // 2, 1) * sinia + pltpu.roll(n, D_IDX // 2, 1) * sinib

    z = seg(_C_QA, _C_KA)
    gqa = gqa_ref[...]
    for hd in range(N_HEADS_A):
        zh = z[:, hd * HD_A:(hd + 1) * HD_A]
        q_ref[hd] = (rope_a(_rms(zh, gqa)) * QK_SCALE_LOG2E).astype(BF16)

    ka = rope_a(_rms(seg(_C_KA, _C_VA), gka_ref[...]))
    kaf_ref[...] = ka
    kab_ref[...] = ka.astype(BF16)
    va = seg(_C_VA, _C_QI)
    vaf_ref[...] = va
    vab_ref[...] = va.astype(BF16)

    zk = seg(_C_KI, _C_WI)
    ms = jnp.sum(zk * zk, axis=-1, keepdims=True) * (1.0 / D_IDX)
    ki = rope_i(zk * lax.rsqrt(ms + EPS) * gki_ref[...])
    kif_ref[...] = ki
    ki2_ref[...] = (ki + pltpu.roll(ki, D_IDX, 1)).astype(BF16)

    z = seg(_C_QI, _C_KI)
    for p in range(N_HEADS_IDX // 2):
        r = rope_i(z[:, p * LANES:(p + 1) * LANES])
        qi_ref[2 * p] = jnp.where(lo_half, r, 0.0).astype(BF16)
        qi_ref[2 * p + 1] = jnp.where(lo_half, 0.0, r).astype(BF16)

    wi_ref[...] = seg(_C_WI, _C_QB) * IDX_SCALE

    def norm_b(zb, g):
        sq = zb * zb
        s_all = jnp.sum(sq, axis=-1, keepdims=True)
        s_lo = jnp.sum(jnp.where(lo_half, sq, 0.0), axis=-1, keepdims=True)
        r_lo = lax.rsqrt(s_lo * (1.0 / HD_B) + EPS)
        r_hi = lax.rsqrt((s_all - s_lo) * (1.0 / HD_B) + EPS)
        return zb * jnp.where(lo_half, r_lo, r_hi) * g

    z = seg(_C_QB, _C_KB)
    gqb = gqb_ref[...]
    for p in range(N_HEADS_B // 2):
        n = norm_b(z[:, p * LANES:(p + 1) * LANES], gqb)
        qb_ref[2 * p] = jnp.where(lo_half, n, 0.0).astype(BF16)
        qb_ref[2 * p + 1] = jnp.where(lo_half, 0.0, n).astype(BF16)
    z = seg(_C_KB, _C_VB)
    gkb = gkb_ref[...]
    for p in range(N_HEADS_B // 2):
        n = norm_b(z[:, p * LANES:(p + 1) * LANES], gkb)
        kbf_ref[:, p * LANES:(p + 1) * LANES] = n
        kbb_ref[:, p * LANES:(p + 1) * LANES] = n.astype(BF16)
    z = seg(_C_VB, _C_END)
    vbf_ref[...] = z
    vbb_ref[...] = z.astype(BF16)


def _proj(x_p, x_s, p_len, ts, lw):
    n_p, d = x_p.shape
    n_s = x_s.shape[0]
    t_all = n_p + n_s
    tm = _pick_tile(int(np.gcd(n_p, n_s)), 256, 16)
    assert tm % ts == 0, "a sample tile holds whole streams"
    n_pt, n_t = n_p // tm, t_all // tm
    w_in = lw['w_in']
    offs = np.cumsum((WA_Q, HD_A, HD_A, WI_Q, D_IDX, N_HEADS_IDX, WB, WB, WB))
    qa_w, ka_w, va_w, qi_w, ki_w, wi_w, qb_w, kb_w, vb_w = [
        w_in[:, a:b] for a, b in zip(np.concatenate([[0], offs[:-1]]), offs)]

    def padl(w):
        return jnp.pad(w, ((0, 0), (0, LANES - w.shape[1])))

    w_pack = jnp.concatenate([qa_w, ka_w, va_w, qi_w, padl(ki_w), padl(wi_w), qb_w, kb_w, vb_w],
                             axis=1).astype(BF16)

    base = np.concatenate([np.arange(n_pt) * tm, np.full(n_t - n_pt, p_len)]).astype(np.float32)
    offs_rows = np.stack([np.arange(tm), np.arange(tm) % ts]).astype(np.float32)

    def tables(dh):
        inv = ROPE_THETA ** (-jnp.arange(0, dh, 2, dtype=F32) / dh)
        inv = jnp.tile(inv, LANES // inv.shape[0])
        ang_b = jnp.asarray(base)[:, None, None] * inv[None, None, :]
        ang_o = jnp.asarray(offs_rows)[:, :, None] * inv[None, None, :]
        return jnp.cos(ang_b), jnp.sin(ang_b), jnp.cos(ang_o), jnp.sin(ang_o)

    rope_tabs = list(tables(HD_A)) + list(tables(D_IDX))

    row = lambda g: g.reshape(1, -1).astype(F32)
    gki = jnp.pad(lw['g_ki'], (0, LANES - D_IDX)).reshape(1, LANES)
    gqb = jnp.tile(lw['g_qb'], 2).reshape(1, LANES)
    gkb = jnp.tile(lw['g_kb'], 2).reshape(1, LANES)

    tok = lambda w: pl.BlockSpec((tm, w), lambda i: (i, 0))
    full = lambda a: pl.BlockSpec(a.shape, lambda i: (0,) * a.ndim)
    hm = pl.BlockSpec((N_HEADS_A, tm, LANES), lambda i: (0, i, 0))
    base_spec = pl.BlockSpec((1, 1, LANES), lambda i: (i, 0, 0))
    offs_spec = pl.BlockSpec((1, tm, LANES), lambda i: (jnp.where(i < n_pt, 0, 1), 0, 0))

    ins = [x_p, x_s, row(lw['g_mix']), w_pack, row(lw['g_qa']), row(lw['g_ka']), gki, gqb, gkb] + rope_tabs
    in_specs = [pl.BlockSpec((tm, d), lambda i: (jnp.minimum(i, n_pt - 1), 0)),
                pl.BlockSpec((tm, d), lambda i: (jnp.maximum(i - n_pt, 0), 0)),
                full(ins[2]), full(w_pack), full(ins[4]), full(ins[5]), full(gki), full(gqb), full(gkb),
                base_spec, base_spec, offs_spec, offs_spec, base_spec, base_spec, offs_spec, offs_spec]
    sds = jax.ShapeDtypeStruct
    out_shape = [
        sds((N_HEADS_A, t_all, LANES), BF16),
        sds((N_HEADS_IDX, t_all, LANES), BF16),
        sds((t_all, LANES), F32),
        sds((t_all, HD_A), F32), sds((t_all, HD_A), F32), sds((t_all, LANES), F32),
        sds((t_all, HD_A), BF16), sds((t_all, HD_A), BF16), sds((t_all, LANES), BF16),
        sds((N_HEADS_B, t_all, LANES), BF16),
        sds((t_all, WB), F32), sds((t_all, WB), F32), sds((t_all, WB), BF16), sds((t_all, WB), BF16),
    ]
    out_specs = [hm, hm, tok(LANES), tok(HD_A), tok(HD_A), tok(LANES), tok(HD_A), tok(HD_A), tok(LANES),
                 hm, tok(WB), tok(WB), tok(WB), tok(WB)]
    return pl.pallas_call(
        functools.partial(_proj_kernel, n_prompt_tiles=n_pt), grid=(n_t,), in_specs=in_specs,
        out_specs=out_specs, out_shape=out_shape, scratch_shapes=[pltpu.VMEM((tm, d), F32)],
        compiler_params=_cparams(1), name="proj")(*ins)


BIS_UNROLL = 4
BIS_MAX_ROUNDS = 80
BIG = 1e38
CAND_DEPTH = 16


def _batcher_pairs(n):
    pairs = []
    p = 1
    while p < n:
        k = p
        while k >= 1:
            for j in range(k % p, n - k, 2 * k):
                for i in range(min(k, n - j - k)):
                    if (i + j) // (2 * p) == (i + j + k) // (2 * p):
                        pairs.append((i + j, i + j + k))
            k //= 2
        p *= 2
    return pairs


def _bitonic_pairs(n):
    pairs = []
    stride = n // 2
    while stride >= 1:
        pairs += [(i, i + stride) for i in range(n) if not i & stride]
        stride //= 2
    return pairs


_SORT16 = _batcher_pairs(CAND_DEPTH)
_BITONIC16 = _bitonic_pairs(CAND_DEPTH)


def _dsa_kernel(q_ref, qi_ref, wi_ref, ki2_ref, k_ref, v_ref, o_ref,
                keys_ref, wb_ref, lohi_ref, cnt_ref, m_ref, alpha_ref, acc_ref, s_ref, p_ref, tiec_ref,
                cand_ref, done_ref, *, tq, tk, nkt_max, topk, pos_base, n_valid):
    i = pl.program_id(1)
    pos0 = pos_base + i * tq
    k_end = ((pos0 + tq - 1) // CHUNK + 1) * CHUNK
    k_lim = jnp.minimum(k_end, n_valid)
    nkt4 = jnp.minimum(((k_lim + 4 * tk - 1) // (4 * tk)) * 4, nkt_max)
    ncol = tk // LANES
    nh = N_HEADS_A
    topk_f = float(topk)

    qrow = pos0 + lax.broadcasted_iota(I32, (tq, LANES), 0)
    qchunk = lax.shift_right_logical(qrow, CHUNK_SHIFT)
    lane = _lane_iota((tq, LANES))

    klim = jnp.minimum(lax.shift_left(qchunk + 1, CHUNK_SHIFT), n_valid)

    def admissible(j, c):
        return (j * tk + c * LANES + lane) < klim

    w = wi_ref[...]
    for h in range(N_HEADS_IDX):
        wb_ref[h] = jnp.broadcast_to(w[:, h:h + 1], (tq, LANES))
    qi2d = qi_ref[...].reshape(N_HEADS_IDX * tq, LANES)

    dn_t = (((1,), (1,)), ((), ()))

    def idx_dots(j):
        kt = ki2_ref[0, pl.ds(pl.multiple_of(j * tk, tk), tk), :]
        return lax.dot_general(qi2d, kt, dn_t, preferred_element_type=F32)

    def score_tile(j, slot):
        for c in range(ncol):
            tot = None
            for h in range(N_HEADS_IDX):
                r = jnp.maximum(s_ref[slot, h * tq:(h + 1) * tq, c * LANES:(c + 1) * LANES], 0.0)
                term = wb_ref[h] * r
                tot = term if tot is None else tot + term
            keys_ref[j, :, c * LANES:(c + 1) * LANES] = jnp.where(admissible(j, c), tot, NEG)

    s_ref[0] = idx_dots(0)
    s_ref[1] = idx_dots(1)

    def score_quad(it, carry):
        t0 = 4 * it
        s_ref[2] = idx_dots(t0 + 2)
        s_ref[3] = idx_dots(t0 + 3)
        score_tile(t0, 0)
        score_tile(t0 + 1, 1)
        s_ref[0] = idx_dots(jnp.minimum(t0 + 4, nkt4 - 2))
        s_ref[1] = idx_dots(jnp.minimum(t0 + 5, nkt4 - 1))
        score_tile(t0 + 2, 2)
        score_tile(t0 + 3, 3)
        return carry

    lax.fori_loop(0, nkt4 // 4, score_quad, 0)

    enough = klim.astype(F32) >= topk_f

    def build_candidates(qd, carry):
        for g in range(tq // 8):
            rows = slice(8 * g, 8 * g + 8)
            new = [keys_ref[4 * qd + t, rows, c * LANES:(c + 1) * LANES] for t in range(4) for c in range(ncol)]
            for a, b in _SORT16:
                new[a], new[b] = jnp.maximum(new[a], new[b]), jnp.minimum(new[a], new[b])
            top = [jnp.maximum(cand_ref[rows, b * LANES:(b + 1) * LANES], new[CAND_DEPTH - 1 - b])
                   for b in range(CAND_DEPTH)]
            for a, b in _BITONIC16:
                top[a], top[b] = jnp.maximum(top[a], top[b]), jnp.minimum(top[a], top[b])
            for b in range(CAND_DEPTH):
                cand_ref[rows, b * LANES:(b + 1) * LANES] = top[b]
        return carry

    cand_ref[...] = jnp.full(cand_ref.shape, -BIG, F32)
    lax.fori_loop(0, nkt4 // 4, build_candidates, 0)

    def count_cand(thr, strict):
        acc = jnp.zeros((tq, LANES), F32)
        for b in range(CAND_DEPTH):
            x = cand_ref[:, b * LANES:(b + 1) * LANES]
            acc = acc + jnp.where((x > thr) if strict else (x >= thr), 1.0, 0.0)
        return jnp.sum(acc, axis=1, keepdims=True)

    def count_keys(thr, strict):
        def tile(j, acc):
            for cc in range(ncol):
                x = keys_ref[j, :, cc * LANES:(cc + 1) * LANES]
                acc = acc + jnp.where((x > thr) if strict else (x >= thr), 1.0, 0.0)
            return acc
        return jnp.sum(lax.fori_loop(0, nkt4, tile, jnp.zeros((tq, LANES), F32)), axis=1, keepdims=True)

    zeros = jnp.zeros((tq, LANES), F32)
    rmax = jnp.max(cand_ref[:, 0:LANES], axis=1, keepdims=True)
    hi0 = jnp.where(enough, rmax + jnp.maximum(jnp.abs(rmax), 1e-30) * 1e-6, BIG) + zeros
    head = None
    for b in range(-(-topk // LANES)):
        x = cand_ref[:, b * LANES:(b + 1) * LANES]
        x = jnp.where(x > 0.5 * NEG, x, BIG)
        head = x if head is None else jnp.minimum(head, x)
    lo_try = jnp.min(head, axis=1, keepdims=True) + zeros

    def search(count):
        lo0 = jnp.where(enough & (count(lo_try, False) >= topk_f), lo_try, -BIG)
        lohi_ref[0] = lo0
        lohi_ref[1] = hi0
        cnt_ref[0] = jnp.where(enough, count(lo0, False), topk_f) + zeros
        cnt_ref[1] = zeros

        def unresolved():
            lo = lohi_ref[0]
            hi = lohi_ref[1]
            mid = lo + (hi - lo) * 0.5
            return (cnt_ref[0] != topk_f) & (mid > lo) & (mid < hi)

        def step():
            lo = lohi_ref[0]
            hi = lohi_ref[1]
            mid = lo + (hi - lo) * 0.5
            active = (cnt_ref[0] != topk_f) & (mid > lo) & (mid < hi) & (done_ref[...] == 0.0)
            cnt = count(mid, False)
            up = active & (cnt >= topk_f)
            dn = active & (cnt < topk_f)
            lohi_ref[0] = jnp.where(up, mid, lo)
            lohi_ref[1] = jnp.where(dn, mid, hi)
            cnt_ref[0] = jnp.where(up, cnt, cnt_ref[0])
            cnt_ref[1] = jnp.where(dn, cnt, cnt_ref[1])

        def body(c):
            it, _ = c
            for _ in range(BIS_UNROLL):
                step()
            ties_only = count(lohi_ref[0], True) == cnt_ref[1]
            done = jnp.where(unresolved() & jnp.logical_not(ties_only), 0.0, 1.0)
            done_ref[...] = done
            return it + 1, jnp.min(done)

        done_ref[...] = jnp.zeros((tq, LANES), F32)
        lax.while_loop(lambda c: (c[1] < 0.5) & (c[0] < BIS_MAX_ROUNDS), body, (jnp.int32(0), jnp.float32(0.0)))

    search(count_cand)
    full_lo = count_keys(lohi_ref[0], False)
    full_hi = count_keys(lohi_ref[1], False)
    agree = jnp.logical_not(enough) | ((full_lo == cnt_ref[0]) & (full_hi == cnt_ref[1]))

    @pl.when(jnp.min(jnp.where(agree, 1.0, 0.0)) < 0.5)
    def _():
        search(count_keys)

    lo = lohi_ref[0]
    hi = lohi_ref[1]
    need = topk_f - cnt_ref[1]
    tie_any = jnp.max(jnp.where(cnt_ref[0] > topk_f, 1.0, 0.0))

    m_ref[...] = jnp.full(m_ref.shape, NEG, F32)
    acc_ref[...] = jnp.zeros(acc_ref.shape, F32)
    tiec_ref[...] = jnp.zeros(tiec_ref.shape, F32)
    q2d = q_ref[...].reshape(nh * tq, LANES)
    ones_col = jnp.where(_lane_iota((tk, LANES)) == 0, 1.0, 0.0).astype(BF16)

    def qk_dots(j):
        kt = k_ref[0, pl.ds(pl.multiple_of(j * tk, tk), tk), :]
        return lax.dot_general(q2d, kt, dn_t, preferred_element_type=F32)

    def softmax_tile(j, slot, tie):
        if tie:
            kk = keys_ref[j]
            cand = [(kk[:, c * LANES:(c + 1) * LANES] >= lo) & (kk[:, c * LANES:(c + 1) * LANES] < hi)
                    for c in range(ncol)]
            candf = jnp.concatenate([jnp.where(cd, 1.0, 0.0) for cd in cand], axis=1)
            r_i = lax.broadcasted_iota(I32, (tk, tk), 0)
            c_i = lax.broadcasted_iota(I32, (tk, tk), 1)
            upper = jnp.where(r_i < c_i, 1.0, 0.0).astype(BF16)
            pref = jnp.dot(candf.astype(BF16), upper, preferred_element_type=F32)
            base = tiec_ref[...]
            sel = []
            for c in range(ncol):
                kc = kk[:, c * LANES:(c + 1) * LANES]
                rank = base + pref[:, c * LANES:(c + 1) * LANES]
                sel.append(((kc >= hi) | (cand[c] & (rank < need))) & admissible(j, c))
            tiec_ref[...] = base + jnp.sum(candf, axis=1, keepdims=True)
        else:
            rb = min(tq, 32)
            lane_r = _lane_iota((rb, LANES))
            for r0 in range(0, tq, rb):
                rs = slice(r0, r0 + rb)
                lo_r = lohi_ref[0, rs]
                qc_r = lax.shift_right_logical(pos0 + r0 + lax.broadcasted_iota(I32, (rb, LANES), 0), CHUNK_SHIFT)
                klim_r = jnp.minimum(lax.shift_left(qc_r + 1, CHUNK_SHIFT), n_valid)
                sel = []
                for c in range(ncol):
                    adm = (j * tk + c * LANES + lane_r) < klim_r
                    sel.append((keys_ref[j, rs, c * LANES:(c + 1) * LANES] >= lo_r) & adm)
                for h in range(nh):
                    hr = slice(h * tq + r0, h * tq + r0 + rb)
                    m_prev = m_ref[h, rs]
                    xs = [jnp.where(sel[c], s_ref[slot, hr, c * LANES:(c + 1) * LANES], NEG) for c in range(ncol)]
                    m_cur = xs[0]
                    for c in range(1, ncol):
                        m_cur = jnp.maximum(m_cur, xs[c])
                    m_new = jnp.maximum(m_prev, jnp.max(m_cur, axis=1, keepdims=True))
                    alpha_ref[slot, h, rs] = jnp.exp2(m_prev - m_new)
                    for c in range(ncol):
                        p_ref[slot, hr, c * LANES:(c + 1) * LANES] = jnp.exp2(xs[c] - m_new).astype(BF16)
                    m_ref[h, rs] = m_new
            return
        for h in range(nh):
            m_prev = m_ref[h]
            xs = [jnp.where(sel[c], s_ref[slot, h * tq:(h + 1) * tq, c * LANES:(c + 1) * LANES], NEG)
                  for c in range(ncol)]
            m_cur = xs[0]
            for c in range(1, ncol):
                m_cur = jnp.maximum(m_cur, xs[c])
            m_new = jnp.maximum(m_prev, jnp.max(m_cur, axis=1, keepdims=True))
            alpha_ref[slot, h] = jnp.exp2(m_prev - m_new)
            for c in range(ncol):
                p_ref[slot, h * tq:(h + 1) * tq, c * LANES:(c + 1) * LANES] = (
                    jnp.exp2(xs[c] - m_new).astype(BF16))
            m_ref[h] = m_new

    def pv_tile(j, slot):
        vt = v_ref[0, pl.ds(pl.multiple_of(j * tk, tk), tk), :]
        pv = jnp.dot(p_ref[slot], jnp.concatenate([vt, ones_col], axis=1), preferred_element_type=F32)
        for h in range(nh):
            alpha = alpha_ref[slot, h]
            for half in range(2):
                hs = slice(half * HD_A, (half + 1) * HD_A)
                acc_ref[h, :, hs] = acc_ref[h, :, hs] * alpha + pv[h * tq:(h + 1) * tq, hs]

    def attend(tie):
        npair = jnp.minimum((k_lim + 2 * tk - 1) // (2 * tk), nkt_max // 2)
        s_ref[0] = qk_dots(0)
        p_ref[1] = jnp.zeros(p_ref.shape[1:], BF16)
        alpha_ref[1] = jnp.ones(alpha_ref.shape[1:], F32)

        def pair(jj, carry):
            a = 2 * jj
            s_ref[1] = qk_dots(a + 1)
            softmax_tile(a, 0, tie)
            pv_tile(jnp.maximum(a - 1, 0), 1)
            s_ref[0] = qk_dots(jnp.minimum(a + 2, 2 * npair - 2))
            softmax_tile(a + 1, 1, tie)
            pv_tile(a, 0)
            return carry

        lax.fori_loop(0, npair, pair, 0)
        pv_tile(2 * npair - 1, 1)

    @pl.when(tie_any == 0)
    def _():
        attend(False)

    @pl.when(tie_any != 0)
    def _():
        attend(True)

    for h in range(nh):
        den = acc_ref[h, :, HD_A:HD_A + 1]
        o_ref[:, h * HD_A:(h + 1) * HD_A] = (acc_ref[h, :, :HD_A] / den).astype(o_ref.dtype)


def _dsa(q_hm, qi_hm, wi, ki2, k, v, *, n_batch, tq, n_qt, q_off, nk, topk, pos_base, n_valid):
    tk = _pick_tile(nk, 512, LANES)
    nkt_max = nk // tk
    assert nkt_max % 4 == 0 and 4 * (tk // LANES) == CAND_DEPTH and nk >= topk, "key tiles are merged in fours"
    kern = functools.partial(_dsa_kernel, tq=tq, tk=tk, nkt_max=nkt_max, topk=topk,
                             pos_base=pos_base, n_valid=n_valid)
    qmap = lambda b, i: (0, q_off + b * n_qt + i, 0)
    rmap = lambda b, i: (q_off + b * n_qt + i, 0)
    kmap = lambda b, i: (b, 0, 0)
    in_specs = [pl.BlockSpec((N_HEADS_A, tq, LANES), qmap), pl.BlockSpec((N_HEADS_IDX, tq, LANES), qmap),
                pl.BlockSpec((tq, LANES), rmap),
                pl.BlockSpec((1, nk, LANES), kmap), pl.BlockSpec((1, nk, LANES), kmap),
                pl.BlockSpec((1, nk, LANES), kmap)]
    scratch = [
        pltpu.VMEM((nkt_max, tq, tk), F32),
        pltpu.VMEM((N_HEADS_IDX, tq, LANES), F32),
        pltpu.VMEM((2, tq, LANES), F32),
        pltpu.VMEM((2, tq, LANES), F32),
        pltpu.VMEM((N_HEADS_A, tq, LANES), F32),
        pltpu.VMEM((2, N_HEADS_A, tq, LANES), F32),
        pltpu.VMEM((N_HEADS_A, tq, 2 * HD_A), F32),
        pltpu.VMEM((4, N_HEADS_A * tq, tk), F32),
        pltpu.VMEM((2, N_HEADS_A * tq, tk), BF16),
        pltpu.VMEM((tq, LANES), F32),
        pltpu.VMEM((tq, CAND_DEPTH * LANES), F32),
        pltpu.VMEM((tq, LANES), F32),
    ]
    return pl.pallas_call(
        kern, grid=(n_batch, n_qt), in_specs=in_specs,
        out_specs=pl.BlockSpec((tq, WA_Q), lambda b, i: (b * n_qt + i, 0)),
        out_shape=jax.ShapeDtypeStruct((n_batch * n_qt * tq, WA_Q), BF16),
        scratch_shapes=scratch,
        compiler_params=_cparams(2), name="dsa")(q_hm, qi_hm, wi, ki2, k, v)


def _band_kernel(q_ref, kp_ref, ko_ref, vp_ref, vo_ref, rext_ref, o_ref, bias_ref,
                 *, tq, tqo, n_own, off, prev_always):
    b = pl.program_id(0)
    i = pl.program_id(1)
    w = BAND_BACK + tqo
    scale = HD_B ** -0.5

    @pl.when((b == 0) & (i == 0))
    def _():
        ri = lax.broadcasted_iota(I32, (tq, w), 0)
        ci = lax.broadcasted_iota(I32, (tq, w), 1)
        qc = lax.shift_right_logical(ri, CHUNK_SHIFT)
        jo = ci - BAND_BACK
        valid_prev = (ci < BAND_BACK) & (lax.shift_right_logical(ci, CHUNK_SHIFT) >= qc)
        valid_own = (jo >= 0) & (jo < n_own) & (lax.shift_right_logical(jnp.maximum(jo, 0), CHUNK_SHIFT) <= qc)
        valid = valid_prev | valid_own
        for h in range(N_HEADS_B):
            pat = jnp.broadcast_to(rext_ref[h:h + 1, :], (tq, off + w))
            rolled = pltpu.roll(pat, 0, 1, stride=1, stride_axis=0)
            bias_ref[h] = jnp.where(valid, rolled[:, off:off + w], NEG)

    dead_cols = 0 if prev_always else jnp.where(i > 0, 0, BAND_BACK)
    lane = _lane_iota((tq, LANES))
    lo_half = lane < HD_B
    prev_dead = lax.broadcasted_iota(I32, (tq, w), 1) < dead_cols
    for p in range(N_HEADS_B // 2):
        sl = slice(p * LANES, (p + 1) * LANES)
        kcat = jnp.concatenate([kp_ref[0, :, sl], ko_ref[0, :, sl]], axis=0)
        vcat = jnp.concatenate([vp_ref[0, :, sl], vo_ref[0, :, sl]], axis=0)
        outs = []
        for e in range(2):
            h = 2 * p + e
            s = lax.dot_general(q_ref[h], kcat, (((1,), (1,)), ((), ())), preferred_element_type=F32)
            s = s * scale + bias_ref[h]
            s = jnp.where(prev_dead, NEG, s)
            m = jnp.max(s, axis=1, keepdims=True)
            pexp = jnp.exp(s - m)
            den = jnp.sum(pexp, axis=1, keepdims=True)
            pv = jnp.dot(pexp.astype(BF16), vcat, preferred_element_type=F32)
            outs.append(pv / den)
        o_ref[:, sl] = jnp.where(lo_half, outs[0], outs[1]).astype(o_ref.dtype)


def _band(qb_hm, kprev, kown, vprev, vown, rel_bias, *, n_batch, tq, tqo, n_qt, q_off, n_own,
          prev_always, prev_map, own_map):
    off = max(tq, LANES)
    off = ((off + LANES - 1) // LANES) * LANES
    w = BAND_BACK + tqo
    u = np.arange(off + w)
    idx = np.clip(BAND_BACK + off - u, -MAX_REL, MAX_REL) + MAX_REL
    rext = rel_bias.astype(F32)[:, idx]
    kern = functools.partial(_band_kernel, tq=tq, tqo=tqo, n_own=n_own, off=off, prev_always=prev_always)
    qmap = lambda b, i: (0, q_off + b * n_qt + i, 0)
    in_specs = [pl.BlockSpec((N_HEADS_B, tq, LANES), qmap),
                pl.BlockSpec((1, BAND_BACK, WB), prev_map), pl.BlockSpec((1, tqo, WB), own_map),
                pl.BlockSpec((1, BAND_BACK, WB), prev_map), pl.BlockSpec((1, tqo, WB), own_map),
                pl.BlockSpec(rext.shape, lambda b, i: (0, 0))]
    return pl.pallas_call(
        kern, grid=(n_batch, n_qt), in_specs=in_specs,
        out_specs=pl.BlockSpec((tq, WB), lambda b, i: (b * n_qt + i, 0)),
        out_shape=jax.ShapeDtypeStruct((n_batch * n_qt * tq, WB), BF16),
        scratch_shapes=[pltpu.VMEM((N_HEADS_B, tq, w), F32)],
        compiler_params=_cparams(2), name="band")(qb_hm, kprev, kown, vprev, vown, rext)


def _merge_kernel(xp_ref, xs_ref, oap_ref, oas_ref, obp_ref, obs_ref, gmix_ref, wg_ref, bg_ref, wa_ref, wb_ref,
                  wo_ref, gffn_ref, wr_ref, br_ref, x1_ref, h2_ref, route_ref, wts_ref, cnt_ref,
                  carry_ref, oa_ref, ob_ref, x_ref, *, n_prompt_tiles):
    @pl.when(pl.program_id(0) == 0)
    def _():
        carry_ref[...] = jnp.zeros(carry_ref.shape, F32)

    @pl.when(pl.program_id(0) < n_prompt_tiles)
    def _():
        x_ref[...] = xp_ref[...]
        oa_ref[...] = oap_ref[...]
        ob_ref[...] = obp_ref[...]

    @pl.when(pl.program_id(0) >= n_prompt_tiles)
    def _():
        x_ref[...] = xs_ref[...]
        oa_ref[...] = oas_ref[...]
        ob_ref[...] = obs_ref[...]

    x = x_ref[...]
    d = x.shape[1]
    h = _rms(x, gmix_ref[...]).astype(BF16)
    gates = jax.nn.sigmoid(jnp.dot(h, wg_ref[...], preferred_element_type=F32) + bg_ref[...])

    ya = jnp.dot(oa_ref[...], wa_ref[...], preferred_element_type=F32)
    yb = jnp.dot(ob_ref[...], wb_ref[...], preferred_element_type=F32)
    m = gates[:, :d] * ya + gates[:, d:] * yb
    x1 = x + jnp.dot(m.astype(BF16), wo_ref[...], preferred_element_type=F32)
    x1_ref[...] = x1
    h2 = _rms(x1, gffn_ref[...]).astype(BF16)
    for c in range(d // LANES):
        h2_ref[:, c, :] = h2[:, c * LANES:(c + 1) * LANES]
    logits = jnp.dot(h2, wr_ref[...], preferred_element_type=F32) + br_ref[...]
    tm = logits.shape[0]
    lane = _lane_iota(logits.shape)
    logits = jnp.where(lane < N_EXPERTS, logits, -jnp.inf)
    wts = jnp.zeros(logits.shape, F32)
    route = jnp.zeros(logits.shape, I32)
    onehot = jnp.zeros(logits.shape, F32)
    den = jnp.zeros((tm, 1), F32)
    picks = []
    v0 = None
    for k in range(TOP_K):
        mx = jnp.max(logits, axis=1, keepdims=True)
        idx = jnp.min(jnp.where(logits == mx, lane, LANES), axis=1, keepdims=True)
        pick = lane == idx
        if v0 is None:
            v0 = mx
        e = jnp.exp(mx - v0)
        wts = jnp.where(lane == k, e, wts)
        route = jnp.where(lane == k, idx, route)
        onehot = jnp.where(pick, 1.0, onehot)
        picks.append(pick)
        den = den + e
        logits = jnp.where(pick, -jnp.inf, logits)
    wts_ref[...] = wts / den
    r_i = lax.broadcasted_iota(I32, (tm, tm), 0)
    c_i = lax.broadcasted_iota(I32, (tm, tm), 1)
    earlier = jnp.where(c_i < r_i, 1.0, 0.0).astype(BF16)
    cum = carry_ref[...] + jnp.dot(earlier, onehot.astype(BF16), preferred_element_type=F32)
    for k in range(TOP_K):
        rank = jnp.sum(jnp.where(picks[k], cum, 0.0), axis=1, keepdims=True).astype(I32)
        route = jnp.where(lane == TOP_K + k, rank, route)
    route_ref[...] = route
    total = carry_ref[...] + jnp.sum(onehot, axis=0, keepdims=True)
    carry_ref[...] = total
    cnt_ref[...] = jnp.broadcast_to(total, cnt_ref.shape)


def _merge(x_p, x_s, oa_p, oa_s, ob_p, ob_s, lw):
    d = x_p.shape[1]
    n_p, n_s = oa_p.shape[0], oa_s.shape[0]
    t_all = n_p + n_s
    tm = _pick_tile(int(np.gcd(n_p, n_s)), 256, 16)
    n_pt = n_p // tm
    w_in = lw['w_in']
    w_gate = w_in[:, w_in.shape[1] - 2 * d:].astype(BF16)
    wr = jnp.pad(lw['w_router'], ((0, 0), (0, LANES - N_EXPERTS))).astype(BF16)
    br = jnp.pad(lw['b_router'], (0, LANES - N_EXPERTS)).reshape(1, LANES).astype(F32)
    row = lambda g: g.reshape(1, -1).astype(F32)
    ins = [x_p, x_s, oa_p, oa_s, ob_p, ob_s, row(lw['g_mix']), w_gate, row(lw['b_gate']),
           lw['w_br_a'].astype(BF16), lw['w_br_b'].astype(BF16), lw['w_out'].astype(BF16), row(lw['g_ffn']), wr, br]
    tok = lambda w: pl.BlockSpec((tm, w), lambda i: (i, 0))
    ptok = lambda w: pl.BlockSpec((tm, w), lambda i: (jnp.minimum(i, n_pt - 1), 0))
    stok = lambda w: pl.BlockSpec((tm, w), lambda i: (jnp.maximum(i - n_pt, 0), 0))
    full = lambda a: pl.BlockSpec(a.shape, lambda i: (0,) * a.ndim)
    in_specs = [ptok(d), stok(d), ptok(WA_Q), stok(WA_Q), ptok(WB), stok(WB)] + [full(a) for a in ins[6:]]
    sds = jax.ShapeDtypeStruct
    slabs = d // LANES
    return pl.pallas_call(
        functools.partial(_merge_kernel, n_prompt_tiles=n_pt), grid=(t_all // tm,), in_specs=in_specs,
        out_specs=[tok(d), pl.BlockSpec((tm, slabs, LANES), lambda i: (i, 0, 0)), tok(LANES), tok(LANES),
                   pl.BlockSpec((8, LANES), lambda i: (0, 0))],
        out_shape=[sds((t_all, d), F32),
                   sds((t_all, slabs, LANES), BF16),
                   sds((t_all, LANES), I32),
                   sds((t_all, LANES), F32),
                   sds((8, LANES), F32)],
        scratch_shapes=[pltpu.VMEM((1, LANES), F32), pltpu.VMEM((tm, WA_Q), BF16), pltpu.VMEM((tm, WB), BF16),
                        pltpu.VMEM((tm, d), F32)],
        compiler_params=_cparams(1), name="merge")(*ins)


def _swiglu(u):
    glu = jnp.minimum(u[:, :D_FF], SWIGLU_LIMIT)
    lin = jnp.clip(u[:, D_FF:], -SWIGLU_LIMIT, SWIGLU_LIMIT)
    return glu * jax.nn.sigmoid(SWIGLU_ALPHA * glu) * (lin + 1.0)


MOE_ROWS = 512


def _route_plan(route, cnt, n_tiles):
    eid = route[:, :TOP_K]
    rank = route[:, TOP_K:2 * TOP_K]
    cnt_e = cnt[0, :N_EXPERTS].astype(I32)
    ntile = (cnt_e + MOE_ROWS - 1) // MOE_ROWS
    tile_end = jnp.cumsum(ntile)
    tile_start = tile_end - ntile
    row_start = tile_start * MOE_ROWS
    onehot = eid[:, :, None] == jnp.arange(N_EXPERTS, dtype=I32)[None, None, :]
    pos = jnp.sum(jnp.where(onehot, row_start[None, None, :], 0), axis=-1) + rank
    g = jnp.arange(n_tiles, dtype=I32)
    used = tile_end[-1]
    g_eff = jnp.minimum(g, used - 1)
    tile_e = jnp.minimum(jnp.sum(g_eff[:, None] >= tile_end[None, :], axis=1), N_EXPERTS - 1).astype(I32)
    rows = jnp.clip(cnt_e[tile_e] - (g - tile_start[tile_e]) * MOE_ROWS, 0, MOE_ROWS)
    rows = jnp.where(g < used, rows, 0).astype(I32)
    return pos.astype(I32), tile_e, rows


def _dispatch_kernel(pos_ref, h_ref, xs_in, xs_ref, sem, *, tm):
    del xs_in

    def issue(t, carry):
        for k in range(TOP_K):
            pltpu.make_async_copy(h_ref.at[t], xs_ref.at[pos_ref[0, 0, t * TOP_K + k]], sem).start(priority=k % 2)
        return carry

    lax.fori_loop(0, tm, issue, 0)
    for k in range(TOP_K):
        pltpu.make_async_copy(h_ref, xs_ref.at[pl.ds(0, tm)], sem).wait()


def _dispatch(h2, pos, n_rows):
    t_all, slabs, _ = h2.shape
    tm = _pick_tile(t_all, 256, 16)
    pos3 = pos.reshape(t_all // tm, 1, tm * TOP_K)
    xs0 = jnp.zeros((n_rows, slabs, LANES), h2.dtype)
    return pl.pallas_call(
        functools.partial(_dispatch_kernel, tm=tm), grid=(t_all // tm,),
        in_specs=[pl.BlockSpec((1, 1, tm * TOP_K), lambda i: (i, 0, 0), memory_space=pltpu.SMEM),
                  pl.BlockSpec((tm, slabs, LANES), lambda i: (i, 0, 0)),
                  pl.BlockSpec(memory_space=pl.ANY)],
        out_specs=pl.BlockSpec(memory_space=pl.ANY),
        out_shape=jax.ShapeDtypeStruct(xs0.shape, xs0.dtype),
        scratch_shapes=[pltpu.SemaphoreType.DMA(())],
        input_output_aliases={2: 0},
        compiler_params=_cparams(1), name="dispatch")(pos3, h2, xs0)


def _experts_kernel(te_ref, rows_ref, xs_ref, wu_ref, bu_ref, wd_ref, bd_ref, ys_ref, wub_ref, wdb_ref):
    g = pl.program_id(0)
    slabs = xs_ref.shape[1]

    @pl.when((g == 0) | (te_ref[g] != te_ref[jnp.maximum(g - 1, 0)]))
    def _():
        wub_ref[...] = wu_ref[0].astype(BF16)
        wdb_ref[...] = wd_ref[0].astype(BF16)

    @pl.when(rows_ref[g] > 0)
    def _():
        half = xs_ref.shape[0] // 2
        for r0 in (0, half):
            x = jnp.concatenate([xs_ref[r0:r0 + half, c, :] for c in range(slabs)], axis=1)
            u = jnp.dot(x, wub_ref[...], preferred_element_type=F32) + bu_ref[0]
            ys_ref[r0:r0 + half, :] = (
                jnp.dot(_swiglu(u).astype(BF16), wdb_ref[...], preferred_element_type=F32) + bd_ref[0])

    @pl.when(rows_ref[g] == 0)
    def _():
        ys_ref[...] = jnp.zeros(ys_ref.shape, F32)


def _experts(xs, tile_e, rows, lw):
    n_rows, slabs, _ = xs.shape
    d = slabs * LANES
    wu = lw['w_up'].astype(F32)
    wd = lw['w_down'].astype(F32)
    bu = lw['b_up'].reshape(N_EXPERTS, 1, 2 * D_FF).astype(F32)
    bd = lw['b_down'].reshape(N_EXPERTS, 1, d).astype(F32)
    tile = pl.BlockSpec((MOE_ROWS, slabs, LANES), lambda g, te, rw: (g, 0, 0))
    ex = lambda a: pl.BlockSpec((1,) + a.shape[1:], lambda g, te, rw: (te[g], 0, 0))
    grid_spec = pltpu.PrefetchScalarGridSpec(
        num_scalar_prefetch=2, grid=(n_rows // MOE_ROWS,),
        in_specs=[tile, ex(wu), ex(bu), ex(wd), ex(bd)],
        out_specs=pl.BlockSpec((MOE_ROWS, d), lambda g, te, rw: (g, 0)),
        scratch_shapes=[pltpu.VMEM(wu.shape[1:], BF16), pltpu.VMEM(wd.shape[1:], BF16)])
    return pl.pallas_call(
        _experts_kernel, grid_spec=grid_spec,
        out_shape=jax.ShapeDtypeStruct((n_rows, d), F32),
        compiler_params=_cparams(1), name="experts")(tile_e, rows, xs, wu, bu, wd, bd)


def _combine_kernel(pos_ref, posn_ref, x1_ref, wts_ref, ys_ref, yp_ref, ys_out_ref, buf_ref, sem,
                    *, tm, n_prompt_tiles):
    i = pl.program_id(0)
    slot = lax.rem(i, 2)

    def fetch(p_ref, sl):
        def issue(t, carry):
            for k in range(TOP_K):
                pltpu.make_async_copy(ys_ref.at[pl.ds(p_ref[0, 0, t * TOP_K + k], 1)],
                                      buf_ref.at[sl, k, pl.ds(t, 1)], sem.at[sl]).start(priority=k % 2)
            return carry

        lax.fori_loop(0, tm, issue, 0)

    @pl.when(i == 0)
    def _():
        fetch(pos_ref, 0)

    @pl.when(i + 1 < pl.num_programs(0))
    def _():
        fetch(posn_ref, 1 - slot)

    for k in range(TOP_K):
        pltpu.make_async_copy(ys_ref.at[pl.ds(0, tm)], buf_ref.at[slot, k], sem.at[slot]).wait()
    w = wts_ref[...]
    acc = x1_ref[...]
    for k in range(TOP_K):
        acc = acc + w[:, k:k + 1] * buf_ref[slot, k]

    @pl.when(pl.program_id(0) < n_prompt_tiles)
    def _():
        yp_ref[...] = acc

    @pl.when(pl.program_id(0) >= n_prompt_tiles)
    def _():
        ys_out_ref[...] = acc


def _combine(x1, wts, ys, pos, n_prompt):
    t_all, d = x1.shape
    n_s = t_all - n_prompt
    tm = _pick_tile(int(np.gcd(n_prompt, n_s)), 256, 16)
    n_pt = n_prompt // tm
    n_t = t_all // tm
    pos3 = pos.reshape(n_t, 1, tm * TOP_K)
    return pl.pallas_call(
        functools.partial(_combine_kernel, tm=tm, n_prompt_tiles=n_pt), grid=(n_t,),
        in_specs=[pl.BlockSpec((1, 1, tm * TOP_K), lambda i: (i, 0, 0), memory_space=pltpu.SMEM),
                  pl.BlockSpec((1, 1, tm * TOP_K), lambda i: (jnp.minimum(i + 1, n_t - 1), 0, 0),
                               memory_space=pltpu.SMEM),
                  pl.BlockSpec((tm, d), lambda i: (i, 0)), pl.BlockSpec((tm, LANES), lambda i: (i, 0)),
                  pl.BlockSpec(memory_space=pl.ANY)],
        out_specs=[pl.BlockSpec((tm, d), lambda i: (jnp.minimum(i, n_pt - 1), 0)),
                   pl.BlockSpec((tm, d), lambda i: (jnp.maximum(i - n_pt, 0), 0))],
        out_shape=[jax.ShapeDtypeStruct((n_prompt, d), F32), jax.ShapeDtypeStruct((n_s, d), F32)],
        scratch_shapes=[pltpu.VMEM((2, TOP_K, tm, d), F32), pltpu.SemaphoreType.DMA((2,))],
        compiler_params=_cparams(1), name="combine")(pos3, pos3, x1, wts, ys)


def _moe(x1, h2, route, wts, cnt, lw, n_prompt):
    t_all = x1.shape[0]
    n_tiles = (TOP_K * t_all) // MOE_ROWS + N_EXPERTS
    pos, tile_e, rows = _route_plan(route, cnt, n_tiles)
    xs = _dispatch(h2, pos, n_tiles * MOE_ROWS)
    ys = _experts(xs, tile_e, rows, lw)
    return _combine(x1, wts, ys, pos, n_prompt)


def _layer(xp, xs, a_k, a_v, a_kidx, b_k, b_v, lw):
    _, s, d = xp.shape
    bs, ts, _ = xs.shape
    p_len = a_k.shape[1]
    t_s = bs * ts
    x_p = xp.reshape(s, d)
    x_s = xs.reshape(t_s, d)

    (q_hm, qi_hm, wi, kaf, vaf, kif, kab, vab, ki2, qb_hm, kbf, vbf, kbb, vbb) = _proj(x_p, x_s, p_len, ts, lw)

    tq_p = _pick_tile(s, 128, CHUNK)
    oa_p = _dsa(q_hm, qi_hm, wi, ki2[None], kab[None], vab[None],
                n_batch=1, tq=tq_p, n_qt=s // tq_p, q_off=0, nk=s, topk=min(TOPK_MAX, s // 4),
                pos_base=0, n_valid=s)
    n_keys = p_len + ts
    nk_s = ((n_keys + 2047) // 2048) * 2048
    pad_s = nk_s - n_keys

    def with_new(cache_bf, new_rows):
        return jnp.concatenate([cache_bf, new_rows.reshape(bs, ts, LANES),
                                jnp.zeros((bs, pad_s, LANES), BF16)], axis=1)

    kidx_c = a_kidx.astype(BF16)
    k_s = with_new(a_k.reshape(bs, p_len, HD_A).astype(BF16), kab[s:])
    v_s = with_new(a_v.reshape(bs, p_len, HD_A).astype(BF16), vab[s:])
    ki2_s = with_new(jnp.concatenate([kidx_c, kidx_c], axis=-1), ki2[s:])
    oa_s = _dsa(q_hm, qi_hm, wi, ki2_s, k_s, v_s,
                n_batch=bs, tq=ts, n_qt=1, q_off=s // ts, nk=nk_s, topk=min(TOPK_MAX, n_keys // 4),
                pos_base=p_len, n_valid=n_keys)

    tq_b = BAND_BACK
    ob_p = _band(qb_hm, kbb[None], kbb[None], vbb[None], vbb[None], lw['rel_bias'],
                 n_batch=1, tq=tq_b, tqo=tq_b, n_qt=s // tq_b, q_off=0, n_own=tq_b, prev_always=False,
                 prev_map=lambda b, i: (0, jnp.maximum(i - 1, 0), 0), own_map=lambda b, i: (0, i, 0))
    own_pad = LANES - ts
    kown_s = jnp.pad(kbb[s:].reshape(bs, ts, WB), ((0, 0), (0, own_pad), (0, 0)))
    vown_s = jnp.pad(vbb[s:].reshape(bs, ts, WB), ((0, 0), (0, own_pad), (0, 0)))
    bk2 = b_k.reshape(bs, BAND_BACK, WB)
    bv2 = b_v.reshape(bs, BAND_BACK, WB)
    ob_s = _band(qb_hm, bk2.astype(BF16), kown_s, bv2.astype(BF16), vown_s, lw['rel_bias'],
                 n_batch=bs, tq=ts, tqo=LANES, n_qt=1, q_off=s // ts, n_own=ts, prev_always=True,
                 prev_map=lambda b, i: (b, 0, 0), own_map=lambda b, i: (b, 0, 0))

    x1, h2, route, wts, cnt = _merge(x_p, x_s, oa_p, oa_s, ob_p, ob_s, lw)
    y_p, y_s = _moe(x1, h2, route, wts, cnt, lw, s)

    keep = min(BAND_BACK, s)
    st_p = (kaf[:s].reshape(1, s, 1, HD_A), vaf[:s].reshape(1, s, 1, HD_A), kif[:s, :D_IDX].reshape(1, s, D_IDX),
            kbf[s - keep:s].reshape(1, keep, N_HEADS_B, HD_B), vbf[s - keep:s].reshape(1, keep, N_HEADS_B, HD_B))
    kb_new = kbf[s:].reshape(bs, ts, N_HEADS_B, HD_B)
    vb_new = vbf[s:].reshape(bs, ts, N_HEADS_B, HD_B)
    st_s = (kaf[s:].reshape(bs, ts, 1, HD_A), vaf[s:].reshape(bs, ts, 1, HD_A),
            kif[s:, :D_IDX].reshape(bs, ts, D_IDX),
            jnp.concatenate([b_k, kb_new], axis=1)[:, ts:], jnp.concatenate([b_v, vb_new], axis=1)[:, ts:])
    return y_p.reshape(1, s, d), y_s.reshape(bs, ts, d), st_p, st_s


def kernel(x_prompt, x_sample, cache_a_k, cache_a_v, cache_a_kidx, state_b_k, state_b_v,
           g_mix, w_in, b_gate, g_qa, g_ka, g_ki, g_qb, g_kb, rel_bias, w_br_a, w_br_b, w_out,
           g_ffn, w_router, b_router, w_up, b_up, w_down, b_down):
    assert x_prompt.shape[0] == 1, "prompt batch is folded into the token axis; one stream supported"
    depth = g_mix.shape[0]
    yp, ys = x_prompt, x_sample
    states_p, states_s = [], []
    for l in range(depth):
        lw = dict(g_mix=g_mix[l], w_in=w_in[l], b_gate=b_gate[l], g_qa=g_qa[l], g_ka=g_ka[l], g_ki=g_ki[l],
                  g_qb=g_qb[l], g_kb=g_kb[l], rel_bias=rel_bias[l], w_br_a=w_br_a[l], w_br_b=w_br_b[l],
                  w_out=w_out[l], g_ffn=g_ffn[l], w_router=w_router[l], b_router=b_router[l],
                  w_up=w_up[l], b_up=b_up[l], w_down=w_down[l], b_down=b_down[l])
        yp, ys, st_p, st_s = _layer(yp, ys, cache_a_k[l], cache_a_v[l], cache_a_kidx[l],
                                    state_b_k[l], state_b_v[l], lw)
        states_p.append(st_p)
        states_s.append(st_s)
    a_k_p, a_v_p, a_ki_p, b_k_p, b_v_p = [jnp.stack(t) for t in zip(*states_p)]
    a_k_s, a_v_s, a_ki_s, b_k_s, b_v_s = [jnp.stack(t) for t in zip(*states_s)]
    return (yp, ys, a_k_p, a_v_p, a_ki_p, b_k_p, b_v_p, a_k_s, a_v_s, a_ki_s, b_k_s, b_v_s)
```

```python
import functools

import numpy as np
import jax
import jax.numpy as jnp
from jax import lax
from jax.experimental import pallas as pl
from jax.experimental.pallas import tpu as pltpu

F32 = jnp.float32
BF16 = jnp.bfloat16
I32 = jnp.int32

CHUNK = 64
CHUNK_SHIFT = 6
N_HEADS_A = 8
HD_A = 128
N_HEADS_IDX = 8
D_IDX = 64
TOPK_MAX = 256
N_HEADS_B = 8
HD_B = 64
N_PREV_CHUNKS = 8
BAND_BACK = N_PREV_CHUNKS * CHUNK
MAX_REL = 128
N_EXPERTS = 32
TOP_K = 4
D_FF = 1024
SWIGLU_LIMIT = 7.0
SWIGLU_ALPHA = 1.702
ROPE_THETA = 10000.0
EPS = 1e-6
NEG = -1e30
IDX_SCALE = (D_IDX ** -0.5) * (N_HEADS_IDX ** -0.5)
LOG2E = 1.4426950408889634
QK_SCALE_LOG2E = (HD_A ** -0.5) * LOG2E

LANES = 128
VMEM_LIMIT_BYTES = 56 * 1024 * 1024

WA_Q = N_HEADS_A * HD_A
WI_Q = N_HEADS_IDX * D_IDX
WB = N_HEADS_B * HD_B


def _pick_tile(n, target, mult):
    best = None
    for t in range(mult, min(n, target) + 1, mult):
        if n % t == 0:
            best = t
    return best if best is not None else n


def _cparams(n_axes):
    return pltpu.CompilerParams(dimension_semantics=("arbitrary",) * n_axes,
                                vmem_limit_bytes=VMEM_LIMIT_BYTES)


def _lane_iota(shape):
    return lax.broadcasted_iota(I32, shape, len(shape) - 1)


def _rms(x, g):
    ms = jnp.mean(x * x, axis=-1, keepdims=True)
    return x * lax.rsqrt(ms + EPS) * g


_C_QA = 0
_C_KA = _C_QA + WA_Q
_C_VA = _C_KA + HD_A
_C_QI = _C_VA + HD_A
_C_KI = _C_QI + WI_Q
_C_WI = _C_KI + LANES
_C_QB = _C_WI + LANES
_C_KB = _C_QB + WB
_C_VB = _C_KB + WB
_C_END = _C_VB + WB


def _proj_kernel(xp_ref, xs_ref, gmix_ref, w_ref, gqa_ref, gka_ref, gki_ref, gqb_ref, gkb_ref,
                 bca_ref, bsa_ref, oca_ref, osa_ref, bci_ref, bsi_ref, oci_ref, osi_ref,
                 q_ref, qi_ref, wi_ref, kaf_ref, vaf_ref, kif_ref, kab_ref, vab_ref, ki2_ref,
                 qb_ref, kbf_ref, vbf_ref, kbb_ref, vbb_ref, x_ref, *, n_prompt_tiles):
    @pl.when(pl.program_id(0) < n_prompt_tiles)
    def _():
        x_ref[...] = xp_ref[...]

    @pl.when(pl.program_id(0) >= n_prompt_tiles)
    def _():
        x_ref[...] = xs_ref[...]

    x = x_ref[...]
    h = _rms(x, gmix_ref[...]).astype(BF16)

    def seg(a, b):
        return jnp.dot(h, w_ref[:, a:b], preferred_element_type=F32)

    lane = _lane_iota((x.shape[0], LANES))
    lo_half = lane < HD_B
    ca = bca_ref[0] * oca_ref[0] - bsa_ref[0] * osa_ref[0]
    sa = bsa_ref[0] * oca_ref[0] + bca_ref[0] * osa_ref[0]
    cosa = ca
    sina = jnp.where(lo_half, -sa, sa)
    ci = bci_ref[0] * oci_ref[0] - bsi_ref[0] * osi_ref[0]
    si = bsi_ref[0] * oci_ref[0] + bci_ref[0] * osi_ref[0]
    first_half = jnp.bitwise_and(lane, D_IDX - 1) < D_IDX // 2
    cosi = ci
    sinia = jnp.where(first_half, -si, 0.0)
    sinib = jnp.where(first_half, 0.0, si)

    def rope_a(n):
        return n * cosa + pltpu.roll(n, HD_A // 2, 1) * sina

    def rope_i(n):
        return n * cosi + pltpu.roll(n, LANES - D_IDX // 2, 1) * sinia + pltpu.roll(n, D_IDX // 2, 1) * sinib

    z = seg(_C_QA, _C_KA)
    gqa = gqa_ref[...]
    for hd in range(N_HEADS_A):
        zh = z[:, hd * HD_A:(hd + 1) * HD_A]
        q_ref[hd] = (rope_a(_rms(zh, gqa)) * QK_SCALE_LOG2E).astype(BF16)

    ka = rope_a(_rms(seg(_C_KA, _C_VA), gka_ref[...]))
    kaf_ref[...] = ka
    kab_ref[...] = ka.astype(BF16)
    va = seg(_C_VA, _C_QI)
    vaf_ref[...] = va
    vab_ref[...] = va.astype(BF16)

    zk = seg(_C_KI, _C_WI)
    ms = jnp.sum(zk * zk, axis=-1, keepdims=True) * (1.0 / D_IDX)
    ki = rope_i(zk * lax.rsqrt(ms + EPS) * gki_ref[...])
    kif_ref[...] = ki
    ki2_ref[...] = (ki + pltpu.roll(ki, D_IDX, 1)).astype(BF16)

    z = seg(_C_QI, _C_KI)
    for p in range(N_HEADS_IDX // 2):
        r = rope_i(z[:, p * LANES:(p + 1) * LANES])
        qi_ref[2 * p] = jnp.where(lo_half, r, 0.0).astype(BF16)
        qi_ref[2 * p + 1] = jnp.where(lo_half, 0.0, r).astype(BF16)

    wi_ref[...] = seg(_C_WI, _C_QB) * IDX_SCALE

    def norm_b(zb, g):
        sq = zb * zb
        s_all = jnp.sum(sq, axis=-1, keepdims=True)
        s_lo = jnp.sum(jnp.where(lo_half, sq, 0.0), axis=-1, keepdims=True)
        r_lo = lax.rsqrt(s_lo * (1.0 / HD_B) + EPS)
        r_hi = lax.rsqrt((s_all - s_lo) * (1.0 / HD_B) + EPS)
        return zb * jnp.where(lo_half, r_lo, r_hi) * g

    z = seg(_C_QB, _C_KB)
    gqb = gqb_ref[...]
    for p in range(N_HEADS_B // 2):
        n = norm_b(z[:, p * LANES:(p + 1) * LANES], gqb)
        qb_ref[2 * p] = jnp.where(lo_half, n, 0.0).astype(BF16)
        qb_ref[2 * p + 1] = jnp.where(lo_half, 0.0, n).astype(BF16)
    z = seg(_C_KB, _C_VB)
    gkb = gkb_ref[...]
    for p in range(N_HEADS_B // 2):
        n = norm_b(z[:, p * LANES:(p + 1) * LANES], gkb)
        kbf_ref[:, p * LANES:(p + 1) * LANES] = n
        kbb_ref[:, p * LANES:(p + 1) * LANES] = n.astype(BF16)
    z = seg(_C_VB, _C_END)
    vbf_ref[...] = z
    vbb_ref[...] = z.astype(BF16)


def _proj(x_p, x_s, p_len, ts, lw):
    n_p, d = x_p.shape
    n_s = x_s.shape[0]
    t_all = n_p + n_s
    tm = _pick_tile(int(np.gcd(n_p, n_s)), 256, 16)
    assert tm % ts == 0, "a sample tile holds whole streams"
    n_pt, n_t = n_p // tm, t_all // tm
    w_in = lw['w_in']
    offs = np.cumsum((WA_Q, HD_A, HD_A, WI_Q, D_IDX, N_HEADS_IDX, WB, WB, WB))
    qa_w, ka_w, va_w, qi_w, ki_w, wi_w, qb_w, kb_w, vb_w = [
        w_in[:, a:b] for a, b in zip(np.concatenate([[0], offs[:-1]]), offs)]

    def padl(w):
        return jnp.pad(w, ((0, 0), (0, LANES - w.shape[1])))

    w_pack = jnp.concatenate([qa_w, ka_w, va_w, qi_w, padl(ki_w), padl(wi_w), qb_w, kb_w, vb_w],
                             axis=1).astype(BF16)

    base = np.concatenate([np.arange(n_pt) * tm, np.full(n_t - n_pt, p_len)]).astype(np.float32)
    offs_rows = np.stack([np.arange(tm), np.arange(tm) % ts]).astype(np.float32)

    def tables(dh):
        inv = ROPE_THETA ** (-jnp.arange(0, dh, 2, dtype=F32) / dh)
        inv = jnp.tile(inv, LANES // inv.shape[0])
        ang_b = jnp.asarray(base)[:, None, None] * inv[None, None, :]
        ang_o = jnp.asarray(offs_rows)[:, :, None] * inv[None, None, :]
        return jnp.cos(ang_b), jnp.sin(ang_b), jnp.cos(ang_o), jnp.sin(ang_o)

    rope_tabs = list(tables(HD_A)) + list(tables(D_IDX))

    row = lambda g: g.reshape(1, -1).astype(F32)
    gki = jnp.pad(lw['g_ki'], (0, LANES - D_IDX)).reshape(1, LANES)
    gqb = jnp.tile(lw['g_qb'], 2).reshape(1, LANES)
    gkb = jnp.tile(lw['g_kb'], 2).reshape(1, LANES)

    tok = lambda w: pl.BlockSpec((tm, w), lambda i: (i, 0))
    full = lambda a: pl.BlockSpec(a.shape, lambda i: (0,) * a.ndim)
    hm = pl.BlockSpec((N_HEADS_A, tm, LANES), lambda i: (0, i, 0))
    base_spec = pl.BlockSpec((1, 1, LANES), lambda i: (i, 0, 0))
    offs_spec = pl.BlockSpec((1, tm, LANES), lambda i: (jnp.where(i < n_pt, 0, 1), 0, 0))

    ins = [x_p, x_s, row(lw['g_mix']), w_pack, row(lw['g_qa']), row(lw['g_ka']), gki, gqb, gkb] + rope_tabs
    in_specs = [pl.BlockSpec((tm, d), lambda i: (jnp.minimum(i, n_pt - 1), 0)),
                pl.BlockSpec((tm, d), lambda i: (jnp.maximum(i - n_pt, 0), 0)),
                full(ins[2]), full(w_pack), full(ins[4]), full(ins[5]), full(gki), full(gqb), full(gkb),
                base_spec, base_spec, offs_spec, offs_spec, base_spec, base_spec, offs_spec, offs_spec]
    sds = jax.ShapeDtypeStruct
    out_shape = [
        sds((N_HEADS_A, t_all, LANES), BF16),
        sds((N_HEADS_IDX, t_all, LANES), BF16),
        sds((t_all, LANES), F32),
        sds((t_all, HD_A), F32), sds((t_all, HD_A), F32), sds((t_all, LANES), F32),
        sds((t_all, HD_A), BF16), sds((t_all, HD_A), BF16), sds((t_all, LANES), BF16),
        sds((N_HEADS_B, t_all, LANES), BF16),
        sds((t_all, WB), F32), sds((t_all, WB), F32), sds((t_all, WB), BF16), sds((t_all, WB), BF16),
    ]
    out_specs = [hm, hm, tok(LANES), tok(HD_A), tok(HD_A), tok(LANES), tok(HD_A), tok(HD_A), tok(LANES),
                 hm, tok(WB), tok(WB), tok(WB), tok(WB)]
    return pl.pallas_call(
        functools.partial(_proj_kernel, n_prompt_tiles=n_pt), grid=(n_t,), in_specs=in_specs,
        out_specs=out_specs, out_shape=out_shape, scratch_shapes=[pltpu.VMEM((tm, d), F32)],
        compiler_params=_cparams(1), name="proj")(*ins)


BIS_UNROLL = 4
BIS_MAX_ROUNDS = 80
BIG = 1e38
CAND_DEPTH = 16


def _batcher_pairs(n):
    pairs = []
    p = 1
    while p < n:
        k = p
        while k >= 1:
            for j in range(k % p, n - k, 2 * k):
                for i in range(min(k, n - j - k)):
                    if (i + j) // (2 * p) == (i + j + k) // (2 * p):
                        pairs.append((i + j, i + j + k))
            k //= 2
        p *= 2
    return pairs


def _bitonic_pairs(n):
    pairs = []
    stride = n // 2
    while stride >= 1:
        pairs += [(i, i + stride) for i in range(n) if not i & stride]
        stride //= 2
    return pairs


_SORT16 = _batcher_pairs(CAND_DEPTH)
_BITONIC16 = _bitonic_pairs(CAND_DEPTH)


def _dsa_kernel(q_ref, qi_ref, wi_ref, ki2_ref, k_ref, v_ref, o_ref,
                keys_ref, wb_ref, lohi_ref, cnt_ref, m_ref, alpha_ref, acc_ref, s_ref, p_ref, tiec_ref,
                cand_ref, done_ref, *, tq, tk, nkt_max, topk, pos_base, n_valid):
    i = pl.program_id(1)
    pos0 = pos_base + i * tq
    k_end = ((pos0 + tq - 1) // CHUNK + 1) * CHUNK
    k_lim = jnp.minimum(k_end, n_valid)
    nkt4 = jnp.minimum(((k_lim + 4 * tk - 1) // (4 * tk)) * 4, nkt_max)
    ncol = tk // LANES
    nh = N_HEADS_A
    topk_f = float(topk)

    qrow = pos0 + lax.broadcasted_iota(I32, (tq, LANES), 0)
    qchunk = lax.shift_right_logical(qrow, CHUNK_SHIFT)
    lane = _lane_iota((tq, LANES))

    klim = jnp.minimum(lax.shift_left(qchunk + 1, CHUNK_SHIFT), n_valid)

    def admissible(j, c):
        return (j * tk + c * LANES + lane) < klim

    w = wi_ref[...]
    for h in range(N_HEADS_IDX):
        wb_ref[h] = jnp.broadcast_to(w[:, h:h + 1], (tq, LANES))
    qi2d = qi_ref[...].reshape(N_HEADS_IDX * tq, LANES)

    dn_t = (((1,), (1,)), ((), ()))

    def idx_dots(j):
        kt = ki2_ref[0, pl.ds(pl.multiple_of(j * tk, tk), tk), :]
        return lax.dot_general(qi2d, kt, dn_t, preferred_element_type=F32)

    def score_tile(j, slot):
        for c in range(ncol):
            tot = None
            for h in range(N_HEADS_IDX):
                r = jnp.maximum(s_ref[slot, h * tq:(h + 1) * tq, c * LANES:(c + 1) * LANES], 0.0)
                term = wb_ref[h] * r
                tot = term if tot is None else tot + term
            keys_ref[j, :, c * LANES:(c + 1) * LANES] = jnp.where(admissible(j, c), tot, NEG)

    s_ref[0] = idx_dots(0)
    s_ref[1] = idx_dots(1)

    def score_quad(it, carry):
        t0 = 4 * it
        s_ref[2] = idx_dots(t0 + 2)
        s_ref[3] = idx_dots(t0 + 3)
        score_tile(t0, 0)
        score_tile(t0 + 1, 1)
        s_ref[0] = idx_dots(jnp.minimum(t0 + 4, nkt4 - 2))
        s_ref[1] = idx_dots(jnp.minimum(t0 + 5, nkt4 - 1))
        score_tile(t0 + 2, 2)
        score_tile(t0 + 3, 3)
        return carry

    lax.fori_loop(0, nkt4 // 4, score_quad, 0)

    enough = klim.astype(F32) >= topk_f

    def build_candidates(qd, carry):
        for g in range(tq // 8):
            rows = slice(8 * g, 8 * g + 8)
            new = [keys_ref[4 * qd + t, rows, c * LANES:(c + 1) * LANES] for t in range(4) for c in range(ncol)]
            for a, b in _SORT16:
                new[a], new[b] = jnp.maximum(new[a], new[b]), jnp.minimum(new[a], new[b])
            top = [jnp.maximum(cand_ref[rows, b * LANES:(b + 1) * LANES], new[CAND_DEPTH - 1 - b])
                   for b in range(CAND_DEPTH)]
            for a, b in _BITONIC16:
                top[a], top[b] = jnp.maximum(top[a], top[b]), jnp.minimum(top[a], top[b])
            for b in range(CAND_DEPTH):
                cand_ref[rows, b * LANES:(b + 1) * LANES] = top[b]
        return carry

    cand_ref[...] = jnp.full(cand_ref.shape, -BIG, F32)
    lax.fori_loop(0, nkt4 // 4, build_candidates, 0)

    def count_cand(thr, strict):
        acc = jnp.zeros((tq, LANES), F32)
        for b in range(CAND_DEPTH):
            x = cand_ref[:, b * LANES:(b + 1) * LANES]
            acc = acc + jnp.where((x > thr) if strict else (x >= thr), 1.0, 0.0)
        return jnp.sum(acc, axis=1, keepdims=True)

    def count_keys(thr, strict):
        def tile(j, acc):
            for cc in range(ncol):
                x = keys_ref[j, :, cc * LANES:(cc + 1) * LANES]
                acc = acc + jnp.where((x > thr) if strict else (x >= thr), 1.0, 0.0)
            return acc
        return jnp.sum(lax.fori_loop(0, nkt4, tile, jnp.zeros((tq, LANES), F32)), axis=1, keepdims=True)

    zeros = jnp.zeros((tq, LANES), F32)
    rmax = jnp.max(cand_ref[:, 0:LANES], axis=1, keepdims=True)
    hi0 = jnp.where(enough, rmax + jnp.maximum(jnp.abs(rmax), 1e-30) * 1e-6, BIG) + zeros
    head = None
    for b in range(-(-topk // LANES)):
        x = cand_ref[:, b * LANES:(b + 1) * LANES]
        x = jnp.where(x > 0.5 * NEG, x, BIG)
        head = x if head is None else jnp.minimum(head, x)
    lo_try = jnp.min(head, axis=1, keepdims=True) + zeros

    def search(count):
        lo0 = jnp.where(enough & (count(lo_try, False) >= topk_f), lo_try, -BIG)
        lohi_ref[0] = lo0
        lohi_ref[1] = hi0
        cnt_ref[0] = jnp.where(enough, count(lo0, False), topk_f) + zeros
        cnt_ref[1] = zeros

        def unresolved():
            lo = lohi_ref[0]
            hi = lohi_ref[1]
            mid = lo + (hi - lo) * 0.5
            return (cnt_ref[0] != topk_f) & (mid > lo) & (mid < hi)

        def step():
            lo = lohi_ref[0]
            hi = lohi_ref[1]
            mid = lo + (hi - lo) * 0.5
            active = (cnt_ref[0] != topk_f) & (mid > lo) & (mid < hi) & (done_ref[...] == 0.0)
            cnt = count(mid, False)
            up = active & (cnt >= topk_f)
            dn = active & (cnt < topk_f)
            lohi_ref[0] = jnp.where(up, mid, lo)
            lohi_ref[1] = jnp.where(dn, mid, hi)
            cnt_ref[0] = jnp.where(up, cnt, cnt_ref[0])
            cnt_ref[1] = jnp.where(dn, cnt, cnt_ref[1])

        def body(c):
            it, _ = c
            for _ in range(BIS_UNROLL):
                step()
            ties_only = count(lohi_ref[0], True) == cnt_ref[1]
            done = jnp.where(unresolved() & jnp.logical_not(ties_only), 0.0, 1.0)
            done_ref[...] = done
            return it + 1, jnp.min(done)

        done_ref[...] = jnp.zeros((tq, LANES), F32)
        lax.while_loop(lambda c: (c[1] < 0.5) & (c[0] < BIS_MAX_ROUNDS), body, (jnp.int32(0), jnp.float32(0.0)))

    search(count_cand)
    full_lo = count_keys(lohi_ref[0], False)
    full_hi = count_keys(lohi_ref[1], False)
    agree = jnp.logical_not(enough) | ((full_lo == cnt_ref[0]) & (full_hi == cnt_ref[1]))

    @pl.when(jnp.min(jnp.where(agree, 1.0, 0.0)) < 0.5)
    def _():
        search(count_keys)

    lo = lohi_ref[0]
    hi = lohi_ref[1]
    need = topk_f - cnt_ref[1]
    tie_any = jnp.max(jnp.where(cnt_ref[0] > topk_f, 1.0, 0.0))

    m_ref[...] = jnp.full(m_ref.shape, NEG, F32)
    acc_ref[...] = jnp.zeros(acc_ref.shape, F32)
    tiec_ref[...] = jnp.zeros(tiec_ref.shape, F32)
    q2d = q_ref[...].reshape(nh * tq, LANES)
    ones_col = jnp.where(_lane_iota((tk, LANES)) == 0, 1.0, 0.0).astype(BF16)

    def qk_dots(j):
        kt = k_ref[0, pl.ds(pl.multiple_of(j * tk, tk), tk), :]
        return lax.dot_general(q2d, kt, dn_t, preferred_element_type=F32)

    def softmax_tile(j, slot, tie):
        if tie:
            kk = keys_ref[j]
            cand = [(kk[:, c * LANES:(c + 1) * LANES] >= lo) & (kk[:, c * LANES:(c + 1) * LANES] < hi)
                    for c in range(ncol)]
            candf = jnp.concatenate([jnp.where(cd, 1.0, 0.0) for cd in cand], axis=1)
            r_i = lax.broadcasted_iota(I32, (tk, tk), 0)
            c_i = lax.broadcasted_iota(I32, (tk, tk), 1)
            upper = jnp.where(r_i < c_i, 1.0, 0.0).astype(BF16)
            pref = jnp.dot(candf.astype(BF16), upper, preferred_element_type=F32)
            base = tiec_ref[...]
            sel = []
            for c in range(ncol):
                kc = kk[:, c * LANES:(c + 1) * LANES]
                rank = base + pref[:, c * LANES:(c + 1) * LANES]
                sel.append(((kc >= hi) | (cand[c] & (rank < need))) & admissible(j, c))
            tiec_ref[...] = base + jnp.sum(candf, axis=1, keepdims=True)
        else:
            rb = min(tq, 32)
            lane_r = _lane_iota((rb, LANES))
            for r0 in range(0, tq, rb):
                rs = slice(r0, r0 + rb)
                lo_r = lohi_ref[0, rs]
                qc_r = lax.shift_right_logical(pos0 + r0 + lax.broadcasted_iota(I32, (rb, LANES), 0), CHUNK_SHIFT)
                klim_r = jnp.minimum(lax.shift_left(qc_r + 1, CHUNK_SHIFT), n_valid)
                sel = []
                for c in range(ncol):
                    adm = (j * tk + c * LANES + lane_r) < klim_r
                    sel.append((keys_ref[j, rs, c * LANES:(c + 1) * LANES] >= lo_r) & adm)
                for h in range(nh):
                    hr = slice(h * tq + r0, h * tq + r0 + rb)
                    m_prev = m_ref[h, rs]
                    xs = [jnp.where(sel[c], s_ref[slot, hr, c * LANES:(c + 1) * LANES], NEG) for c in range(ncol)]
                    m_cur = xs[0]
                    for c in range(1, ncol):
                        m_cur = jnp.maximum(m_cur, xs[c])
                    m_new = jnp.maximum(m_prev, jnp.max(m_cur, axis=1, keepdims=True))
                    alpha_ref[slot, h, rs] = jnp.exp2(m_prev - m_new)
                    for c in range(ncol):
                        p_ref[slot, hr, c * LANES:(c + 1) * LANES] = jnp.exp2(xs[c] - m_new).astype(BF16)
                    m_ref[h, rs] = m_new
            return
        for h in range(nh):
            m_prev = m_ref[h]
            xs = [jnp.where(sel[c], s_ref[slot, h * tq:(h + 1) * tq, c * LANES:(c + 1) * LANES], NEG)
                  for c in range(ncol)]
            m_cur = xs[0]
            for c in range(1, ncol):
                m_cur = jnp.maximum(m_cur, xs[c])
            m_new = jnp.maximum(m_prev, jnp.max(m_cur, axis=1, keepdims=True))
            alpha_ref[slot, h] = jnp.exp2(m_prev - m_new)
            for c in range(ncol):
                p_ref[slot, h * tq:(h + 1) * tq, c * LANES:(c + 1) * LANES] = (
                    jnp.exp2(xs[c] - m_new).astype(BF16))
            m_ref[h] = m_new

    def pv_tile(j, slot):
        vt = v_ref[0, pl.ds(pl.multiple_of(j * tk, tk), tk), :]
        pv = jnp.dot(p_ref[slot], jnp.concatenate([vt, ones_col], axis=1), preferred_element_type=F32)
        for h in range(nh):
            alpha = alpha_ref[slot, h]
            for half in range(2):
                hs = slice(half * HD_A, (half + 1) * HD_A)
                acc_ref[h, :, hs] = acc_ref[h, :, hs] * alpha + pv[h * tq:(h + 1) * tq, hs]

    def attend(tie):
        npair = jnp.minimum((k_lim + 2 * tk - 1) // (2 * tk), nkt_max // 2)
        s_ref[0] = qk_dots(0)
        p_ref[1] = jnp.zeros(p_ref.shape[1:], BF16)
        alpha_ref[1] = jnp.ones(alpha_ref.shape[1:], F32)

        def pair(jj, carry):
            a = 2 * jj
            s_ref[1] = qk_dots(a + 1)
            softmax_tile(a, 0, tie)
            pv_tile(jnp.maximum(a - 1, 0), 1)
            s_ref[0] = qk_dots(jnp.minimum(a + 2, 2 * npair - 2))
            softmax_tile(a + 1, 1, tie)
            pv_tile(a, 0)
            return carry

        lax.fori_loop(0, npair, pair, 0)
        pv_tile(2 * npair - 1, 1)

    @pl.when(tie_any == 0)
    def _():
        attend(False)

    @pl.when(tie_any != 0)
    def _():
        attend(True)

    for h in range(nh):
        den = acc_ref[h, :, HD_A:HD_A + 1]
        o_ref[:, h * HD_A:(h + 1) * HD_A] = (acc_ref[h, :, :HD_A] / den).astype(o_ref.dtype)


def _dsa(q_hm, qi_hm, wi, ki2, k, v, *, n_batch, tq, n_qt, q_off, nk, topk, pos_base, n_valid):
    tk = _pick_tile(nk, 512, LANES)
    nkt_max = nk // tk
    assert nkt_max % 4 == 0 and 4 * (tk // LANES) == CAND_DEPTH and nk >= topk, "key tiles are merged in fours"
    kern = functools.partial(_dsa_kernel, tq=tq, tk=tk, nkt_max=nkt_max, topk=topk,
                             pos_base=pos_base, n_valid=n_valid)
    qmap = lambda b, i: (0, q_off + b * n_qt + i, 0)
    rmap = lambda b, i: (q_off + b * n_qt + i, 0)
    kmap = lambda b, i: (b, 0, 0)
    in_specs = [pl.BlockSpec((N_HEADS_A, tq, LANES), qmap), pl.BlockSpec((N_HEADS_IDX, tq, LANES), qmap),
                pl.BlockSpec((tq, LANES), rmap),
                pl.BlockSpec((1, nk, LANES), kmap), pl.BlockSpec((1, nk, LANES), kmap),
                pl.BlockSpec((1, nk, LANES), kmap)]
    scratch = [
        pltpu.VMEM((nkt_max, tq, tk), F32),
        pltpu.VMEM((N_HEADS_IDX, tq, LANES), F32),
        pltpu.VMEM((2, tq, LANES), F32),
        pltpu.VMEM((2, tq, LANES), F32),
        pltpu.VMEM((N_HEADS_A, tq, LANES), F32),
        pltpu.VMEM((2, N_HEADS_A, tq, LANES), F32),
        pltpu.VMEM((N_HEADS_A, tq, 2 * HD_A), F32),
        pltpu.VMEM((4, N_HEADS_A * tq, tk), F32),
        pltpu.VMEM((2, N_HEADS_A * tq, tk), BF16),
        pltpu.VMEM((tq, LANES), F32),
        pltpu.VMEM((tq, CAND_DEPTH * LANES), F32),
        pltpu.VMEM((tq, LANES), F32),
    ]
    return pl.pallas_call(
        kern, grid=(n_batch, n_qt), in_specs=in_specs,
        out_specs=pl.BlockSpec((tq, WA_Q), lambda b, i: (b * n_qt + i, 0)),
        out_shape=jax.ShapeDtypeStruct((n_batch * n_qt * tq, WA_Q), BF16),
        scratch_shapes=scratch,
        compiler_params=_cparams(2), name="dsa")(q_hm, qi_hm, wi, ki2, k, v)


def _band_kernel(q_ref, kp_ref, ko_ref, vp_ref, vo_ref, rext_ref, o_ref, bias_ref,
                 *, tq, tqo, n_own, off, prev_always):
    b = pl.program_id(0)
    i = pl.program_id(1)
    w = BAND_BACK + tqo
    scale = HD_B ** -0.5

    @pl.when((b == 0) & (i == 0))
    def _():
        ri = lax.broadcasted_iota(I32, (tq, w), 0)
        ci = lax.broadcasted_iota(I32, (tq, w), 1)
        qc = lax.shift_right_logical(ri, CHUNK_SHIFT)
        jo = ci - BAND_BACK
        valid_prev = (ci < BAND_BACK) & (lax.shift_right_logical(ci, CHUNK_SHIFT) >= qc)
        valid_own = (jo >= 0) & (jo < n_own) & (lax.shift_right_logical(jnp.maximum(jo, 0), CHUNK_SHIFT) <= qc)
        valid = valid_prev | valid_own
        for h in range(N_HEADS_B):
            pat = jnp.broadcast_to(rext_ref[h:h + 1, :], (tq, off + w))
            rolled = pltpu.roll(pat, 0, 1, stride=1, stride_axis=0)
            bias_ref[h] = jnp.where(valid, rolled[:, off:off + w], NEG)

    dead_cols = 0 if prev_always else jnp.where(i > 0, 0, BAND_BACK)
    lane = _lane_iota((tq, LANES))
    lo_half = lane < HD_B
    prev_dead = lax.broadcasted_iota(I32, (tq, w), 1) < dead_cols
    for p in range(N_HEADS_B // 2):
        sl = slice(p * LANES, (p + 1) * LANES)
        kcat = jnp.concatenate([kp_ref[0, :, sl], ko_ref[0, :, sl]], axis=0)
        vcat = jnp.concatenate([vp_ref[0, :, sl], vo_ref[0, :, sl]], axis=0)
        outs = []
        for e in range(2):
            h = 2 * p + e
            s = lax.dot_general(q_ref[h], kcat, (((1,), (1,)), ((), ())), preferred_element_type=F32)
            s = s * scale + bias_ref[h]
            s = jnp.where(prev_dead, NEG, s)
            m = jnp.max(s, axis=1, keepdims=True)
            pexp = jnp.exp(s - m)
            den = jnp.sum(pexp, axis=1, keepdims=True)
            pv = jnp.dot(pexp.astype(BF16), vcat, preferred_element_type=F32)
            outs.append(pv / den)
        o_ref[:, sl] = jnp.where(lo_half, outs[0], outs[1]).astype(o_ref.dtype)


def _band(qb_hm, kprev, kown, vprev, vown, rel_bias, *, n_batch, tq, tqo, n_qt, q_off, n_own,
          prev_always, prev_map, own_map):
    off = max(tq, LANES)
    off = ((off + LANES - 1) // LANES) * LANES
    w = BAND_BACK + tqo
    u = np.arange(off + w)
    idx = np.clip(BAND_BACK + off - u, -MAX_REL, MAX_REL) + MAX_REL
    rext = rel_bias.astype(F32)[:, idx]
    kern = functools.partial(_band_kernel, tq=tq, tqo=tqo, n_own=n_own, off=off, prev_always=prev_always)
    qmap = lambda b, i: (0, q_off + b * n_qt + i, 0)
    in_specs = [pl.BlockSpec((N_HEADS_B, tq, LANES), qmap),
                pl.BlockSpec((1, BAND_BACK, WB), prev_map), pl.BlockSpec((1, tqo, WB), own_map),
                pl.BlockSpec((1, BAND_BACK, WB), prev_map), pl.BlockSpec((1, tqo, WB), own_map),
                pl.BlockSpec(rext.shape, lambda b, i: (0, 0))]
    return pl.pallas_call(
        kern, grid=(n_batch, n_qt), in_specs=in_specs,
        out_specs=pl.BlockSpec((tq, WB), lambda b, i: (b * n_qt + i, 0)),
        out_shape=jax.ShapeDtypeStruct((n_batch * n_qt * tq, WB), BF16),
        scratch_shapes=[pltpu.VMEM((N_HEADS_B, tq, w), F32)],
        compiler_params=_cparams(2), name="band")(qb_hm, kprev, kown, vprev, vown, rext)


def _merge_kernel(xp_ref, xs_ref, oap_ref, oas_ref, obp_ref, obs_ref, gmix_ref, wg_ref, bg_ref, wa_ref, wb_ref,
                  wo_ref, gffn_ref, wr_ref, br_ref, x1_ref, h2_ref, route_ref, wts_ref, cnt_ref,
                  carry_ref, oa_ref, ob_ref, x_ref, *, n_prompt_tiles):
    @pl.when(pl.program_id(0) == 0)
    def _():
        carry_ref[...] = jnp.zeros(carry_ref.shape, F32)

    @pl.when(pl.program_id(0) < n_prompt_tiles)
    def _():
        x_ref[...] = xp_ref[...]
        oa_ref[...] = oap_ref[...]
        ob_ref[...] = obp_ref[...]

    @pl.when(pl.program_id(0) >= n_prompt_tiles)
    def _():
        x_ref[...] = xs_ref[...]
        oa_ref[...] = oas_ref[...]
        ob_ref[...] = obs_ref[...]

    x = x_ref[...]
    d = x.shape[1]
    h = _rms(x, gmix_ref[...]).astype(BF16)
    gates = jax.nn.sigmoid(jnp.dot(h, wg_ref[...], preferred_element_type=F32) + bg_ref[...])

    ya = jnp.dot(oa_ref[...], wa_ref[...], preferred_element_type=F32)
    yb = jnp.dot(ob_ref[...], wb_ref[...], preferred_element_type=F32)
    m = gates[:, :d] * ya + gates[:, d:] * yb
    x1 = x + jnp.dot(m.astype(BF16), wo_ref[...], preferred_element_type=F32)
    x1_ref[...] = x1
    h2 = _rms(x1, gffn_ref[...]).astype(BF16)
    for c in range(d // LANES):
        h2_ref[:, c, :] = h2[:, c * LANES:(c + 1) * LANES]
    logits = jnp.dot(h2, wr_ref[...], preferred_element_type=F32) + br_ref[...]
    tm = logits.shape[0]
    lane = _lane_iota(logits.shape)
    logits = jnp.where(lane < N_EXPERTS, logits, -jnp.inf)
    wts = jnp.zeros(logits.shape, F32)
    route = jnp.zeros(logits.shape, I32)
    onehot = jnp.zeros(logits.shape, F32)
    den = jnp.zeros((tm, 1), F32)
    picks = []
    v0 = None
    for k in range(TOP_K):
        mx = jnp.max(logits, axis=1, keepdims=True)
        idx = jnp.min(jnp.where(logits == mx, lane, LANES), axis=1, keepdims=True)
        pick = lane == idx
        if v0 is None:
            v0 = mx
        e = jnp.exp(mx - v0)
        wts = jnp.where(lane == k, e, wts)
        route = jnp.where(lane == k, idx, route)
        onehot = jnp.where(pick, 1.0, onehot)
        picks.append(pick)
        den = den + e
        logits = jnp.where(pick, -jnp.inf, logits)
    wts_ref[...] = wts / den
    r_i = lax.broadcasted_iota(I32, (tm, tm), 0)
    c_i = lax.broadcasted_iota(I32, (tm, tm), 1)
    earlier = jnp.where(c_i < r_i, 1.0, 0.0).astype(BF16)
    cum = carry_ref[...] + jnp.dot(earlier, onehot.astype(BF16), preferred_element_type=F32)
    for k in range(TOP_K):
        rank = jnp.sum(jnp.where(picks[k], cum, 0.0), axis=1, keepdims=True).astype(I32)
        route = jnp.where(lane == TOP_K + k, rank, route)
    route_ref[...] = route
    total = carry_ref[...] + jnp.sum(onehot, axis=0, keepdims=True)
    carry_ref[...] = total
    cnt_ref[...] = jnp.broadcast_to(total, cnt_ref.shape)


def _merge(x_p, x_s, oa_p, oa_s, ob_p, ob_s, lw):
    d = x_p.shape[1]
    n_p, n_s = oa_p.shape[0], oa_s.shape[0]
    t_all = n_p + n_s
    tm = _pick_tile(int(np.gcd(n_p, n_s)), 256, 16)
    n_pt = n_p // tm
    w_in = lw['w_in']
    w_gate = w_in[:, w_in.shape[1] - 2 * d:].astype(BF16)
    wr = jnp.pad(lw['w_router'], ((0, 0), (0, LANES - N_EXPERTS))).astype(BF16)
    br = jnp.pad(lw['b_router'], (0, LANES - N_EXPERTS)).reshape(1, LANES).astype(F32)
    row = lambda g: g.reshape(1, -1).astype(F32)
    ins = [x_p, x_s, oa_p, oa_s, ob_p, ob_s, row(lw['g_mix']), w_gate, row(lw['b_gate']),
           lw['w_br_a'].astype(BF16), lw['w_br_b'].astype(BF16), lw['w_out'].astype(BF16), row(lw['g_ffn']), wr, br]
    tok = lambda w: pl.BlockSpec((tm, w), lambda i: (i, 0))
    ptok = lambda w: pl.BlockSpec((tm, w), lambda i: (jnp.minimum(i, n_pt - 1), 0))
    stok = lambda w: pl.BlockSpec((tm, w), lambda i: (jnp.maximum(i - n_pt, 0), 0))
    full = lambda a: pl.BlockSpec(a.shape, lambda i: (0,) * a.ndim)
    in_specs = [ptok(d), stok(d), ptok(WA_Q), stok(WA_Q), ptok(WB), stok(WB)] + [full(a) for a in ins[6:]]
    sds = jax.ShapeDtypeStruct
    slabs = d // LANES
    return pl.pallas_call(
        functools.partial(_merge_kernel, n_prompt_tiles=n_pt), grid=(t_all // tm,), in_specs=in_specs,
        out_specs=[tok(d), pl.BlockSpec((tm, slabs, LANES), lambda i: (i, 0, 0)), tok(LANES), tok(LANES),
                   pl.BlockSpec((8, LANES), lambda i: (0, 0))],
        out_shape=[sds((t_all, d), F32),
                   sds((t_all, slabs, LANES), BF16),
                   sds((t_all, LANES), I32),
                   sds((t_all, LANES), F32),
                   sds((8, LANES), F32)],
        scratch_shapes=[pltpu.VMEM((1, LANES), F32), pltpu.VMEM((tm, WA_Q), BF16), pltpu.VMEM((tm, WB), BF16),
                        pltpu.VMEM((tm, d), F32)],
        compiler_params=_cparams(1), name="merge")(*ins)


def _swiglu(u):
    glu = jnp.minimum(u[:, :D_FF], SWIGLU_LIMIT)
    lin = jnp.clip(u[:, D_FF:], -SWIGLU_LIMIT, SWIGLU_LIMIT)
    return glu * jax.nn.sigmoid(SWIGLU_ALPHA * glu) * (lin + 1.0)


MOE_ROWS = 512


def _route_plan(route, cnt, n_tiles):
    eid = route[:, :TOP_K]
    rank = route[:, TOP_K:2 * TOP_K]
    cnt_e = cnt[0, :N_EXPERTS].astype(I32)
    ntile = (cnt_e + MOE_ROWS - 1) // MOE_ROWS
    tile_end = jnp.cumsum(ntile)
    tile_start = tile_end - ntile
    row_start = tile_start * MOE_ROWS
    onehot = eid[:, :, None] == jnp.arange(N_EXPERTS, dtype=I32)[None, None, :]
    pos = jnp.sum(jnp.where(onehot, row_start[None, None, :], 0), axis=-1) + rank
    g = jnp.arange(n_tiles, dtype=I32)
    used = tile_end[-1]
    g_eff = jnp.minimum(g, used - 1)
    tile_e = jnp.minimum(jnp.sum(g_eff[:, None] >= tile_end[None, :], axis=1), N_EXPERTS - 1).astype(I32)
    rows = jnp.clip(cnt_e[tile_e] - (g - tile_start[tile_e]) * MOE_ROWS, 0, MOE_ROWS)
    rows = jnp.where(g < used, rows, 0).astype(I32)
    return pos.astype(I32), tile_e, rows


def _dispatch_kernel(pos_ref, h_ref, xs_in, xs_ref, sem, *, tm):
    del xs_in

    def issue(t, carry):
        for k in range(TOP_K):
            pltpu.make_async_copy(h_ref.at[t], xs_ref.at[pos_ref[0, 0, t * TOP_K + k]], sem).start(priority=k % 2)
        return carry

    lax.fori_loop(0, tm, issue, 0, unroll=4)
    for k in range(TOP_K):
        pltpu.make_async_copy(h_ref, xs_ref.at[pl.ds(0, tm)], sem).wait()


def _dispatch(h2, pos, n_rows):
    t_all, slabs, _ = h2.shape
    tm = _pick_tile(t_all, 256, 16)
    pos3 = pos.reshape(t_all // tm, 1, tm * TOP_K)
    xs0 = jnp.zeros((n_rows, slabs, LANES), h2.dtype)
    return pl.pallas_call(
        functools.partial(_dispatch_kernel, tm=tm), grid=(t_all // tm,),
        in_specs=[pl.BlockSpec((1, 1, tm * TOP_K), lambda i: (i, 0, 0), memory_space=pltpu.SMEM),
                  pl.BlockSpec((tm, slabs, LANES), lambda i: (i, 0, 0)),
                  pl.BlockSpec(memory_space=pl.ANY)],
        out_specs=pl.BlockSpec(memory_space=pl.ANY),
        out_shape=jax.ShapeDtypeStruct(xs0.shape, xs0.dtype),
        scratch_shapes=[pltpu.SemaphoreType.DMA(())],
        input_output_aliases={2: 0},
        compiler_params=_cparams(1), name="dispatch")(pos3, h2, xs0)


def _experts_kernel(te_ref, rows_ref, xs_ref, wu_ref, bu_ref, wd_ref, bd_ref, ys_ref, wub_ref, wdb_ref):
    g = pl.program_id(0)
    slabs = xs_ref.shape[1]

    @pl.when((g == 0) | (te_ref[g] != te_ref[jnp.maximum(g - 1, 0)]))
    def _():
        wub_ref[...] = wu_ref[0].astype(BF16)
        wdb_ref[...] = wd_ref[0].astype(BF16)

    @pl.when(rows_ref[g] > 0)
    def _():
        half = xs_ref.shape[0] // 2
        for r0 in (0, half):
            x = jnp.concatenate([xs_ref[r0:r0 + half, c, :] for c in range(slabs)], axis=1)
            u = jnp.dot(x, wub_ref[...], preferred_element_type=F32) + bu_ref[0]
            ys_ref[r0:r0 + half, :] = (
                jnp.dot(_swiglu(u).astype(BF16), wdb_ref[...], preferred_element_type=F32) + bd_ref[0])

    @pl.when(rows_ref[g] == 0)
    def _():
        ys_ref[...] = jnp.zeros(ys_ref.shape, F32)


def _experts(xs, tile_e, rows, lw):
    n_rows, slabs, _ = xs.shape
    d = slabs * LANES
    wu = lw['w_up'].astype(F32)
    wd = lw['w_down'].astype(F32)
    bu = lw['b_up'].reshape(N_EXPERTS, 1, 2 * D_FF).astype(F32)
    bd = lw['b_down'].reshape(N_EXPERTS, 1, d).astype(F32)
    tile = pl.BlockSpec((MOE_ROWS, slabs, LANES), lambda g, te, rw: (g, 0, 0))
    ex = lambda a: pl.BlockSpec((1,) + a.shape[1:], lambda g, te, rw: (te[g], 0, 0))
    grid_spec = pltpu.PrefetchScalarGridSpec(
        num_scalar_prefetch=2, grid=(n_rows // MOE_ROWS,),
        in_specs=[tile, ex(wu), ex(bu), ex(wd), ex(bd)],
        out_specs=pl.BlockSpec((MOE_ROWS, d), lambda g, te, rw: (g, 0)),
        scratch_shapes=[pltpu.VMEM(wu.shape[1:], BF16), pltpu.VMEM(wd.shape[1:], BF16)])
    return pl.pallas_call(
        _experts_kernel, grid_spec=grid_spec,
        out_shape=jax.ShapeDtypeStruct((n_rows, d), F32),
        compiler_params=_cparams(1), name="experts")(tile_e, rows, xs, wu, bu, wd, bd)


def _combine_kernel(pos_ref, posn_ref, x1_ref, wts_ref, ys_ref, yp_ref, ys_out_ref, buf_ref, sem,
                    *, tm, n_prompt_tiles):
    i = pl.program_id(0)
    slot = lax.rem(i, 2)

    def fetch(p_ref, sl):
        def issue(t, carry):
            for k in range(TOP_K):
                pltpu.make_async_copy(ys_ref.at[pl.ds(p_ref[0, 0, t * TOP_K + k], 1)],
                                      buf_ref.at[sl, k, pl.ds(t, 1)], sem.at[sl]).start(priority=k % 2)
            return carry

        lax.fori_loop(0, tm, issue, 0, unroll=4)

    @pl.when(i == 0)
    def _():
        fetch(pos_ref, 0)

    @pl.when(i + 1 < pl.num_programs(0))
    def _():
        fetch(posn_ref, 1 - slot)

    for k in range(TOP_K):
        pltpu.make_async_copy(ys_ref.at[pl.ds(0, tm)], buf_ref.at[slot, k], sem.at[slot]).wait()
    w = wts_ref[...]
    acc = x1_ref[...]
    for k in range(TOP_K):
        acc = acc + w[:, k:k + 1] * buf_ref[slot, k]

    @pl.when(pl.program_id(0) < n_prompt_tiles)
    def _():
        yp_ref[...] = acc

    @pl.when(pl.program_id(0) >= n_prompt_tiles)
    def _():
        ys_out_ref[...] = acc


def _combine(x1, wts, ys, pos, n_prompt):
    t_all, d = x1.shape
    n_s = t_all - n_prompt
    tm = _pick_tile(int(np.gcd(n_prompt, n_s)), 256, 16)
    n_pt = n_prompt // tm
    n_t = t_all // tm
    pos3 = pos.reshape(n_t, 1, tm * TOP_K)
    return pl.pallas_call(
        functools.partial(_combine_kernel, tm=tm, n_prompt_tiles=n_pt), grid=(n_t,),
        in_specs=[pl.BlockSpec((1, 1, tm * TOP_K), lambda i: (i, 0, 0), memory_space=pltpu.SMEM),
                  pl.BlockSpec((1, 1, tm * TOP_K), lambda i: (jnp.minimum(i + 1, n_t - 1), 0, 0),
                               memory_space=pltpu.SMEM),
                  pl.BlockSpec((tm, d), lambda i: (i, 0)), pl.BlockSpec((tm, LANES), lambda i: (i, 0)),
                  pl.BlockSpec(memory_space=pl.ANY)],
        out_specs=[pl.BlockSpec((tm, d), lambda i: (jnp.minimum(i, n_pt - 1), 0)),
                   pl.BlockSpec((tm, d), lambda i: (jnp.maximum(i - n_pt, 0), 0))],
        out_shape=[jax.ShapeDtypeStruct((n_prompt, d), F32), jax.ShapeDtypeStruct((n_s, d), F32)],
        scratch_shapes=[pltpu.VMEM((2, TOP_K, tm, d), F32), pltpu.SemaphoreType.DMA((2,))],
        compiler_params=_cparams(1), name="combine")(pos3, pos3, x1, wts, ys)


def _moe(x1, h2, route, wts, cnt, lw, n_prompt):
    t_all = x1.shape[0]
    n_tiles = (TOP_K * t_all) // MOE_ROWS + N_EXPERTS
    pos, tile_e, rows = _route_plan(route, cnt, n_tiles)
    xs = _dispatch(h2, pos, n_tiles * MOE_ROWS)
    ys = _experts(xs, tile_e, rows, lw)
    return _combine(x1, wts, ys, pos, n_prompt)


def _layer(xp, xs, a_k, a_v, a_kidx, b_k, b_v, lw):
    _, s, d = xp.shape
    bs, ts, _ = xs.shape
    p_len = a_k.shape[1]
    t_s = bs * ts
    x_p = xp.reshape(s, d)
    x_s = xs.reshape(t_s, d)

    (q_hm, qi_hm, wi, kaf, vaf, kif, kab, vab, ki2, qb_hm, kbf, vbf, kbb, vbb) = _proj(x_p, x_s, p_len, ts, lw)

    tq_p = _pick_tile(s, 128, CHUNK)
    oa_p = _dsa(q_hm, qi_hm, wi, ki2[None], kab[None], vab[None],
                n_batch=1, tq=tq_p, n_qt=s // tq_p, q_off=0, nk=s, topk=min(TOPK_MAX, s // 4),
                pos_base=0, n_valid=s)
    n_keys = p_len + ts
    nk_s = ((n_keys + 2047) // 2048) * 2048
    pad_s = nk_s - n_keys

    def with_new(cache_bf, new_rows):
        return jnp.concatenate([cache_bf, new_rows.reshape(bs, ts, LANES),
                                jnp.zeros((bs, pad_s, LANES), BF16)], axis=1)

    kidx_c = a_kidx.astype(BF16)
    k_s = with_new(a_k.reshape(bs, p_len, HD_A).astype(BF16), kab[s:])
    v_s = with_new(a_v.reshape(bs, p_len, HD_A).astype(BF16), vab[s:])
    ki2_s = with_new(jnp.concatenate([kidx_c, kidx_c], axis=-1), ki2[s:])
    oa_s = _dsa(q_hm, qi_hm, wi, ki2_s, k_s, v_s,
                n_batch=bs, tq=ts, n_qt=1, q_off=s // ts, nk=nk_s, topk=min(TOPK_MAX, n_keys // 4),
                pos_base=p_len, n_valid=n_keys)

    tq_b = BAND_BACK
    ob_p = _band(qb_hm, kbb[None], kbb[None], vbb[None], vbb[None], lw['rel_bias'],
                 n_batch=1, tq=tq_b, tqo=tq_b, n_qt=s // tq_b, q_off=0, n_own=tq_b, prev_always=False,
                 prev_map=lambda b, i: (0, jnp.maximum(i - 1, 0), 0), own_map=lambda b, i: (0, i, 0))
    own_pad = LANES - ts
    kown_s = jnp.pad(kbb[s:].reshape(bs, ts, WB), ((0, 0), (0, own_pad), (0, 0)))
    vown_s = jnp.pad(vbb[s:].reshape(bs, ts, WB), ((0, 0), (0, own_pad), (0, 0)))
    bk2 = b_k.reshape(bs, BAND_BACK, WB)
    bv2 = b_v.reshape(bs, BAND_BACK, WB)
    ob_s = _band(qb_hm, bk2.astype(BF16), kown_s, bv2.astype(BF16), vown_s, lw['rel_bias'],
                 n_batch=bs, tq=ts, tqo=LANES, n_qt=1, q_off=s // ts, n_own=ts, prev_always=True,
                 prev_map=lambda b, i: (b, 0, 0), own_map=lambda b, i: (b, 0, 0))

    x1, h2, route, wts, cnt = _merge(x_p, x_s, oa_p, oa_s, ob_p, ob_s, lw)
    y_p, y_s = _moe(x1, h2, route, wts, cnt, lw, s)

    keep = min(BAND_BACK, s)
    st_p = (kaf[:s].reshape(1, s, 1, HD_A), vaf[:s].reshape(1, s, 1, HD_A), kif[:s, :D_IDX].reshape(1, s, D_IDX),
            kbf[s - keep:s].reshape(1, keep, N_HEADS_B, HD_B), vbf[s - keep:s].reshape(1, keep, N_HEADS_B, HD_B))
    kb_new = kbf[s:].reshape(bs, ts, N_HEADS_B, HD_B)
    vb_new = vbf[s:].reshape(bs, ts, N_HEADS_B, HD_B)
    st_s = (kaf[s:].reshape(bs, ts, 1, HD_A), vaf[s:].reshape(bs, ts, 1, HD_A),
            kif[s:, :D_IDX].reshape(bs, ts, D_IDX),
            jnp.concatenate([b_k, kb_new], axis=1)[:, ts:], jnp.concatenate([b_v, vb_new], axis=1)[:, ts:])
    return y_p.reshape(1, s, d), y_s.reshape(bs, ts, d), st_p, st_s


def kernel(x_prompt, x_sample, cache_a_k, cache_a_v, cache_a_kidx, state_b_k, state_b_v,
           g_mix, w_in, b_gate, g_qa, g_ka, g_ki, g_qb, g_kb, rel_bias, w_br_a, w_br_b, w_out,
           g_ffn, w_router, b_router, w_up, b_up, w_down, b_down):
    assert x_prompt.shape[0] == 1, "prompt batch is folded into the token axis; one stream supported"
    depth = g_mix.shape[0]
    yp, ys = x_prompt, x_sample
    states_p, states_s = [], []
    for l in range(depth):
        lw = dict(g_mix=g_mix[l], w_in=w_in[l], b_gate=b_gate[l], g_qa=g_qa[l], g_ka=g_ka[l], g_ki=g_ki[l],
                  g_qb=g_qb[l], g_kb=g_kb[l], rel_bias=rel_bias[l], w_br_a=w_br_a[l], w_br_b=w_br_b[l],
                  w_out=w_out[l], g_ffn=g_ffn[l], w_router=w_router[l], b_router=b_router[l],
                  w_up=w_up[l], b_up=b_up[l], w_down=w_down[l], b_down=b_down[l])
        yp, ys, st_p, st_s = _layer(yp, ys, cache_a_k[l], cache_a_v[l], cache_a_kidx[l],
                                    state_b_k[l], state_b_v[l], lw)
        states_p.append(st_p)
        states_s.append(st_s)
    a_k_p, a_v_p, a_ki_p, b_k_p, b_v_p = [jnp.stack(t) for t in zip(*states_p)]
    a_k_s, a_v_s, a_ki_s, b_k_s, b_v_s = [jnp.stack(t) for t in zip(*states_s)]
    return (yp, ys, a_k_p, a_v_p, a_ki_p, b_k_p, b_v_p, a_k_s, a_v_s, a_ki_s, b_k_s, b_v_s)
```
